```python
import math
import jax, jax.numpy as jnp
from jax import lax
import numpy as np

D_MODEL = 2048
BATCH = 8
SEQ = 2048
DEPTH = 1
DEC_BATCH = 32
DEC_SEQ = 4
PAST_LEN = 8192
PAGE_SIZE = 128

HEAD_DIM = 128
NSA_HEADS = 8
NSA_KV_GROUPS = 2
NSA_HPG = NSA_HEADS // NSA_KV_GROUPS
CMP_BLOCK = 32
CMP_STRIDE = 16
SLC_BLOCK = 64
N_SELECT = 16
WINDOW = 512
FOX_HEADS = 8
NSA_Q = NSA_HEADS * HEAD_DIM
NSA_KV = NSA_KV_GROUPS * HEAD_DIM
FOX_W = FOX_HEADS * HEAD_DIM
MIX_WIDTH = NSA_Q + FOX_W
IN_SIZES = (NSA_Q, 6 * NSA_KV, 3 * NSA_HEADS, FOX_W, 2 * FOX_W, FOX_HEADS)
IN_WIDTH = sum(IN_SIZES)
N_BUCKETS = 32
MAX_DISTANCE = 128
PEER_HEADS = 8
N_KEYS = 128
N_EXPERTS = N_KEYS * N_KEYS
PEER_TOPK = 16
PEER_QDIM = 256
Q_BLOCK = 128
SLC_Q_BLOCK = 32
TOKEN_CHUNK = 128
ALPHA = (2.0 * DEPTH) ** 0.25
BETA = (8.0 * DEPTH) ** -0.25
LN_EPS = 1e-5
NEG = -1e30
FORCE_BONUS = 1e4
FORGET_BIAS_INIT = 3.0

kernel_name = 'hymba_nsa_fox_peer_deepnorm_step'


def layer_norm(x, g, b):
    xf = x.astype(jnp.float32)
    mu = jnp.mean(xf, -1, keepdims=True)
    var = jnp.mean(jnp.square(xf - mu), -1, keepdims=True)
    return ((xf - mu) * lax.rsqrt(var + LN_EPS) * g + b).astype(x.dtype)


def rms_norm(x, g):
    xf = x.astype(jnp.float32)
    return (xf * lax.rsqrt(jnp.mean(xf * xf, -1, keepdims=True) + LN_EPS) * g).astype(x.dtype)


def masked_softmax(s, mask):
    s = jnp.where(mask, s, NEG)
    m = jnp.max(s, -1, keepdims=True)
    p = jnp.where(mask, jnp.exp(s - m), 0.0)
    return p / jnp.maximum(jnp.sum(p, -1, keepdims=True), 1e-30)


def t5_bucket(dist):
    max_exact = N_BUCKETS // 2
    d = jnp.maximum(dist, 0)
    large = max_exact + (jnp.log(jnp.maximum(d, 1).astype(jnp.float32) / max_exact)
                         / math.log(MAX_DISTANCE / max_exact) * (N_BUCKETS - max_exact)).astype(jnp.int32)
    return jnp.where(d < max_exact, d, jnp.minimum(large, N_BUCKETS - 1))


def head_bias(table, dist):
    b = jnp.moveaxis(table[t5_bucket(dist)].astype(jnp.float32), -1, 0)
    return b.reshape((NSA_KV_GROUPS, NSA_HPG) + dist.shape)


def query_block(t, cap):
    return t if t <= cap else cap


def to_blocks(x, axis, qb):
    n = x.shape[axis] // qb
    return jnp.moveaxis(x.reshape(x.shape[:axis] + (n, qb) + x.shape[axis + 1:]), axis, 0)


def from_blocks(y):
    y = jnp.moveaxis(y, 0, 1)
    return y.reshape((y.shape[0], y.shape[1] * y.shape[2]) + y.shape[3:])


def gather_pages(pool, page_table):
    g = pool[page_table]
    return g.reshape((g.shape[0], g.shape[1] * g.shape[2]) + g.shape[3:])


def value_column_scale():
    nsa = np.ones((3, 2, NSA_KV), np.float32)
    nsa[:, 1] = BETA
    fox = np.ones((2, FOX_W), np.float32)
    fox[1] = BETA
    return np.concatenate([np.ones(NSA_Q, np.float32), nsa.ravel(), np.ones(3 * NSA_HEADS, np.float32),
                           np.ones(FOX_W, np.float32), fox.ravel(), np.ones(FOX_HEADS, np.float32)])


def cmp_to_slc(n_cmp, n_slc):
    c0 = np.arange(n_cmp) * CMP_STRIDE
    s0 = np.arange(n_slc) * SLC_BLOCK
    ov = np.minimum(c0[:, None] + CMP_BLOCK, s0[None, :] + SLC_BLOCK) - np.maximum(c0[:, None], s0[None, :])
    return jnp.asarray(np.maximum(ov, 0) / CMP_STRIDE, dtype=jnp.float32)


def project(x, w_in, b_forget):
    B, T, _ = x.shape
    offs = [int(o) for o in np.cumsum(IN_SIZES)[:-1]]
    q_n, kv_n, gate_n, q_f, kv_f, f_f = jnp.split(x @ w_in, offs, axis=-1)
    qn = q_n.reshape(B, T, NSA_HEADS, HEAD_DIM)
    kvn = kv_n.reshape(B, T, 3, 2, NSA_KV_GROUPS, HEAD_DIM)
    gates = jax.nn.sigmoid(gate_n).reshape(B, T, NSA_HEADS, 3)
    qf = q_f.reshape(B, T, FOX_HEADS, HEAD_DIM)
    kvf = kv_f.reshape(B, T, 2, FOX_HEADS, HEAD_DIM)
    logf = jax.nn.log_sigmoid((f_f + b_forget).astype(jnp.float32))
    return qn, kvn, gates, qf, kvf, logf


def compress(k, pos, w1, w2):
    B, T, G, dk = k.shape
    n_cmp = (T - CMP_BLOCK) // CMP_STRIDE + 1
    halves = k[:, :(n_cmp + 1) * CMP_STRIDE].reshape(B, n_cmp + 1, CMP_STRIDE, G, dk)
    pe = pos.reshape(2, CMP_STRIDE, 1, dk)
    w1r = w1.reshape(2, CMP_STRIDE, dk, w1.shape[-1])
    h = (jnp.einsum('bnsgd,sdh->bngh', halves[:, :-1] + pe[0], w1r[0])
         + jnp.einsum('bnsgd,sdh->bngh', halves[:, 1:] + pe[1], w1r[1]))
    return jax.nn.gelu(h) @ w2


def nsa(q, nsa_full, win_ext, gates, bias_table, cmp_pos, cmp_w1, cmp_w2):
    B, Tq = q.shape[:2]
    Tk = nsa_full.shape[1]
    p0 = Tk - Tq
    dt = q.dtype
    scale = HEAD_DIM ** -0.5
    G, HPG = NSA_KV_GROUPS, NSA_HPG
    qg = q.reshape(B, Tq, G, HPG, HEAD_DIM)
    q_pos = p0 + jnp.arange(Tq)

    kc = compress(nsa_full[:, :, 0, 0], cmp_pos[0], cmp_w1[0], cmp_w2[0])
    vc = compress(nsa_full[:, :, 0, 1], cmp_pos[1], cmp_w1[1], cmp_w2[1])
    n_cmp = kc.shape[1]
    dist_c = q_pos[:, None] - (jnp.arange(n_cmp) * CMP_STRIDE + CMP_BLOCK - 1)[None, :]
    s_c = jnp.einsum('bqghd,bngd->bghqn', qg, kc).astype(jnp.float32) * scale + head_bias(bias_table, dist_c)
    p_c = masked_softmax(s_c, dist_c >= 0)
    o_c = jnp.einsum('bghqn,bngd->bqghd', p_c.astype(dt), vc)

    n_slc = -(-Tk // SLC_BLOCK)
    imp = jnp.einsum('bghqn,nj->bgqj', p_c, cmp_to_slc(n_cmp, n_slc))
    blk = jnp.arange(n_slc)[None, :]
    cur = (q_pos // SLC_BLOCK)[:, None]
    forced = (blk == 0) | (blk == cur) | (blk == cur - 1)
    imp = jnp.where(blk * SLC_BLOCK <= q_pos[:, None], imp + FORCE_BONUS * forced, NEG)
    _, sel = lax.top_k(imp, min(N_SELECT, n_slc))
    tok = (sel[..., None] * SLC_BLOCK + jnp.arange(SLC_BLOCK)).reshape(B, G, Tq, -1)
    kv_s = jnp.pad(nsa_full[:, :, 1], ((0, 0), (0, n_slc * SLC_BLOCK - Tk), (0, 0), (0, 0), (0, 0)))
    kv_s = kv_s.transpose(0, 3, 1, 2, 4)
    b_ix = jnp.arange(B)[:, None, None, None]
    g_ix = jnp.arange(G)[None, :, None, None]
    table_g = bias_table.reshape(N_BUCKETS, G, HPG)

    def slc_block(args):
        qc, tc, pc = args
        kv = kv_s[b_ix, g_ix, tc]
        dist = pc[None, None, :, None] - tc
        bias = jnp.moveaxis(table_g[t5_bucket(dist), g_ix].astype(jnp.float32), -1, 2)
        s = jnp.einsum('bqghd,bgqnd->bghqn', qc, kv[..., 0, :]).astype(jnp.float32) * scale + bias
        p = masked_softmax(s, (dist >= 0)[:, :, None])
        return jnp.einsum('bghqn,bgqnd->bqghd', p.astype(dt), kv[..., 1, :])

    qs = query_block(Tq, SLC_Q_BLOCK)
    o_s = from_blocks(lax.map(slc_block, (to_blocks(qg, 1, qs), to_blocks(tok, 2, qs), q_pos.reshape(-1, qs))))

    qw = query_block(Tq, Q_BLOCK)

    def win_block(args):
        c, qc = args
        kv = lax.dynamic_slice_in_dim(win_ext, c * qw, WINDOW + qw, axis=1)
        pos = p0 + c * qw + jnp.arange(qw)
        kpos = p0 - WINDOW + c * qw + jnp.arange(WINDOW + qw)
        dist = pos[:, None] - kpos[None, :]
        mask = (dist >= 0) & (dist <= WINDOW) & (kpos >= 0)[None, :]
        s = jnp.einsum('bqghd,bkgd->bghqk', qc, kv[:, :, 0]).astype(jnp.float32) * scale + head_bias(bias_table, dist)
        p = masked_softmax(s, mask)
        return jnp.einsum('bghqk,bkgd->bqghd', p.astype(dt), kv[:, :, 1])

    o_w = from_blocks(lax.map(win_block, (jnp.arange(Tq // qw), to_blocks(qg, 1, qw))))

    g = gates.reshape(B, Tq, G, HPG, 3)
    o = g[..., 0:1] * o_c + g[..., 1:2] * o_s + g[..., 2:3] * o_w
    return o.reshape(B, Tq, NSA_Q)


def fox(q, kv, logf):
    B, Tq = q.shape[:2]
    Tk = kv.shape[1]
    p0 = Tk - Tq
    dt = q.dtype
    scale = HEAD_DIM ** -0.5
    cum = jnp.cumsum(logf.astype(jnp.float32), axis=1)
    cum_k = jnp.moveaxis(cum, 1, 2)
    k_all, v_all = kv[:, :, 0], kv[:, :, 1]
    kpos = jnp.arange(Tk)

    def blk(args):
        qc, cq, pos = args
        s = jnp.einsum('bqhd,bkhd->bhqk', qc, k_all).astype(jnp.float32) * scale
        s = s + jnp.moveaxis(cq, 1, 2)[..., None] - cum_k[:, :, None, :]
        p = masked_softmax(s, kpos[None, :] <= pos[:, None])
        return jnp.einsum('bhqk,bkhd->bqhd', p.astype(dt), v_all)

    qb = query_block(Tq, Q_BLOCK)
    o = from_blocks(lax.map(blk, (to_blocks(q, 1, qb), to_blocks(cum[:, p0:], 1, qb),
                                  (p0 + jnp.arange(Tq)).reshape(-1, qb))))
    return o.reshape(B, Tq, FOX_W)


def peer(x, w_q, sub_keys, u_table, v_table):
    n, d = x.shape
    c = min(TOKEN_CHUNK, n)
    n_pad = -(-n // c) * c
    xp = jnp.pad(x, ((0, n_pad - n), (0, 0)))

    def chunk(xc):
        qh = (xc @ w_q).reshape(c, PEER_HEADS, 2, PEER_QDIM // 2)
        s = jnp.einsum('chpd,hpkd->chpk', qh, sub_keys).astype(jnp.float32)
        sv, si = lax.top_k(s, PEER_TOPK)
        cand = (sv[:, :, 0, :, None] + sv[:, :, 1, None, :]).reshape(c, PEER_HEADS, PEER_TOPK * PEER_TOPK)
        best, bi = lax.top_k(cand, PEER_TOPK)
        i1 = jnp.take_along_axis(si[:, :, 0], bi // PEER_TOPK, axis=-1)
        i2 = jnp.take_along_axis(si[:, :, 1], bi % PEER_TOPK, axis=-1)
        experts = i1 * N_KEYS + i2
        gate = jax.nn.softmax(best, axis=-1)
        act = jax.nn.gelu(jnp.einsum('cd,chkd->chk', xc, u_table[experts]).astype(jnp.float32))
        return jnp.einsum('chk,chkd->cd', (gate * act).astype(x.dtype), v_table[experts])

    y = lax.map(chunk, xp.reshape(n_pad // c, c, d))
    return y.reshape(n_pad, d)[:n]


def layer_forward(x, past, w_in, b_forget, cmp_pos, cmp_w1, cmp_w2, bias_table, g_nsa, g_fox, w_out,
                  ln1_g, ln1_b, peer_w_q, peer_keys, peer_u, peer_v, ln2_g, ln2_b):
    B, T, D = x.shape
    qn, kvn, gates, qf, kvf, logf = project(x, w_in, b_forget)
    nsa_rows, win_rows = kvn[:, :, :2], kvn[:, :, 2]
    if past is None:
        nsa_full, fox_full, logf_full = nsa_rows, kvf, logf
        win_ext = jnp.pad(win_rows, ((0, 0), (WINDOW, 0), (0, 0), (0, 0), (0, 0)))
        buf_len = min(WINDOW, T)
    else:
        nsa_past, win_past, fox_past, logf_past = past
        buf_len = win_past.shape[1]
        nsa_full = jnp.concatenate([nsa_past, nsa_rows], axis=1)
        fox_full = jnp.concatenate([fox_past, kvf], axis=1)
        logf_full = jnp.concatenate([logf_past.astype(jnp.float32), logf], axis=1)
        pad = jnp.zeros((B, WINDOW - buf_len) + win_rows.shape[2:], win_rows.dtype)
        win_ext = jnp.concatenate([pad, win_past, win_rows], axis=1)
    o_n = nsa(qn, nsa_full, win_ext, gates, bias_table, cmp_pos, cmp_w1, cmp_w2)
    o_f = fox(qf, fox_full, logf_full)
    mix = jnp.concatenate([rms_norm(o_n, g_nsa), rms_norm(o_f, g_fox)], axis=-1) @ w_out
    h = layer_norm(ALPHA * x + mix, ln1_g, ln1_b)
    f = peer(h.reshape(B * T, D), peer_w_q, peer_keys, peer_u, peer_v).reshape(B, T, D)
    y = layer_norm(ALPHA * h + f, ln2_g, ln2_b)
    return y, nsa_rows, win_ext[:, win_ext.shape[1] - buf_len:], kvf, logf


def setup_inputs(seed: int = 0) -> dict:
    key = jax.random.key(seed)
    ks = iter(jax.random.split(key, 32))

    def nrm(shape, scale):
        return jax.random.normal(next(ks), shape, jnp.float32) * scale

    n_pages = PAST_LEN // PAGE_SIZE
    n_pool = (DEC_BATCH * n_pages * 5) // 4
    win_buf = min(WINDOW, PAST_LEN)
    G = NSA_KV_GROUPS
    inputs = {
        'x_prompt': nrm((BATCH, SEQ, D_MODEL), 1.0),
        'x_sample': nrm((DEC_BATCH, DEC_SEQ, D_MODEL), 1.0),
        'cache_nsa_kv': nrm((DEPTH, n_pool, PAGE_SIZE, 2, 2, G, HEAD_DIM), 1.0),
        'cache_nsa_win': nrm((DEPTH, DEC_BATCH, win_buf, 2, G, HEAD_DIM), 1.0),
        'cache_fox_kv': nrm((DEPTH, n_pool, PAGE_SIZE, 2, FOX_HEADS, HEAD_DIM), 1.0),
        'cache_fox_logf': jax.nn.log_sigmoid(FORGET_BIAS_INIT + nrm((DEPTH, n_pool, PAGE_SIZE, FOX_HEADS), 1.0)),
        'page_table': jax.random.permutation(next(ks), n_pool)[:DEC_BATCH * n_pages]
                      .reshape(DEC_BATCH, n_pages).astype(jnp.int32),
        'w_in': nrm((DEPTH, D_MODEL, IN_WIDTH), D_MODEL ** -0.5) * jnp.asarray(value_column_scale()),
        'b_forget': FORGET_BIAS_INIT + nrm((DEPTH, FOX_HEADS), 0.5),
        'nsa_cmp_pos': nrm((DEPTH, 2, CMP_BLOCK, HEAD_DIM), 0.1),
        'nsa_cmp_w1': nrm((DEPTH, 2, CMP_BLOCK, HEAD_DIM, HEAD_DIM), (CMP_BLOCK * HEAD_DIM) ** -0.5),
        'nsa_cmp_w2': nrm((DEPTH, 2, HEAD_DIM, HEAD_DIM), HEAD_DIM ** -0.5),
        'rel_bias_table': nrm((N_BUCKETS, NSA_HEADS), 0.1),
        'g_nsa': 1.0 + nrm((DEPTH, NSA_Q), 0.05),
        'g_fox': 1.0 + nrm((DEPTH, FOX_W), 0.05),
        'w_out': nrm((DEPTH, MIX_WIDTH, D_MODEL), MIX_WIDTH ** -0.5 * BETA),
        'ln1_g': 1.0 + nrm((DEPTH, D_MODEL), 0.05),
        'ln1_b': nrm((DEPTH, D_MODEL), 0.02),
        'peer_w_q': nrm((DEPTH, D_MODEL, PEER_HEADS * PEER_QDIM), D_MODEL ** -0.5),
        'peer_sub_keys': nrm((DEPTH, PEER_HEADS, 2, N_KEYS, PEER_QDIM // 2), (PEER_QDIM // 2) ** -0.5),
        'peer_u': nrm((DEPTH, N_EXPERTS, D_MODEL), D_MODEL ** -0.5 * BETA),
        'peer_v': nrm((DEPTH, N_EXPERTS, D_MODEL), BETA),
        'ln2_g': 1.0 + nrm((DEPTH, D_MODEL), 0.05),
        'ln2_b': nrm((DEPTH, D_MODEL), 0.02),
    }
    return inputs


def reference(x_prompt, x_sample, cache_nsa_kv, cache_nsa_win, cache_fox_kv, cache_fox_logf, page_table,
              w_in, b_forget, nsa_cmp_pos, nsa_cmp_w1, nsa_cmp_w2, rel_bias_table, g_nsa, g_fox, w_out,
              ln1_g, ln1_b, peer_w_q, peer_sub_keys, peer_u, peer_v, ln2_g, ln2_b):
    yp, ys = x_prompt, x_sample
    p_nsa, p_win, p_fox, p_logf = [], [], [], []
    s_nsa, s_win, s_fox, s_logf = [], [], [], []
    for layer in range(DEPTH):
        w = (w_in[layer], b_forget[layer], nsa_cmp_pos[layer], nsa_cmp_w1[layer], nsa_cmp_w2[layer],
             rel_bias_table, g_nsa[layer], g_fox[layer], w_out[layer], ln1_g[layer], ln1_b[layer],
             peer_w_q[layer], peer_sub_keys[layer], peer_u[layer], peer_v[layer], ln2_g[layer], ln2_b[layer])
        yp, a_nsa, a_win, a_fox, a_logf = layer_forward(yp, None, *w)
        past = (gather_pages(cache_nsa_kv[layer], page_table), cache_nsa_win[layer],
                gather_pages(cache_fox_kv[layer], page_table), gather_pages(cache_fox_logf[layer], page_table))
        ys, b_nsa, b_win, b_fox, b_logf = layer_forward(ys, past, *w)
        p_nsa.append(a_nsa); p_win.append(a_win); p_fox.append(a_fox); p_logf.append(a_logf)
        s_nsa.append(b_nsa); s_win.append(b_win); s_fox.append(b_fox); s_logf.append(b_logf)
    return (yp, ys, jnp.stack(p_nsa), jnp.stack(p_win), jnp.stack(p_fox), jnp.stack(p_logf),
            jnp.stack(s_nsa), jnp.stack(s_win), jnp.stack(s_fox), jnp.stack(s_logf))
```

```python
import functools
import math

import jax
import jax.numpy as jnp
import numpy as np
from jax import lax
from jax.experimental import pallas as pl
from jax.experimental.pallas import tpu as pltpu

D_MODEL = 2048
HEAD_DIM = 128
NSA_HEADS = 8
NSA_KV_GROUPS = 2
NSA_HPG = NSA_HEADS // NSA_KV_GROUPS
CMP_BLOCK = 32
CMP_STRIDE = 16
SLC_BLOCK = 64
N_SELECT = 16
WINDOW = 512
FOX_HEADS = 8
NSA_Q = NSA_HEADS * HEAD_DIM
NSA_KV = NSA_KV_GROUPS * HEAD_DIM
FOX_W = FOX_HEADS * HEAD_DIM
IN_SIZES = (NSA_Q, 6 * NSA_KV, 3 * NSA_HEADS, FOX_W, 2 * FOX_W, FOX_HEADS)
N_BUCKETS = 32
MAX_DISTANCE = 128
PEER_HEADS = 8
N_KEYS = 128
PEER_TOPK = 16
PEER_QDIM = 256
Q_BLOCK = 128
SLC_Q_BLOCK = 32
TOKEN_CHUNK = 128
DEPTH = 1
ALPHA = (2.0 * DEPTH) ** 0.25
LN_EPS = 1e-5
NEG = -1e30
FORCE_BONUS = 1e4

LANE = 128


def _mm_kernel(x_ref, w_ref, o_ref):
    o_ref[...] = jnp.dot(x_ref[...].astype(jnp.bfloat16), w_ref[...],
                         preferred_element_type=jnp.float32)


def _matmul(x, w, tm=512, tn=512):
    m, k = x.shape
    n = w.shape[1]
    tm = min(tm, m)
    n_pad = -(-n // tn) * tn
    wb = w.astype(jnp.bfloat16)
    if n_pad != n:
        wb = jnp.pad(wb, ((0, 0), (0, n_pad - n)))
    out = pl.pallas_call(
        _mm_kernel,
        grid=(m // tm, n_pad // tn),
        in_specs=[pl.BlockSpec((tm, k), lambda i, j: (i, 0)),
                  pl.BlockSpec((k, tn), lambda i, j: (0, j))],
        out_specs=pl.BlockSpec((tm, tn), lambda i, j: (i, j)),
        out_shape=jax.ShapeDtypeStruct((m, n_pad), jnp.float32),
        compiler_params=pltpu.CompilerParams(
            dimension_semantics=("parallel", "arbitrary"),
            vmem_limit_bytes=48 * 1024 * 1024),
        name="dense_matmul",
    )(x, wb)
    return out[:, :n] if n_pad != n else out


def layer_norm(x, g, b):
    xf = x.astype(jnp.float32)
    mu = jnp.mean(xf, -1, keepdims=True)
    var = jnp.mean(jnp.square(xf - mu), -1, keepdims=True)
    return ((xf - mu) * lax.rsqrt(var + LN_EPS) * g + b).astype(x.dtype)


def rms_norm(x, g):
    xf = x.astype(jnp.float32)
    return (xf * lax.rsqrt(jnp.mean(xf * xf, -1, keepdims=True) + LN_EPS) * g).astype(x.dtype)


def masked_softmax(s, mask):
    s = jnp.where(mask, s, NEG)
    m = jnp.max(s, -1, keepdims=True)
    p = jnp.where(mask, jnp.exp(s - m), 0.0)
    return p / jnp.maximum(jnp.sum(p, -1, keepdims=True), 1e-30)


def t5_bucket(dist):
    max_exact = N_BUCKETS // 2
    d = jnp.maximum(dist, 0)
    large = max_exact + (jnp.log(jnp.maximum(d, 1).astype(jnp.float32) / max_exact)
                         / math.log(MAX_DISTANCE / max_exact) * (N_BUCKETS - max_exact)).astype(jnp.int32)
    return jnp.where(d < max_exact, d, jnp.minimum(large, N_BUCKETS - 1))


def head_bias(table, dist):
    b = jnp.moveaxis(table[t5_bucket(dist)].astype(jnp.float32), -1, 0)
    return b.reshape((NSA_KV_GROUPS, NSA_HPG) + dist.shape)


def query_block(t, cap):
    return t if t <= cap else cap


def to_blocks(x, axis, qb):
    n = x.shape[axis] // qb
    return jnp.moveaxis(x.reshape(x.shape[:axis] + (n, qb) + x.shape[axis + 1:]), axis, 0)


def from_blocks(y):
    y = jnp.moveaxis(y, 0, 1)
    return y.reshape((y.shape[0], y.shape[1] * y.shape[2]) + y.shape[3:])


def gather_pages(pool, page_table):
    g = pool[page_table]
    return g.reshape((g.shape[0], g.shape[1] * g.shape[2]) + g.shape[3:])


def cmp_to_slc(n_cmp, n_slc):
    c0 = np.arange(n_cmp) * CMP_STRIDE
    s0 = np.arange(n_slc) * SLC_BLOCK
    ov = np.minimum(c0[:, None] + CMP_BLOCK, s0[None, :] + SLC_BLOCK) - np.maximum(c0[:, None], s0[None, :])
    return jnp.asarray(np.maximum(ov, 0) / CMP_STRIDE, dtype=jnp.float32)


def project(x, w_in, b_forget):
    B, T, D = x.shape
    offs = [int(o) for o in np.cumsum(IN_SIZES)[:-1]]
    y = _matmul(x.reshape(B * T, D), w_in).reshape(B, T, -1)
    q_n, kv_n, gate_n, q_f, kv_f, f_f = jnp.split(y, offs, axis=-1)
    qn = q_n.reshape(B, T, NSA_HEADS, HEAD_DIM)
    kvn = kv_n.reshape(B, T, 3, 2, NSA_KV_GROUPS, HEAD_DIM)
    gates = jax.nn.sigmoid(gate_n).reshape(B, T, NSA_HEADS, 3)
    qf = q_f.reshape(B, T, FOX_HEADS, HEAD_DIM)
    kvf = kv_f.reshape(B, T, 2, FOX_HEADS, HEAD_DIM)
    logf = jax.nn.log_sigmoid((f_f + b_forget).astype(jnp.float32))
    return qn, kvn, gates, qf, kvf, logf


def compress(k, pos, w1, w2):
    B, T, G, dk = k.shape
    n_cmp = (T - CMP_BLOCK) // CMP_STRIDE + 1
    halves = k[:, :(n_cmp + 1) * CMP_STRIDE].reshape(B, n_cmp + 1, CMP_STRIDE, G, dk)
    pe = pos.reshape(2, CMP_STRIDE, 1, dk)
    w1r = w1.reshape(2, CMP_STRIDE, dk, w1.shape[-1])
    h = (jnp.einsum('bnsgd,sdh->bngh', halves[:, :-1] + pe[0], w1r[0])
         + jnp.einsum('bnsgd,sdh->bngh', halves[:, 1:] + pe[1], w1r[1]))
    return jax.nn.gelu(h) @ w2


def nsa(q, nsa_full, win_ext, gates, bias_table, cmp_pos, cmp_w1, cmp_w2):
    B, Tq = q.shape[:2]
    Tk = nsa_full.shape[1]
    p0 = Tk - Tq
    dt = q.dtype
    scale = HEAD_DIM ** -0.5
    G, HPG = NSA_KV_GROUPS, NSA_HPG
    qg = q.reshape(B, Tq, G, HPG, HEAD_DIM)
    q_pos = p0 + jnp.arange(Tq)

    kc = compress(nsa_full[:, :, 0, 0], cmp_pos[0], cmp_w1[0], cmp_w2[0])
    vc = compress(nsa_full[:, :, 0, 1], cmp_pos[1], cmp_w1[1], cmp_w2[1])
    n_cmp = kc.shape[1]
    dist_c = q_pos[:, None] - (jnp.arange(n_cmp) * CMP_STRIDE + CMP_BLOCK - 1)[None, :]
    s_c = jnp.einsum('bqghd,bngd->bghqn', qg, kc).astype(jnp.float32) * scale + head_bias(bias_table, dist_c)
    p_c = masked_softmax(s_c, dist_c >= 0)
    o_c = jnp.einsum('bghqn,bngd->bqghd', p_c.astype(dt), vc)

    n_slc = -(-Tk // SLC_BLOCK)
    imp = jnp.einsum('bghqn,nj->bgqj', p_c, cmp_to_slc(n_cmp, n_slc))
    blk = jnp.arange(n_slc)[None, :]
    cur = (q_pos // SLC_BLOCK)[:, None]
    forced = (blk == 0) | (blk == cur) | (blk == cur - 1)
    imp = jnp.where(blk * SLC_BLOCK <= q_pos[:, None], imp + FORCE_BONUS * forced, NEG)
    _, sel = lax.top_k(imp, min(N_SELECT, n_slc))
    tok = (sel[..., None] * SLC_BLOCK + jnp.arange(SLC_BLOCK)).reshape(B, G, Tq, -1)
    kv_s = jnp.pad(nsa_full[:, :, 1], ((0, 0), (0, n_slc * SLC_BLOCK - Tk), (0, 0), (0, 0), (0, 0)))
    kv_s = kv_s.transpose(0, 3, 1, 2, 4)
    b_ix = jnp.arange(B)[:, None, None, None]
    g_ix = jnp.arange(G)[None, :, None, None]
    table_g = bias_table.reshape(N_BUCKETS, G, HPG)

    def slc_block(args):
        qc, tc, pc = args
        kv = kv_s[b_ix, g_ix, tc]
        dist = pc[None, None, :, None] - tc
        bias = jnp.moveaxis(table_g[t5_bucket(dist), g_ix].astype(jnp.float32), -1, 2)
        s = jnp.einsum('bqghd,bgqnd->bghqn', qc, kv[..., 0, :]).astype(jnp.float32) * scale + bias
        p = masked_softmax(s, (dist >= 0)[:, :, None])
        return jnp.einsum('bghqn,bgqnd->bqghd', p.astype(dt), kv[..., 1, :])

    qs = query_block(Tq, SLC_Q_BLOCK)
    o_s = from_blocks(lax.map(slc_block, (to_blocks(qg, 1, qs), to_blocks(tok, 2, qs), q_pos.reshape(-1, qs))))

    qw = query_block(Tq, Q_BLOCK)

    def win_block(args):
        c, qc = args
        kv = lax.dynamic_slice_in_dim(win_ext, c * qw, WINDOW + qw, axis=1)
        pos = p0 + c * qw + jnp.arange(qw)
        kpos = p0 - WINDOW + c * qw + jnp.arange(WINDOW + qw)
        dist = pos[:, None] - kpos[None, :]
        mask = (dist >= 0) & (dist <= WINDOW) & (kpos >= 0)[None, :]
        s = jnp.einsum('bqghd,bkgd->bghqk', qc, kv[:, :, 0]).astype(jnp.float32) * scale + head_bias(bias_table, dist)
        p = masked_softmax(s, mask)
        return jnp.einsum('bghqk,bkgd->bqghd', p.astype(dt), kv[:, :, 1])

    o_w = from_blocks(lax.map(win_block, (jnp.arange(Tq // qw), to_blocks(qg, 1, qw))))

    g = gates.reshape(B, Tq, G, HPG, 3)
    o = g[..., 0:1] * o_c + g[..., 1:2] * o_s + g[..., 2:3] * o_w
    return o.reshape(B, Tq, NSA_Q)


def fox(q, kv, logf):
    B, Tq = q.shape[:2]
    Tk = kv.shape[1]
    p0 = Tk - Tq
    dt = q.dtype
    scale = HEAD_DIM ** -0.5
    cum = jnp.cumsum(logf.astype(jnp.float32), axis=1)
    cum_k = jnp.moveaxis(cum, 1, 2)
    k_all, v_all = kv[:, :, 0], kv[:, :, 1]
    kpos = jnp.arange(Tk)

    def blk(args):
        qc, cq, pos = args
        s = jnp.einsum('bqhd,bkhd->bhqk', qc, k_all).astype(jnp.float32) * scale
        s = s + jnp.moveaxis(cq, 1, 2)[..., None] - cum_k[:, :, None, :]
        p = masked_softmax(s, kpos[None, :] <= pos[:, None])
        return jnp.einsum('bhqk,bkhd->bqhd', p.astype(dt), v_all)

    qb = query_block(Tq, Q_BLOCK)
    o = from_blocks(lax.map(blk, (to_blocks(q, 1, qb), to_blocks(cum[:, p0:], 1, qb),
                                  (p0 + jnp.arange(Tq)).reshape(-1, qb))))
    return o.reshape(B, Tq, FOX_W)


def peer(x, w_q, sub_keys, u_table, v_table):
    n, d = x.shape
    c = min(TOKEN_CHUNK, n)
    n_pad = -(-n // c) * c
    xp = jnp.pad(x, ((0, n_pad - n), (0, 0)))
    q_all = _matmul(xp, w_q)

    def chunk(args):
        xc, qc = args
        qh = qc.reshape(c, PEER_HEADS, 2, PEER_QDIM // 2)
        s = jnp.einsum('chpd,hpkd->chpk', qh, sub_keys).astype(jnp.float32)
        sv, si = lax.top_k(s, PEER_TOPK)
        cand = (sv[:, :, 0, :, None] + sv[:, :, 1, None, :]).reshape(c, PEER_HEADS, PEER_TOPK * PEER_TOPK)
        best, bi = lax.top_k(cand, PEER_TOPK)
        i1 = jnp.take_along_axis(si[:, :, 0], bi // PEER_TOPK, axis=-1)
        i2 = jnp.take_along_axis(si[:, :, 1], bi % PEER_TOPK, axis=-1)
        experts = i1 * N_KEYS + i2
        gate = jax.nn.softmax(best, axis=-1)
        act = jax.nn.gelu(jnp.einsum('cd,chkd->chk', xc, u_table[experts]).astype(jnp.float32))
        return jnp.einsum('chk,chkd->cd', (gate * act).astype(x.dtype), v_table[experts])

    y = lax.map(chunk, (xp.reshape(n_pad // c, c, d), q_all.reshape(n_pad // c, c, -1)))
    return y.reshape(n_pad, d)[:n]


def layer_forward(x, past, w_in, b_forget, cmp_pos, cmp_w1, cmp_w2, bias_table, g_nsa, g_fox, w_out,
                  ln1_g, ln1_b, peer_w_q, peer_keys, peer_u, peer_v, ln2_g, ln2_b):
    B, T, D = x.shape
    qn, kvn, gates, qf, kvf, logf = project(x, w_in, b_forget)
    nsa_rows, win_rows = kvn[:, :, :2], kvn[:, :, 2]
    if past is None:
        nsa_full, fox_full, logf_full = nsa_rows, kvf, logf
        win_ext = jnp.pad(win_rows, ((0, 0), (WINDOW, 0), (0, 0), (0, 0), (0, 0)))
        buf_len = min(WINDOW, T)
    else:
        nsa_past, win_past, fox_past, logf_past = past
        buf_len = win_past.shape[1]
        nsa_full = jnp.concatenate([nsa_past, nsa_rows], axis=1)
        fox_full = jnp.concatenate([fox_past, kvf], axis=1)
        logf_full = jnp.concatenate([logf_past.astype(jnp.float32), logf], axis=1)
        pad = jnp.zeros((B, WINDOW - buf_len) + win_rows.shape[2:], win_rows.dtype)
        win_ext = jnp.concatenate([pad, win_past, win_rows], axis=1)
    o_n = nsa(qn, nsa_full, win_ext, gates, bias_table, cmp_pos, cmp_w1, cmp_w2)
    o_f = fox(qf, fox_full, logf_full)
    mixed = jnp.concatenate([rms_norm(o_n, g_nsa), rms_norm(o_f, g_fox)], axis=-1)
    mix = _matmul(mixed.reshape(B * T, -1), w_out).reshape(B, T, D)
    h = layer_norm(ALPHA * x + mix, ln1_g, ln1_b)
    f = peer(h.reshape(B * T, D), peer_w_q, peer_keys, peer_u, peer_v).reshape(B, T, D)
    y = layer_norm(ALPHA * h + f, ln2_g, ln2_b)
    return y, nsa_rows, win_ext[:, win_ext.shape[1] - buf_len:], kvf, logf


def kernel(x_prompt, x_sample, cache_nsa_kv, cache_nsa_win, cache_fox_kv, cache_fox_logf, page_table,
           w_in, b_forget, nsa_cmp_pos, nsa_cmp_w1, nsa_cmp_w2, rel_bias_table, g_nsa, g_fox, w_out,
           ln1_g, ln1_b, peer_w_q, peer_sub_keys, peer_u, peer_v, ln2_g, ln2_b):
    layer = 0
    w = (w_in[layer], b_forget[layer], nsa_cmp_pos[layer], nsa_cmp_w1[layer], nsa_cmp_w2[layer],
         rel_bias_table, g_nsa[layer], g_fox[layer], w_out[layer], ln1_g[layer], ln1_b[layer],
         peer_w_q[layer], peer_sub_keys[layer], peer_u[layer], peer_v[layer], ln2_g[layer], ln2_b[layer])
    yp, a_nsa, a_win, a_fox, a_logf = layer_forward(x_prompt, None, *w)
    past = (gather_pages(cache_nsa_kv[layer], page_table), cache_nsa_win[layer],
            gather_pages(cache_fox_kv[layer], page_table), gather_pages(cache_fox_logf[layer], page_table))
    ys, b_nsa, b_win, b_fox, b_logf = layer_forward(x_sample, past, *w)
    return (yp, ys, a_nsa[None], a_win[None], a_fox[None], a_logf[None],
            b_nsa[None], b_win[None], b_fox[None], b_logf[None])
```

```python
import functools
import math

import jax
import jax.numpy as jnp
import numpy as np
from jax import lax
from jax.experimental import pallas as pl
from jax.experimental.pallas import tpu as pltpu

D_MODEL = 2048
HEAD_DIM = 128
NSA_HEADS = 8
NSA_KV_GROUPS = 2
NSA_HPG = NSA_HEADS // NSA_KV_GROUPS
CMP_BLOCK = 32
CMP_STRIDE = 16
SLC_BLOCK = 64
N_SELECT = 16
WINDOW = 512
FOX_HEADS = 8
NSA_Q = NSA_HEADS * HEAD_DIM
NSA_KV = NSA_KV_GROUPS * HEAD_DIM
FOX_W = FOX_HEADS * HEAD_DIM
IN_SIZES = (NSA_Q, 6 * NSA_KV, 3 * NSA_HEADS, FOX_W, 2 * FOX_W, FOX_HEADS)
N_BUCKETS = 32
MAX_DISTANCE = 128
PEER_HEADS = 8
N_KEYS = 128
PEER_TOPK = 16
PEER_QDIM = 256
Q_BLOCK = 128
SLC_Q_BLOCK = 32
TOKEN_CHUNK = 128
DEPTH = 1
ALPHA = (2.0 * DEPTH) ** 0.25
LN_EPS = 1e-5
NEG = -1e30
FORCE_BONUS = 1e4
SCALE = HEAD_DIM ** -0.5

LANE = 128
VMEM_LIMIT = 48 * 1024 * 1024

BIG_WIDTH = NSA_Q + 6 * NSA_KV + FOX_W + 2 * FOX_W
CB_QN = 0
CB_KVN = NSA_Q // LANE
CB_QF = CB_KVN + 6 * NSA_KV // LANE
CB_KF = CB_QF + FOX_W // LANE
CB_VF = CB_KF + FOX_W // LANE
SMALL_WIDTH = 3 * LANE


def _dot_nt(a, b):
    return lax.dot_general(a, b, (((1,), (1,)), ((), ())), preferred_element_type=jnp.float32)


def _dot(a, b):
    return jnp.dot(a, b, preferred_element_type=jnp.float32)


def _mm_kernel(x_ref, w_ref, o_ref):
    o_ref[...] = _dot(x_ref[...].astype(jnp.bfloat16), w_ref[...])


def _matmul(x, w, tm=512, tn=512):
    m, k = x.shape
    n = w.shape[1]
    tm = min(tm, m)
    n_pad = -(-n // tn) * tn
    wb = w.astype(jnp.bfloat16)
    if n_pad != n:
        wb = jnp.pad(wb, ((0, 0), (0, n_pad - n)))
    out = pl.pallas_call(
        _mm_kernel,
        grid=(m // tm, n_pad // tn),
        in_specs=[pl.BlockSpec((tm, k), lambda i, j: (i, 0)),
                  pl.BlockSpec((k, tn), lambda i, j: (0, j))],
        out_specs=pl.BlockSpec((tm, tn), lambda i, j: (i, j)),
        out_shape=jax.ShapeDtypeStruct((m, n_pad), jnp.float32),
        compiler_params=pltpu.CompilerParams(
            dimension_semantics=("parallel", "arbitrary"),
            vmem_limit_bytes=VMEM_LIMIT),
        name="dense_matmul",
    )(x, wb)
    return out[:, :n] if n_pad != n else out


def _proj_small_kernel(x_ref, w_ref, b_ref, o_ref):
    y = _dot(x_ref[...].astype(jnp.bfloat16), w_ref[...]) + b_ref[...]
    gates = y[:, :2 * LANE]
    o_ref[:, :2 * LANE] = 1.0 / (1.0 + jnp.exp(-gates))
    f = y[:, 2 * LANE:]
    o_ref[:, 2 * LANE:] = -(jnp.maximum(-f, 0.0) + jnp.log1p(jnp.exp(-jnp.abs(f))))


def _proj_small(x, w_small, b_small, tm=512):
    m, k = x.shape
    tm = min(tm, m)
    return pl.pallas_call(
        _proj_small_kernel,
        grid=(m // tm,),
        in_specs=[pl.BlockSpec((tm, k), lambda i: (i, 0)),
                  pl.BlockSpec((k, SMALL_WIDTH), lambda i: (0, 0)),
                  pl.BlockSpec((1, SMALL_WIDTH), lambda i: (0, 0))],
        out_specs=pl.BlockSpec((tm, SMALL_WIDTH), lambda i: (i, 0)),
        out_shape=jax.ShapeDtypeStruct((m, SMALL_WIDTH), jnp.float32),
        compiler_params=pltpu.CompilerParams(
            dimension_semantics=("parallel",), vmem_limit_bytes=VMEM_LIMIT),
        name="proj_small",
    )(x, w_small, b_small)


def _permute_w_in(w_in, b_forget):
    offs = [0] + [int(o) for o in np.cumsum(IN_SIZES)]
    q_n, kv_n, gate, q_f, kv_f, f_f = (w_in[:, offs[i]:offs[i + 1]] for i in range(6))
    w_big = jnp.concatenate([q_n, kv_n, q_f, kv_f], axis=1).astype(jnp.bfloat16)
    d = w_in.shape[0]
    n_gate = 3 * NSA_HPG
    zg = jnp.zeros((d, LANE - n_gate), w_in.dtype)
    zf = jnp.zeros((d, LANE - FOX_HEADS), w_in.dtype)
    w_small = jnp.concatenate([gate[:, :n_gate], zg, gate[:, n_gate:], zg, f_f, zf], axis=1).astype(jnp.bfloat16)
    b_small = jnp.concatenate([jnp.zeros((2 * LANE,), jnp.float32), b_forget.astype(jnp.float32),
                               jnp.zeros((LANE - FOX_HEADS,), jnp.float32)])[None]
    return w_big, w_small, b_small


def _gelu_tanh(h):
    return 0.5 * h * (1.0 + jnp.tanh(math.sqrt(2.0 / math.pi) * (h + 0.044715 * (h * h * h))))


def _compress_kernel(k_ref, pe_ref, w1_ref, w2_ref, o_ref, *, nh):
    def half(s0):
        acc = jnp.zeros((nh, HEAD_DIM), jnp.float32)
        for s in range(CMP_STRIDE):
            rows = k_ref[0, pl.ds(s, nh, stride=CMP_STRIDE), :] + pe_ref[0, s0 + s:s0 + s + 1, :]
            acc = acc + _dot(rows.astype(jnp.bfloat16), w1_ref[0, s0 + s])
        return acc

    first = half(0)
    second = half(CMP_STRIDE)
    h = first + pltpu.roll(second, nh - 1, 0)
    o_ref[0, 0] = _dot(_gelu_tanh(h).astype(jnp.bfloat16), w2_ref[0])


def _compress_prompt(yb3, cmp_pos, cmp_w1, cmp_w2):
    b, t, _ = yb3.shape
    nh = t // CMP_STRIDE
    n_kg = 2 * NSA_KV_GROUPS
    return pl.pallas_call(
        functools.partial(_compress_kernel, nh=nh),
        grid=(b, n_kg),
        in_specs=[pl.BlockSpec((1, t, LANE), lambda i, c: (i, 0, CB_KVN + c)),
                  pl.BlockSpec((1, CMP_BLOCK, HEAD_DIM), lambda i, c: (c // NSA_KV_GROUPS, 0, 0)),
                  pl.BlockSpec((1, CMP_BLOCK, HEAD_DIM, HEAD_DIM), lambda i, c: (c // NSA_KV_GROUPS, 0, 0, 0)),
                  pl.BlockSpec((1, HEAD_DIM, HEAD_DIM), lambda i, c: (c // NSA_KV_GROUPS, 0, 0))],
        out_specs=pl.BlockSpec((1, 1, nh, HEAD_DIM), lambda i, c: (i, c, 0, 0)),
        out_shape=jax.ShapeDtypeStruct((b, n_kg, nh, HEAD_DIM), jnp.float32),
        compiler_params=pltpu.CompilerParams(
            dimension_semantics=("parallel", "arbitrary"), vmem_limit_bytes=VMEM_LIMIT),
        name="nsa_compress",
    )(yb3, cmp_pos, cmp_w1.astype(jnp.bfloat16), cmp_w2.astype(jnp.bfloat16))


def _nsa_kernel(q_ref, kc_ref, vc_ref, ks_ref, vs_ref, kw_ref, vw_ref, gate_ref, biasc_ref, biast_ref,
                c2s_ref, expand_ref, o_ref, q_s, sel_s, m_s, l_s, acc_s, out_s, *, n_slc, n_top):
    i = pl.program_id(2)
    rows_q = NSA_HPG * LANE
    bf16 = jnp.bfloat16

    qt = q_ref[0]
    q_s[...] = jnp.concatenate([qt[:, h * LANE:(h + 1) * LANE] for h in range(NSA_HPG)], axis=0).astype(bf16)

    r = lax.broadcasted_iota(jnp.int32, (rows_q, LANE), 0) & (LANE - 1)
    c = lax.broadcasted_iota(jnp.int32, (rows_q, LANE), 1)
    q_pos = i * LANE + r
    s = _dot_nt(q_s[...], kc_ref[0, 0].astype(bf16)) * SCALE + biasc_ref[...].reshape(rows_q, LANE)
    mask = c * CMP_STRIDE + (CMP_BLOCK - 1) <= q_pos
    s = jnp.where(mask, s, NEG)
    p = jnp.where(mask, jnp.exp(s - jnp.max(s, -1, keepdims=True)), 0.0)
    p = p / jnp.maximum(jnp.sum(p, -1, keepdims=True), 1e-30)
    out_s[0] = _dot(p.astype(bf16), vc_ref[0, 0].astype(bf16))

    p_sum = p[0:LANE]
    for h in range(1, NSA_HPG):
        p_sum = p_sum + p[h * LANE:(h + 1) * LANE]
    p_hi = p_sum.astype(bf16)
    p_lo = (p_sum - p_hi.astype(jnp.float32)).astype(bf16)
    imp = _dot(p_hi, c2s_ref[...]) + _dot(p_lo, c2s_ref[...])
    jj = lax.broadcasted_iota(jnp.int32, (LANE, LANE), 1)
    qp = i * LANE + lax.broadcasted_iota(jnp.int32, (LANE, LANE), 0)
    cur = qp >> int(math.log2(SLC_BLOCK))
    forced = (jj == 0) | (jj == cur) | (jj == cur - 1)
    val = jnp.where(jj * SLC_BLOCK <= qp, imp + jnp.where(forced, FORCE_BONUS, 0.0), NEG)
    val = jnp.where(jj < n_slc, val, -3e38)
    rank = jnp.zeros((LANE, LANE), jnp.int32)
    for t in range(n_slc):
        col = val[:, t:t + 1]
        ahead = (col > val) | ((col == val) & (jj > t))
        rank = rank + ahead.astype(jnp.int32)
    sel_s[...] = jnp.where(rank < n_top, 1.0, 0.0).astype(bf16)

    def attend(k_ref, v_ref, lo, use_sel, use_win, slot):
        m_s[...] = jnp.full((rows_q, 1), NEG, jnp.float32)
        l_s[...] = jnp.zeros((rows_q, 1), jnp.float32)
        acc_s[...] = jnp.zeros((rows_q, HEAD_DIM), jnp.float32)

        def body(kt, carry):
            off = pl.multiple_of(kt * LANE, LANE)
            k = k_ref[0, pl.ds(off, LANE), :].astype(bf16)
            v = v_ref[0, pl.ds(off, LANE), :].astype(bf16)
            dq = i - kt
            sc = _dot_nt(q_s[...], k) * SCALE + biast_ref[0, jnp.minimum(dq, 2)]
            rr = lax.broadcasted_iota(jnp.int32, (rows_q, LANE), 0) & (LANE - 1)
            cc = lax.broadcasted_iota(jnp.int32, (rows_q, LANE), 1)
            dist = rr - cc + dq * LANE
            msk = dist >= 0
            if use_win:
                msk = msk & (dist <= WINDOW)
            if use_sel:
                se = _dot(sel_s[...], expand_ref[kt])
                msk = msk & (jnp.concatenate([se] * NSA_HPG, axis=0) > 0.5)
            sc = jnp.where(msk, sc, NEG)
            m_old = m_s[...]
            m_new = jnp.maximum(m_old, jnp.max(sc, -1, keepdims=True))
            pe = jnp.where(msk, jnp.exp(sc - m_new), 0.0)
            alpha = jnp.exp(m_old - m_new)
            l_s[...] = alpha * l_s[...] + jnp.sum(pe, -1, keepdims=True)
            acc_s[...] = alpha * acc_s[...] + _dot(pe.astype(bf16), v)
            m_s[...] = m_new
            return carry

        lax.fori_loop(lo, i + 1, body, 0)
        out_s[slot] = acc_s[...] / jnp.maximum(l_s[...], 1e-30)

    attend(ks_ref, vs_ref, 0, True, False, 1)
    attend(kw_ref, vw_ref, jnp.maximum(i - WINDOW // LANE, 0), False, True, 2)

    gt = gate_ref[0]
    for h in range(NSA_HPG):
        sl = slice(h * LANE, (h + 1) * LANE)
        o_ref[0, :, sl] = (gt[:, 3 * h:3 * h + 1] * out_s[0, sl, :]
                           + gt[:, 3 * h + 1:3 * h + 2] * out_s[1, sl, :]
                           + gt[:, 3 * h + 2:3 * h + 3] * out_s[2, sl, :])


def _t5_bucket_np(d):
    max_exact = N_BUCKETS // 2
    d = np.maximum(d, 0)
    large = max_exact + (np.log(np.maximum(d, 1).astype(np.float32) / np.float32(max_exact))
                         / np.float32(math.log(MAX_DISTANCE / max_exact)) * (N_BUCKETS - max_exact)).astype(np.int32)
    return np.where(d < max_exact, d, np.minimum(large, N_BUCKETS - 1)).astype(np.int32)


def _nsa_prompt(yb3, ys3, kcvc, bias_table):
    b, t, _ = yb3.shape
    n_t = t // LANE
    n_slc = t // SLC_BLOCK
    n_cmp = (t - CMP_BLOCK) // CMP_STRIDE + 1
    n_top = min(N_SELECT, n_slc)
    rows_q = NSA_HPG * LANE
    assert t % LANE == 0 and n_cmp <= LANE and n_slc <= LANE

    table = bias_table.astype(jnp.float32)
    dist_c = np.arange(t)[:, None] - (np.arange(LANE) * CMP_STRIDE + CMP_BLOCK - 1)[None, :]
    bias_c = jnp.moveaxis(table[_t5_bucket_np(dist_c)], -1, 0)
    rc = np.arange(LANE)[:, None] - np.arange(LANE)[None, :]
    buckets_t = np.stack([_t5_bucket_np(rc), _t5_bucket_np(rc + LANE), _t5_bucket_np(rc + 2 * LANE)])
    assert (_t5_bucket_np(np.arange(LANE + 1, 4 * LANE)) == N_BUCKETS - 1).all()
    bias_t = jnp.moveaxis(table[buckets_t], -1, 0)
    bias_t = bias_t.reshape(NSA_KV_GROUPS, NSA_HPG, 3, LANE, LANE).transpose(0, 2, 1, 3, 4)
    bias_t = bias_t.reshape(NSA_KV_GROUPS, 3, rows_q, LANE)

    c0 = np.arange(n_cmp) * CMP_STRIDE
    s0 = np.arange(n_slc) * SLC_BLOCK
    ov = np.minimum(c0[:, None] + CMP_BLOCK, s0[None, :] + SLC_BLOCK) - np.maximum(c0[:, None], s0[None, :])
    c2s = np.zeros((LANE, LANE), np.float32)
    c2s[:n_cmp, :n_slc] = np.maximum(ov, 0) / CMP_STRIDE
    expand = np.zeros((n_t, LANE, LANE), np.float32)
    for kt in range(n_t):
        tok_blk = (kt * LANE + np.arange(LANE)) // SLC_BLOCK
        expand[kt, tok_blk, np.arange(LANE)] = 1.0

    kv_spec = lambda cb: pl.BlockSpec((1, t, LANE), lambda bi, g, i: (bi, 0, cb + g))
    g_n = NSA_KV_GROUPS
    return pl.pallas_call(
        functools.partial(_nsa_kernel, n_slc=n_slc, n_top=n_top),
        grid=(b, g_n, n_t),
        in_specs=[
            pl.BlockSpec((1, LANE, rows_q), lambda bi, g, i: (bi, i, g)),
            pl.BlockSpec((1, 1, t // CMP_STRIDE, HEAD_DIM), lambda bi, g, i: (bi, g, 0, 0)),
            pl.BlockSpec((1, 1, t // CMP_STRIDE, HEAD_DIM), lambda bi, g, i: (bi, g_n + g, 0, 0)),
            kv_spec(CB_KVN + 2 * g_n), kv_spec(CB_KVN + 3 * g_n),
            kv_spec(CB_KVN + 4 * g_n), kv_spec(CB_KVN + 5 * g_n),
            pl.BlockSpec((1, LANE, LANE), lambda bi, g, i: (bi, i, g)),
            pl.BlockSpec((NSA_HPG, LANE, LANE), lambda bi, g, i: (g, i, 0)),
            pl.BlockSpec((1, 3, rows_q, LANE), lambda bi, g, i: (g, 0, 0, 0)),
            pl.BlockSpec((LANE, LANE), lambda bi, g, i: (0, 0)),
            pl.BlockSpec((n_t, LANE, LANE), lambda bi, g, i: (0, 0, 0)),
        ],
        out_specs=pl.BlockSpec((1, LANE, rows_q), lambda bi, g, i: (bi, i, g)),
        out_shape=jax.ShapeDtypeStruct((b, t, NSA_Q), jnp.float32),
        scratch_shapes=[pltpu.VMEM((rows_q, HEAD_DIM), jnp.bfloat16),
                        pltpu.VMEM((LANE, LANE), jnp.bfloat16),
                        pltpu.VMEM((rows_q, 1), jnp.float32),
                        pltpu.VMEM((rows_q, 1), jnp.float32),
                        pltpu.VMEM((rows_q, HEAD_DIM), jnp.float32),
                        pltpu.VMEM((3, rows_q, HEAD_DIM), jnp.float32)],
        compiler_params=pltpu.CompilerParams(
            dimension_semantics=("parallel", "parallel", "arbitrary"), vmem_limit_bytes=VMEM_LIMIT),
        name="nsa_prompt",
    )(yb3, kcvc, kcvc, yb3, yb3, yb3, yb3, ys3, bias_c, bias_t,
      jnp.asarray(c2s, jnp.bfloat16), jnp.asarray(expand, jnp.bfloat16))


FOX_TILE = 256


def _fox_kernel(q_ref, k_ref, v_ref, cq_ref, ck_ref, o_ref, m_s, l_s, acc_s):
    i = pl.program_id(2)
    tq = FOX_TILE
    bf16 = jnp.bfloat16
    q = q_ref[0].astype(bf16)
    cq = cq_ref[0, 0]
    m_s[...] = jnp.full((tq, 1), NEG, jnp.float32)
    l_s[...] = jnp.zeros((tq, 1), jnp.float32)
    acc_s[...] = jnp.zeros((tq, HEAD_DIM), jnp.float32)

    def body(kt, carry):
        off = pl.multiple_of(kt * tq, tq)
        k = k_ref[0, pl.ds(off, tq), :].astype(bf16)
        v = v_ref[0, pl.ds(off, tq), :].astype(bf16)
        sc = _dot_nt(q, k) * SCALE + cq - ck_ref[0, 0, kt]
        rr = lax.broadcasted_iota(jnp.int32, (tq, tq), 0)
        cc = lax.broadcasted_iota(jnp.int32, (tq, tq), 1)
        msk = cc + (kt - i) * tq <= rr
        sc = jnp.where(msk, sc, NEG)
        m_old = m_s[...]
        m_new = jnp.maximum(m_old, jnp.max(sc, -1, keepdims=True))
        pe = jnp.where(msk, jnp.exp(sc - m_new), 0.0)
        alpha = jnp.exp(m_old - m_new)
        l_s[...] = alpha * l_s[...] + jnp.sum(pe, -1, keepdims=True)
        acc_s[...] = alpha * acc_s[...] + _dot(pe.astype(bf16), v)
        m_s[...] = m_new
        return carry

    lax.fori_loop(0, i + 1, body, 0)
    o_ref[0] = acc_s[...] / jnp.maximum(l_s[...], 1e-30)


def _fox_prompt(yb3, logf):
    b, t, _ = yb3.shape
    tq = FOX_TILE
    n_t = t // tq
    cum = jnp.moveaxis(jnp.cumsum(logf.astype(jnp.float32), axis=1), 1, 2)
    cum_q = cum[..., None]
    cum_k = cum.reshape(b, FOX_HEADS, n_t, 1, tq)
    return pl.pallas_call(
        _fox_kernel,
        grid=(b, FOX_HEADS, n_t),
        in_specs=[pl.BlockSpec((1, tq, LANE), lambda bi, h, i: (bi, i, CB_QF + h)),
                  pl.BlockSpec((1, t, LANE), lambda bi, h, i: (bi, 0, CB_KF + h)),
                  pl.BlockSpec((1, t, LANE), lambda bi, h, i: (bi, 0, CB_VF + h)),
                  pl.BlockSpec((1, 1, tq, 1), lambda bi, h, i: (bi, h, i, 0)),
                  pl.BlockSpec((1, 1, n_t, 1, tq), lambda bi, h, i: (bi, h, 0, 0, 0))],
        out_specs=pl.BlockSpec((1, tq, LANE), lambda bi, h, i: (bi, i, h)),
        out_shape=jax.ShapeDtypeStruct((b, t, FOX_W), jnp.float32),
        scratch_shapes=[pltpu.VMEM((tq, 1), jnp.float32),
                        pltpu.VMEM((tq, 1), jnp.float32),
                        pltpu.VMEM((tq, HEAD_DIM), jnp.float32)],
        compiler_params=pltpu.CompilerParams(
            dimension_semantics=("parallel", "parallel", "arbitrary"), vmem_limit_bytes=VMEM_LIMIT),
        name="fox_prompt",
    )(yb3, yb3, yb3, cum_q, cum_k)


def _ln(z, g, b):
    mu = jnp.mean(z, -1, keepdims=True)
    zc = z - mu
    var = jnp.mean(zc * zc, -1, keepdims=True)
    return zc * lax.rsqrt(var + LN_EPS) * g + b


def _post_kernel(on_ref, of_ref, x_ref, gn_ref, gf_ref, w_ref, lg_ref, lb_ref, h_ref):
    def rms(o, g):
        return (o * lax.rsqrt(jnp.mean(o * o, -1, keepdims=True) + LN_EPS) * g).astype(jnp.bfloat16)

    mix = (_dot(rms(on_ref[...], gn_ref[...]), w_ref[:NSA_Q, :])
           + _dot(rms(of_ref[...], gf_ref[...]), w_ref[NSA_Q:, :]))
    h_ref[...] = _ln(ALPHA * x_ref[...] + mix, lg_ref[...], lb_ref[...])


def _post_attention(o_n, o_f, x, g_nsa, g_fox, w_out, ln_g, ln_b, tm=256):
    m, d = x.shape
    tm = min(tm, m)
    row = lambda n: pl.BlockSpec((1, n), lambda i: (0, 0))
    return pl.pallas_call(
        _post_kernel,
        grid=(m // tm,),
        in_specs=[pl.BlockSpec((tm, NSA_Q), lambda i: (i, 0)),
                  pl.BlockSpec((tm, FOX_W), lambda i: (i, 0)),
                  pl.BlockSpec((tm, d), lambda i: (i, 0)),
                  row(NSA_Q), row(FOX_W),
                  pl.BlockSpec((NSA_Q + FOX_W, d), lambda i: (0, 0)),
                  row(d), row(d)],
        out_specs=pl.BlockSpec((tm, d), lambda i: (i, 0)),
        out_shape=jax.ShapeDtypeStruct((m, d), jnp.float32),
        compiler_params=pltpu.CompilerParams(
            dimension_semantics=("parallel",), vmem_limit_bytes=VMEM_LIMIT),
        name="post_attention",
    )(o_n, o_f, x, g_nsa[None], g_fox[None], w_out.astype(jnp.bfloat16), ln_g[None], ln_b[None])


def _add_ln_kernel(h_ref, f_ref, g_ref, b_ref, o_ref):
    o_ref[...] = _ln(ALPHA * h_ref[...] + f_ref[...], g_ref[...], b_ref[...])


def _add_ln(h, f, ln_g, ln_b, tm=512):
    m, d = h.shape
    tm = min(tm, m)
    return pl.pallas_call(
        _add_ln_kernel,
        grid=(m // tm,),
        in_specs=[pl.BlockSpec((tm, d), lambda i: (i, 0)),
                  pl.BlockSpec((tm, d), lambda i: (i, 0)),
                  pl.BlockSpec((1, d), lambda i: (0, 0)),
                  pl.BlockSpec((1, d), lambda i: (0, 0))],
        out_specs=pl.BlockSpec((tm, d), lambda i: (i, 0)),
        out_shape=jax.ShapeDtypeStruct((m, d), jnp.float32),
        compiler_params=pltpu.CompilerParams(
            dimension_semantics=("parallel",), vmem_limit_bytes=VMEM_LIMIT),
        name="add_layer_norm",
    )(h, f, ln_g[None], ln_b[None])


def layer_norm(x, g, b):
    xf = x.astype(jnp.float32)
    mu = jnp.mean(xf, -1, keepdims=True)
    var = jnp.mean(jnp.square(xf - mu), -1, keepdims=True)
    return ((xf - mu) * lax.rsqrt(var + LN_EPS) * g + b).astype(x.dtype)


def rms_norm(x, g):
    xf = x.astype(jnp.float32)
    return (xf * lax.rsqrt(jnp.mean(xf * xf, -1, keepdims=True) + LN_EPS) * g).astype(x.dtype)


def masked_softmax(s, mask):
    s = jnp.where(mask, s, NEG)
    m = jnp.max(s, -1, keepdims=True)
    p = jnp.where(mask, jnp.exp(s - m), 0.0)
    return p / jnp.maximum(jnp.sum(p, -1, keepdims=True), 1e-30)


def t5_bucket(dist):
    max_exact = N_BUCKETS // 2
    d = jnp.maximum(dist, 0)
    large = max_exact + (jnp.log(jnp.maximum(d, 1).astype(jnp.float32) / max_exact)
                         / math.log(MAX_DISTANCE / max_exact) * (N_BUCKETS - max_exact)).astype(jnp.int32)
    return jnp.where(d < max_exact, d, jnp.minimum(large, N_BUCKETS - 1))


def head_bias(table, dist):
    b = jnp.moveaxis(table[t5_bucket(dist)].astype(jnp.float32), -1, 0)
    return b.reshape((NSA_KV_GROUPS, NSA_HPG) + dist.shape)


def query_block(t, cap):
    return t if t <= cap else cap


def to_blocks(x, axis, qb):
    n = x.shape[axis] // qb
    return jnp.moveaxis(x.reshape(x.shape[:axis] + (n, qb) + x.shape[axis + 1:]), axis, 0)


def from_blocks(y):
    y = jnp.moveaxis(y, 0, 1)
    return y.reshape((y.shape[0], y.shape[1] * y.shape[2]) + y.shape[3:])


def gather_pages(pool, page_table):
    g = pool[page_table]
    return g.reshape((g.shape[0], g.shape[1] * g.shape[2]) + g.shape[3:])


def cmp_to_slc(n_cmp, n_slc):
    c0 = np.arange(n_cmp) * CMP_STRIDE
    s0 = np.arange(n_slc) * SLC_BLOCK
    ov = np.minimum(c0[:, None] + CMP_BLOCK, s0[None, :] + SLC_BLOCK) - np.maximum(c0[:, None], s0[None, :])
    return jnp.asarray(np.maximum(ov, 0) / CMP_STRIDE, dtype=jnp.float32)


def project(x, w_in, b_forget):
    B, T, D = x.shape
    offs = [int(o) for o in np.cumsum(IN_SIZES)[:-1]]
    y = _matmul(x.reshape(B * T, D), w_in).reshape(B, T, -1)
    q_n, kv_n, gate_n, q_f, kv_f, f_f = jnp.split(y, offs, axis=-1)
    qn = q_n.reshape(B, T, NSA_HEADS, HEAD_DIM)
    kvn = kv_n.reshape(B, T, 3, 2, NSA_KV_GROUPS, HEAD_DIM)
    gates = jax.nn.sigmoid(gate_n).reshape(B, T, NSA_HEADS, 3)
    qf = q_f.reshape(B, T, FOX_HEADS, HEAD_DIM)
    kvf = kv_f.reshape(B, T, 2, FOX_HEADS, HEAD_DIM)
    logf = jax.nn.log_sigmoid((f_f + b_forget).astype(jnp.float32))
    return qn, kvn, gates, qf, kvf, logf


def compress(k, pos, w1, w2):
    B, T, G, dk = k.shape
    n_cmp = (T - CMP_BLOCK) // CMP_STRIDE + 1
    halves = k[:, :(n_cmp + 1) * CMP_STRIDE].reshape(B, n_cmp + 1, CMP_STRIDE, G, dk)
    pe = pos.reshape(2, CMP_STRIDE, 1, dk)
    w1r = w1.reshape(2, CMP_STRIDE, dk, w1.shape[-1])
    h = (jnp.einsum('bnsgd,sdh->bngh', halves[:, :-1] + pe[0], w1r[0])
         + jnp.einsum('bnsgd,sdh->bngh', halves[:, 1:] + pe[1], w1r[1]))
    return jax.nn.gelu(h) @ w2


def nsa(q, nsa_full, win_ext, gates, bias_table, cmp_pos, cmp_w1, cmp_w2):
    B, Tq = q.shape[:2]
    Tk = nsa_full.shape[1]
    p0 = Tk - Tq
    dt = q.dtype
    scale = HEAD_DIM ** -0.5
    G, HPG = NSA_KV_GROUPS, NSA_HPG
    qg = q.reshape(B, Tq, G, HPG, HEAD_DIM)
    q_pos = p0 + jnp.arange(Tq)

    kc = compress(nsa_full[:, :, 0, 0], cmp_pos[0], cmp_w1[0], cmp_w2[0])
    vc = compress(nsa_full[:, :, 0, 1], cmp_pos[1], cmp_w1[1], cmp_w2[1])
    n_cmp = kc.shape[1]
    dist_c = q_pos[:, None] - (jnp.arange(n_cmp) * CMP_STRIDE + CMP_BLOCK - 1)[None, :]
    s_c = jnp.einsum('bqghd,bngd->bghqn', qg, kc).astype(jnp.float32) * scale + head_bias(bias_table, dist_c)
    p_c = masked_softmax(s_c, dist_c >= 0)
    o_c = jnp.einsum('bghqn,bngd->bqghd', p_c.astype(dt), vc)

    n_slc = -(-Tk // SLC_BLOCK)
    imp = jnp.einsum('bghqn,nj->bgqj', p_c, cmp_to_slc(n_cmp, n_slc))
    blk = jnp.arange(n_slc)[None, :]
    cur = (q_pos // SLC_BLOCK)[:, None]
    forced = (blk == 0) | (blk == cur) | (blk == cur - 1)
    imp = jnp.where(blk * SLC_BLOCK <= q_pos[:, None], imp + FORCE_BONUS * forced, NEG)
    _, sel = lax.top_k(imp, min(N_SELECT, n_slc))
    tok = (sel[..., None] * SLC_BLOCK + jnp.arange(SLC_BLOCK)).reshape(B, G, Tq, -1)
    kv_s = jnp.pad(nsa_full[:, :, 1], ((0, 0), (0, n_slc * SLC_BLOCK - Tk), (0, 0), (0, 0), (0, 0)))
    kv_s = kv_s.transpose(0, 3, 1, 2, 4)
    b_ix = jnp.arange(B)[:, None, None, None]
    g_ix = jnp.arange(G)[None, :, None, None]
    table_g = bias_table.reshape(N_BUCKETS, G, HPG)

    def slc_block(args):
        qc, tc, pc = args
        kv = kv_s[b_ix, g_ix, tc]
        dist = pc[None, None, :, None] - tc
        bias = jnp.moveaxis(table_g[t5_bucket(dist), g_ix].astype(jnp.float32), -1, 2)
        s = jnp.einsum('bqghd,bgqnd->bghqn', qc, kv[..., 0, :]).astype(jnp.float32) * scale + bias
        p = masked_softmax(s, (dist >= 0)[:, :, None])
        return jnp.einsum('bghqn,bgqnd->bqghd', p.astype(dt), kv[..., 1, :])

    qs = query_block(Tq, SLC_Q_BLOCK)
    o_s = from_blocks(lax.map(slc_block, (to_blocks(qg, 1, qs), to_blocks(tok, 2, qs), q_pos.reshape(-1, qs))))

    qw = query_block(Tq, Q_BLOCK)

    def win_block(args):
        c, qc = args
        kv = lax.dynamic_slice_in_dim(win_ext, c * qw, WINDOW + qw, axis=1)
        pos = p0 + c * qw + jnp.arange(qw)
        kpos = p0 - WINDOW + c * qw + jnp.arange(WINDOW + qw)
        dist = pos[:, None] - kpos[None, :]
        mask = (dist >= 0) & (dist <= WINDOW) & (kpos >= 0)[None, :]
        s = jnp.einsum('bqghd,bkgd->bghqk', qc, kv[:, :, 0]).astype(jnp.float32) * scale + head_bias(bias_table, dist)
        p = masked_softmax(s, mask)
        return jnp.einsum('bghqk,bkgd->bqghd', p.astype(dt), kv[:, :, 1])

    o_w = from_blocks(lax.map(win_block, (jnp.arange(Tq // qw), to_blocks(qg, 1, qw))))

    g = gates.reshape(B, Tq, G, HPG, 3)
    o = g[..., 0:1] * o_c + g[..., 1:2] * o_s + g[..., 2:3] * o_w
    return o.reshape(B, Tq, NSA_Q)


def fox(q, kv, logf):
    B, Tq = q.shape[:2]
    Tk = kv.shape[1]
    p0 = Tk - Tq
    dt = q.dtype
    scale = HEAD_DIM ** -0.5
    cum = jnp.cumsum(logf.astype(jnp.float32), axis=1)
    cum_k = jnp.moveaxis(cum, 1, 2)
    k_all, v_all = kv[:, :, 0], kv[:, :, 1]
    kpos = jnp.arange(Tk)

    def blk(args):
        qc, cq, pos = args
        s = jnp.einsum('bqhd,bkhd->bhqk', qc, k_all).astype(jnp.float32) * scale
        s = s + jnp.moveaxis(cq, 1, 2)[..., None] - cum_k[:, :, None, :]
        p = masked_softmax(s, kpos[None, :] <= pos[:, None])
        return jnp.einsum('bhqk,bkhd->bqhd', p.astype(dt), v_all)

    qb = query_block(Tq, Q_BLOCK)
    o = from_blocks(lax.map(blk, (to_blocks(q, 1, qb), to_blocks(cum[:, p0:], 1, qb),
                                  (p0 + jnp.arange(Tq)).reshape(-1, qb))))
    return o.reshape(B, Tq, FOX_W)


def peer(x, w_q, sub_keys, u_table, v_table):
    n, d = x.shape
    c = min(TOKEN_CHUNK, n)
    n_pad = -(-n // c) * c
    xp = jnp.pad(x, ((0, n_pad - n), (0, 0)))
    q_all = _matmul(xp, w_q)

    def chunk(args):
        xc, qc = args
        qh = qc.reshape(c, PEER_HEADS, 2, PEER_QDIM // 2)
        s = jnp.einsum('chpd,hpkd->chpk', qh, sub_keys).astype(jnp.float32)
        sv, si = lax.top_k(s, PEER_TOPK)
        cand = (sv[:, :, 0, :, None] + sv[:, :, 1, None, :]).reshape(c, PEER_HEADS, PEER_TOPK * PEER_TOPK)
        best, bi = lax.top_k(cand, PEER_TOPK)
        i1 = jnp.take_along_axis(si[:, :, 0], bi // PEER_TOPK, axis=-1)
        i2 = jnp.take_along_axis(si[:, :, 1], bi % PEER_TOPK, axis=-1)
        experts = i1 * N_KEYS + i2
        gate = jax.nn.softmax(best, axis=-1)
        act = jax.nn.gelu(jnp.einsum('cd,chkd->chk', xc, u_table[experts]).astype(jnp.float32))
        return jnp.einsum('chk,chkd->cd', (gate * act).astype(x.dtype), v_table[experts])

    y = lax.map(chunk, (xp.reshape(n_pad // c, c, d), q_all.reshape(n_pad // c, c, -1)))
    return y.reshape(n_pad, d)[:n]


def layer_forward(x, past, w_in, b_forget, cmp_pos, cmp_w1, cmp_w2, bias_table, g_nsa, g_fox, w_out,
                  ln1_g, ln1_b, peer_w_q, peer_keys, peer_u, peer_v, ln2_g, ln2_b):
    B, T, D = x.shape
    qn, kvn, gates, qf, kvf, logf = project(x, w_in, b_forget)
    nsa_rows, win_rows = kvn[:, :, :2], kvn[:, :, 2]
    nsa_past, win_past, fox_past, logf_past = past
    buf_len = win_past.shape[1]
    nsa_full = jnp.concatenate([nsa_past, nsa_rows], axis=1)
    fox_full = jnp.concatenate([fox_past, kvf], axis=1)
    logf_full = jnp.concatenate([logf_past.astype(jnp.float32), logf], axis=1)
    pad = jnp.zeros((B, WINDOW - buf_len) + win_rows.shape[2:], win_rows.dtype)
    win_ext = jnp.concatenate([pad, win_past, win_rows], axis=1)
    o_n = nsa(qn, nsa_full, win_ext, gates, bias_table, cmp_pos, cmp_w1, cmp_w2)
    o_f = fox(qf, fox_full, logf_full)
    h = _post_attention(o_n.reshape(B * T, -1), o_f.reshape(B * T, -1), x.reshape(B * T, D),
                        g_nsa, g_fox, w_out, ln1_g, ln1_b)
    f = peer(h, peer_w_q, peer_keys, peer_u, peer_v)
    y = _add_ln(h, f, ln2_g, ln2_b).reshape(B, T, D)
    return y, nsa_rows, win_ext[:, win_ext.shape[1] - buf_len:], kvf, logf


def prompt_forward(x, w_in, b_forget, cmp_pos, cmp_w1, cmp_w2, bias_table, g_nsa, g_fox, w_out,
                   ln1_g, ln1_b, peer_w_q, peer_keys, peer_u, peer_v, ln2_g, ln2_b):
    B, T, D = x.shape
    G = NSA_KV_GROUPS
    x2 = x.reshape(B * T, D)
    w_big, w_small, b_small = _permute_w_in(w_in, b_forget)
    yb = _matmul(x2, w_big)
    ys = _proj_small(x2, w_small, b_small)
    yb3 = yb.reshape(B, T, BIG_WIDTH)
    ys3 = ys.reshape(B, T, SMALL_WIDTH)
    logf = ys3[:, :, 2 * LANE:2 * LANE + FOX_HEADS]
    kvn = yb3[:, :, CB_KVN * LANE:CB_QF * LANE].reshape(B, T, 3, 2, G, HEAD_DIM)
    kvf = yb3[:, :, CB_KF * LANE:].reshape(B, T, 2, FOX_HEADS, HEAD_DIM)

    kcvc = _compress_prompt(yb3, cmp_pos, cmp_w1, cmp_w2)
    o_n = _nsa_prompt(yb3, ys3, kcvc, bias_table)
    o_f = _fox_prompt(yb3, logf)
    h = _post_attention(o_n.reshape(B * T, NSA_Q), o_f.reshape(B * T, FOX_W), x2,
                        g_nsa, g_fox, w_out, ln1_g, ln1_b)
    f = peer(h, peer_w_q, peer_keys, peer_u, peer_v)
    y = _add_ln(h, f, ln2_g, ln2_b).reshape(B, T, D)
    buf_len = min(WINDOW, T)
    return y, kvn[:, :, :2], kvn[:, T - buf_len:, 2], kvf, logf


def kernel(x_prompt, x_sample, cache_nsa_kv, cache_nsa_win, cache_fox_kv, cache_fox_logf, page_table,
           w_in, b_forget, nsa_cmp_pos, nsa_cmp_w1, nsa_cmp_w2, rel_bias_table, g_nsa, g_fox, w_out,
           ln1_g, ln1_b, peer_w_q, peer_sub_keys, peer_u, peer_v, ln2_g, ln2_b):
    layer = 0
    w = (w_in[layer], b_forget[layer], nsa_cmp_pos[layer], nsa_cmp_w1[layer], nsa_cmp_w2[layer],
         rel_bias_table, g_nsa[layer], g_fox[layer], w_out[layer], ln1_g[layer], ln1_b[layer],
         peer_w_q[layer], peer_sub_keys[layer], peer_u[layer], peer_v[layer], ln2_g[layer], ln2_b[layer])
    yp, a_nsa, a_win, a_fox, a_logf = prompt_forward(x_prompt, *w)
    past = (gather_pages(cache_nsa_kv[layer], page_table), cache_nsa_win[layer],
            gather_pages(cache_fox_kv[layer], page_table), gather_pages(cache_fox_logf[layer], page_table))
    ys, b_nsa, b_win, b_fox, b_logf = layer_forward(x_sample, past, *w)
    return (yp, ys, a_nsa[None], a_win[None], a_fox[None], a_logf[None],
            b_nsa[None], b_win[None], b_fox[None], b_logf[None])
```

```python
import functools
import math

import jax
import jax.numpy as jnp
import numpy as np
from jax import lax
from jax.experimental import pallas as pl
from jax.experimental.pallas import tpu as pltpu

D_MODEL = 2048
HEAD_DIM = 128
NSA_HEADS = 8
NSA_KV_GROUPS = 2
NSA_HPG = NSA_HEADS // NSA_KV_GROUPS
CMP_BLOCK = 32
CMP_STRIDE = 16
SLC_BLOCK = 64
N_SELECT = 16
WINDOW = 512
FOX_HEADS = 8
NSA_Q = NSA_HEADS * HEAD_DIM
NSA_KV = NSA_KV_GROUPS * HEAD_DIM
FOX_W = FOX_HEADS * HEAD_DIM
IN_SIZES = (NSA_Q, 6 * NSA_KV, 3 * NSA_HEADS, FOX_W, 2 * FOX_W, FOX_HEADS)
N_BUCKETS = 32
MAX_DISTANCE = 128
PEER_HEADS = 8
N_KEYS = 128
PEER_TOPK = 16
PEER_QDIM = 256
Q_BLOCK = 128
SLC_Q_BLOCK = 32
TOKEN_CHUNK = 128
DEPTH = 1
ALPHA = (2.0 * DEPTH) ** 0.25
LN_EPS = 1e-5
NEG = -1e30
FORCE_BONUS = 1e4
SCALE = HEAD_DIM ** -0.5

LANE = 128
VMEM_LIMIT = 48 * 1024 * 1024

BIG_WIDTH = NSA_Q + 6 * NSA_KV + FOX_W + 2 * FOX_W
CB_QN = 0
CB_KVN = NSA_Q // LANE
CB_QF = CB_KVN + 6 * NSA_KV // LANE
CB_KF = CB_QF + FOX_W // LANE
CB_VF = CB_KF + FOX_W // LANE
SMALL_WIDTH = 3 * LANE


def _dot_nt(a, b):
    return lax.dot_general(a, b, (((1,), (1,)), ((), ())), preferred_element_type=jnp.float32)


def _dot(a, b):
    return jnp.dot(a, b, preferred_element_type=jnp.float32)


def _mm_kernel(x_ref, w_ref, o_ref):
    o_ref[...] = _dot(x_ref[...].astype(jnp.bfloat16), w_ref[...])


def _matmul(x, w, tm=512, tn=512):
    m, k = x.shape
    n = w.shape[1]
    tm = min(tm, m)
    n_pad = -(-n // tn) * tn
    wb = w.astype(jnp.bfloat16)
    if n_pad != n:
        wb = jnp.pad(wb, ((0, 0), (0, n_pad - n)))
    out = pl.pallas_call(
        _mm_kernel,
        grid=(m // tm, n_pad // tn),
        in_specs=[pl.BlockSpec((tm, k), lambda i, j: (i, 0)),
                  pl.BlockSpec((k, tn), lambda i, j: (0, j))],
        out_specs=pl.BlockSpec((tm, tn), lambda i, j: (i, j)),
        out_shape=jax.ShapeDtypeStruct((m, n_pad), jnp.float32),
        compiler_params=pltpu.CompilerParams(
            dimension_semantics=("parallel", "arbitrary"),
            vmem_limit_bytes=VMEM_LIMIT),
        name="dense_matmul",
    )(x, wb)
    return out[:, :n] if n_pad != n else out


def _proj_small_kernel(x_ref, w_ref, b_ref, o_ref):
    y = _dot(x_ref[...].astype(jnp.bfloat16), w_ref[...]) + b_ref[...]
    gates = y[:, :2 * LANE]
    o_ref[:, :2 * LANE] = 1.0 / (1.0 + jnp.exp(-gates))
    f = y[:, 2 * LANE:]
    o_ref[:, 2 * LANE:] = -(jnp.maximum(-f, 0.0) + jnp.log1p(jnp.exp(-jnp.abs(f))))


def _proj_small(x, w_small, b_small, tm=512):
    m, k = x.shape
    tm = min(tm, m)
    return pl.pallas_call(
        _proj_small_kernel,
        grid=(m // tm,),
        in_specs=[pl.BlockSpec((tm, k), lambda i: (i, 0)),
                  pl.BlockSpec((k, SMALL_WIDTH), lambda i: (0, 0)),
                  pl.BlockSpec((1, SMALL_WIDTH), lambda i: (0, 0))],
        out_specs=pl.BlockSpec((tm, SMALL_WIDTH), lambda i: (i, 0)),
        out_shape=jax.ShapeDtypeStruct((m, SMALL_WIDTH), jnp.float32),
        compiler_params=pltpu.CompilerParams(
            dimension_semantics=("parallel",), vmem_limit_bytes=VMEM_LIMIT),
        name="proj_small",
    )(x, w_small, b_small)


def _permute_w_in(w_in, b_forget):
    offs = [0] + [int(o) for o in np.cumsum(IN_SIZES)]
    q_n, kv_n, gate, q_f, kv_f, f_f = (w_in[:, offs[i]:offs[i + 1]] for i in range(6))
    w_big = jnp.concatenate([q_n, kv_n, q_f, kv_f], axis=1).astype(jnp.bfloat16)
    d = w_in.shape[0]
    n_gate = 3 * NSA_HPG
    zg = jnp.zeros((d, LANE - n_gate), w_in.dtype)
    zf = jnp.zeros((d, LANE - FOX_HEADS), w_in.dtype)
    w_small = jnp.concatenate([gate[:, :n_gate], zg, gate[:, n_gate:], zg, f_f, zf], axis=1).astype(jnp.bfloat16)
    b_small = jnp.concatenate([jnp.zeros((2 * LANE,), jnp.float32), b_forget.astype(jnp.float32),
                               jnp.zeros((LANE - FOX_HEADS,), jnp.float32)])[None]
    return w_big, w_small, b_small


def _gelu_tanh(h):
    return 0.5 * h * (1.0 + jnp.tanh(math.sqrt(2.0 / math.pi) * (h + 0.044715 * (h * h * h))))


def _compress_kernel(k_ref, pe_ref, w1_ref, w2_ref, o_ref, *, nh):
    def half(s0):
        acc = jnp.zeros((nh, HEAD_DIM), jnp.float32)
        for s in range(CMP_STRIDE):
            rows = k_ref[0, pl.ds(s, nh, stride=CMP_STRIDE), :] + pe_ref[0, s0 + s:s0 + s + 1, :]
            acc = acc + _dot(rows.astype(jnp.bfloat16), w1_ref[0, s0 + s])
        return acc

    first = half(0)
    second = half(CMP_STRIDE)
    h = first + pltpu.roll(second, nh - 1, 0)
    o_ref[0, 0] = _dot(_gelu_tanh(h).astype(jnp.bfloat16), w2_ref[0])


def _compress_prompt(yb3, cmp_pos, cmp_w1, cmp_w2):
    b, t, _ = yb3.shape
    nh = t // CMP_STRIDE
    n_kg = 2 * NSA_KV_GROUPS
    return pl.pallas_call(
        functools.partial(_compress_kernel, nh=nh),
        grid=(b, n_kg),
        in_specs=[pl.BlockSpec((1, t, LANE), lambda i, c: (i, 0, CB_KVN + c)),
                  pl.BlockSpec((1, CMP_BLOCK, HEAD_DIM), lambda i, c: (c // NSA_KV_GROUPS, 0, 0)),
                  pl.BlockSpec((1, CMP_BLOCK, HEAD_DIM, HEAD_DIM), lambda i, c: (c // NSA_KV_GROUPS, 0, 0, 0)),
                  pl.BlockSpec((1, HEAD_DIM, HEAD_DIM), lambda i, c: (c // NSA_KV_GROUPS, 0, 0))],
        out_specs=pl.BlockSpec((1, 1, nh, HEAD_DIM), lambda i, c: (i, c, 0, 0)),
        out_shape=jax.ShapeDtypeStruct((b, n_kg, nh, HEAD_DIM), jnp.float32),
        compiler_params=pltpu.CompilerParams(
            dimension_semantics=("parallel", "arbitrary"), vmem_limit_bytes=VMEM_LIMIT),
        name="nsa_compress",
    )(yb3, cmp_pos, cmp_w1.astype(jnp.bfloat16), cmp_w2.astype(jnp.bfloat16))


def _nsa_kernel(q_ref, kc_ref, vc_ref, ks_ref, vs_ref, kw_ref, vw_ref, gate_ref, biasc_ref, biast_ref,
                c2s_ref, expand_ref, o_ref, q_s, sel_s, m_s, l_s, acc_s, out_s, *, n_slc, n_top):
    i = pl.program_id(2)
    rows_q = NSA_HPG * LANE
    bf16 = jnp.bfloat16

    qt = q_ref[0]
    q_s[...] = jnp.concatenate([qt[:, h * LANE:(h + 1) * LANE] for h in range(NSA_HPG)], axis=0).astype(bf16)

    r = lax.broadcasted_iota(jnp.int32, (rows_q, LANE), 0) & (LANE - 1)
    c = lax.broadcasted_iota(jnp.int32, (rows_q, LANE), 1)
    q_pos = i * LANE + r
    s = _dot_nt(q_s[...], kc_ref[0, 0].astype(bf16)) * SCALE + biasc_ref[...].reshape(rows_q, LANE)
    mask = c * CMP_STRIDE + (CMP_BLOCK - 1) <= q_pos
    s = jnp.where(mask, s, NEG)
    p = jnp.where(mask, jnp.exp(s - jnp.max(s, -1, keepdims=True)), 0.0)
    p = p / jnp.maximum(jnp.sum(p, -1, keepdims=True), 1e-30)
    out_s[0] = _dot(p.astype(bf16), vc_ref[0, 0].astype(bf16))

    p_sum = p[0:LANE]
    for h in range(1, NSA_HPG):
        p_sum = p_sum + p[h * LANE:(h + 1) * LANE]
    p_hi = p_sum.astype(bf16)
    p_lo = (p_sum - p_hi.astype(jnp.float32)).astype(bf16)
    imp = _dot(p_hi, c2s_ref[...]) + _dot(p_lo, c2s_ref[...])
    jj = lax.broadcasted_iota(jnp.int32, (LANE, LANE), 1)
    qp = i * LANE + lax.broadcasted_iota(jnp.int32, (LANE, LANE), 0)
    cur = qp >> int(math.log2(SLC_BLOCK))
    forced = (jj == 0) | (jj == cur) | (jj == cur - 1)
    val = jnp.where(jj * SLC_BLOCK <= qp, imp + jnp.where(forced, FORCE_BONUS, 0.0), NEG)
    val = jnp.where(jj < n_slc, val, -3e38)
    rank = jnp.zeros((LANE, LANE), jnp.int32)
    for t in range(n_slc):
        col = val[:, t:t + 1]
        ahead = (col > val) | ((col == val) & (jj > t))
        rank = rank + ahead.astype(jnp.int32)
    sel_s[...] = jnp.where(rank < n_top, 1.0, 0.0).astype(bf16)

    def attend(k_ref, v_ref, lo, use_sel, use_win, slot):
        m_s[...] = jnp.full((rows_q, 1), NEG, jnp.float32)
        l_s[...] = jnp.zeros((rows_q, 1), jnp.float32)
        acc_s[...] = jnp.zeros((rows_q, HEAD_DIM), jnp.float32)

        def body(kt, carry):
            off = pl.multiple_of(kt * LANE, LANE)
            k = k_ref[0, pl.ds(off, LANE), :].astype(bf16)
            v = v_ref[0, pl.ds(off, LANE), :].astype(bf16)
            dq = i - kt
            sc = _dot_nt(q_s[...], k) * SCALE + biast_ref[0, jnp.minimum(dq, 2)]
            rr = lax.broadcasted_iota(jnp.int32, (rows_q, LANE), 0) & (LANE - 1)
            cc = lax.broadcasted_iota(jnp.int32, (rows_q, LANE), 1)
            dist = rr - cc + dq * LANE
            msk = dist >= 0
            if use_win:
                msk = msk & (dist <= WINDOW)
            if use_sel:
                se = _dot(sel_s[...], expand_ref[kt])
                msk = msk & (jnp.concatenate([se] * NSA_HPG, axis=0) > 0.5)
            sc = jnp.where(msk, sc, NEG)
            m_old = m_s[...]
            m_new = jnp.maximum(m_old, jnp.max(sc, -1, keepdims=True))
            pe = jnp.where(msk, jnp.exp(sc - m_new), 0.0)
            alpha = jnp.exp(m_old - m_new)
            l_s[...] = alpha * l_s[...] + jnp.sum(pe, -1, keepdims=True)
            acc_s[...] = alpha * acc_s[...] + _dot(pe.astype(bf16), v)
            m_s[...] = m_new
            return carry

        lax.fori_loop(lo, i + 1, body, 0)
        out_s[slot] = acc_s[...] / jnp.maximum(l_s[...], 1e-30)

    attend(ks_ref, vs_ref, 0, True, False, 1)
    attend(kw_ref, vw_ref, jnp.maximum(i - WINDOW // LANE, 0), False, True, 2)

    gt = gate_ref[0]
    for h in range(NSA_HPG):
        sl = slice(h * LANE, (h + 1) * LANE)
        o_ref[0, :, sl] = (gt[:, 3 * h:3 * h + 1] * out_s[0, sl, :]
                           + gt[:, 3 * h + 1:3 * h + 2] * out_s[1, sl, :]
                           + gt[:, 3 * h + 2:3 * h + 3] * out_s[2, sl, :])


def _t5_bucket_np(d):
    max_exact = N_BUCKETS // 2
    d = np.maximum(d, 0)
    large = max_exact + (np.log(np.maximum(d, 1).astype(np.float32) / np.float32(max_exact))
                         / np.float32(math.log(MAX_DISTANCE / max_exact)) * (N_BUCKETS - max_exact)).astype(np.int32)
    return np.where(d < max_exact, d, np.minimum(large, N_BUCKETS - 1)).astype(np.int32)


def _nsa_prompt(yb3, ys3, kcvc, bias_table):
    b, t, _ = yb3.shape
    n_t = t // LANE
    n_slc = t // SLC_BLOCK
    n_cmp = (t - CMP_BLOCK) // CMP_STRIDE + 1
    n_top = min(N_SELECT, n_slc)
    rows_q = NSA_HPG * LANE
    assert t % LANE == 0 and n_cmp <= LANE and n_slc <= LANE

    table = bias_table.astype(jnp.float32)
    dist_c = np.arange(t)[:, None] - (np.arange(LANE) * CMP_STRIDE + CMP_BLOCK - 1)[None, :]
    bias_c = jnp.moveaxis(table[_t5_bucket_np(dist_c)], -1, 0)
    rc = np.arange(LANE)[:, None] - np.arange(LANE)[None, :]
    buckets_t = np.stack([_t5_bucket_np(rc), _t5_bucket_np(rc + LANE), _t5_bucket_np(rc + 2 * LANE)])
    assert (_t5_bucket_np(np.arange(LANE + 1, 4 * LANE)) == N_BUCKETS - 1).all()
    bias_t = jnp.moveaxis(table[buckets_t], -1, 0)
    bias_t = bias_t.reshape(NSA_KV_GROUPS, NSA_HPG, 3, LANE, LANE).transpose(0, 2, 1, 3, 4)
    bias_t = bias_t.reshape(NSA_KV_GROUPS, 3, rows_q, LANE)

    c0 = np.arange(n_cmp) * CMP_STRIDE
    s0 = np.arange(n_slc) * SLC_BLOCK
    ov = np.minimum(c0[:, None] + CMP_BLOCK, s0[None, :] + SLC_BLOCK) - np.maximum(c0[:, None], s0[None, :])
    c2s = np.zeros((LANE, LANE), np.float32)
    c2s[:n_cmp, :n_slc] = np.maximum(ov, 0) / CMP_STRIDE
    expand = np.zeros((n_t, LANE, LANE), np.float32)
    for kt in range(n_t):
        tok_blk = (kt * LANE + np.arange(LANE)) // SLC_BLOCK
        expand[kt, tok_blk, np.arange(LANE)] = 1.0

    kv_spec = lambda cb: pl.BlockSpec((1, t, LANE), lambda bi, g, i: (bi, 0, cb + g))
    g_n = NSA_KV_GROUPS
    return pl.pallas_call(
        functools.partial(_nsa_kernel, n_slc=n_slc, n_top=n_top),
        grid=(b, g_n, n_t),
        in_specs=[
            pl.BlockSpec((1, LANE, rows_q), lambda bi, g, i: (bi, i, g)),
            pl.BlockSpec((1, 1, t // CMP_STRIDE, HEAD_DIM), lambda bi, g, i: (bi, g, 0, 0)),
            pl.BlockSpec((1, 1, t // CMP_STRIDE, HEAD_DIM), lambda bi, g, i: (bi, g_n + g, 0, 0)),
            kv_spec(CB_KVN + 2 * g_n), kv_spec(CB_KVN + 3 * g_n),
            kv_spec(CB_KVN + 4 * g_n), kv_spec(CB_KVN + 5 * g_n),
            pl.BlockSpec((1, LANE, LANE), lambda bi, g, i: (bi, i, g)),
            pl.BlockSpec((NSA_HPG, LANE, LANE), lambda bi, g, i: (g, i, 0)),
            pl.BlockSpec((1, 3, rows_q, LANE), lambda bi, g, i: (g, 0, 0, 0)),
            pl.BlockSpec((LANE, LANE), lambda bi, g, i: (0, 0)),
            pl.BlockSpec((n_t, LANE, LANE), lambda bi, g, i: (0, 0, 0)),
        ],
        out_specs=pl.BlockSpec((1, LANE, rows_q), lambda bi, g, i: (bi, i, g)),
        out_shape=jax.ShapeDtypeStruct((b, t, NSA_Q), jnp.float32),
        scratch_shapes=[pltpu.VMEM((rows_q, HEAD_DIM), jnp.bfloat16),
                        pltpu.VMEM((LANE, LANE), jnp.bfloat16),
                        pltpu.VMEM((rows_q, 1), jnp.float32),
                        pltpu.VMEM((rows_q, 1), jnp.float32),
                        pltpu.VMEM((rows_q, HEAD_DIM), jnp.float32),
                        pltpu.VMEM((3, rows_q, HEAD_DIM), jnp.float32)],
        compiler_params=pltpu.CompilerParams(
            dimension_semantics=("parallel", "parallel", "arbitrary"), vmem_limit_bytes=VMEM_LIMIT),
        name="nsa_prompt",
    )(yb3, kcvc, kcvc, yb3, yb3, yb3, yb3, ys3, bias_c, bias_t,
      jnp.asarray(c2s, jnp.bfloat16), jnp.asarray(expand, jnp.bfloat16))


FOX_TILE = 256


def _fox_kernel(q_ref, k_ref, v_ref, cq_ref, ck_ref, o_ref, m_s, l_s, acc_s):
    i = pl.program_id(2)
    tq = FOX_TILE
    bf16 = jnp.bfloat16
    q = q_ref[0].astype(bf16)
    cq = cq_ref[0, 0]
    m_s[...] = jnp.full((tq, 1), NEG, jnp.float32)
    l_s[...] = jnp.zeros((tq, 1), jnp.float32)
    acc_s[...] = jnp.zeros((tq, HEAD_DIM), jnp.float32)

    def body(kt, carry):
        off = pl.multiple_of(kt * tq, tq)
        k = k_ref[0, pl.ds(off, tq), :].astype(bf16)
        v = v_ref[0, pl.ds(off, tq), :].astype(bf16)
        sc = _dot_nt(q, k) * SCALE + cq - ck_ref[0, 0, kt]
        rr = lax.broadcasted_iota(jnp.int32, (tq, tq), 0)
        cc = lax.broadcasted_iota(jnp.int32, (tq, tq), 1)
        msk = cc + (kt - i) * tq <= rr
        sc = jnp.where(msk, sc, NEG)
        m_old = m_s[...]
        m_new = jnp.maximum(m_old, jnp.max(sc, -1, keepdims=True))
        pe = jnp.where(msk, jnp.exp(sc - m_new), 0.0)
        alpha = jnp.exp(m_old - m_new)
        l_s[...] = alpha * l_s[...] + jnp.sum(pe, -1, keepdims=True)
        acc_s[...] = alpha * acc_s[...] + _dot(pe.astype(bf16), v)
        m_s[...] = m_new
        return carry

    lax.fori_loop(0, i + 1, body, 0)
    o_ref[0] = acc_s[...] / jnp.maximum(l_s[...], 1e-30)


def _fox_prompt(yb3, logf):
    b, t, _ = yb3.shape
    tq = FOX_TILE
    n_t = t // tq
    cum = jnp.moveaxis(jnp.cumsum(logf.astype(jnp.float32), axis=1), 1, 2)
    cum_q = cum[..., None]
    cum_k = cum.reshape(b, FOX_HEADS, n_t, 1, tq)
    return pl.pallas_call(
        _fox_kernel,
        grid=(b, FOX_HEADS, n_t),
        in_specs=[pl.BlockSpec((1, tq, LANE), lambda bi, h, i: (bi, i, CB_QF + h)),
                  pl.BlockSpec((1, t, LANE), lambda bi, h, i: (bi, 0, CB_KF + h)),
                  pl.BlockSpec((1, t, LANE), lambda bi, h, i: (bi, 0, CB_VF + h)),
                  pl.BlockSpec((1, 1, tq, 1), lambda bi, h, i: (bi, h, i, 0)),
                  pl.BlockSpec((1, 1, n_t, 1, tq), lambda bi, h, i: (bi, h, 0, 0, 0))],
        out_specs=pl.BlockSpec((1, tq, LANE), lambda bi, h, i: (bi, i, h)),
        out_shape=jax.ShapeDtypeStruct((b, t, FOX_W), jnp.float32),
        scratch_shapes=[pltpu.VMEM((tq, 1), jnp.float32),
                        pltpu.VMEM((tq, 1), jnp.float32),
                        pltpu.VMEM((tq, HEAD_DIM), jnp.float32)],
        compiler_params=pltpu.CompilerParams(
            dimension_semantics=("parallel", "parallel", "arbitrary"), vmem_limit_bytes=VMEM_LIMIT),
        name="fox_prompt",
    )(yb3, yb3, yb3, cum_q, cum_k)


def _ln(z, g, b):
    mu = jnp.mean(z, -1, keepdims=True)
    zc = z - mu
    var = jnp.mean(zc * zc, -1, keepdims=True)
    return zc * lax.rsqrt(var + LN_EPS) * g + b


def _post_kernel(on_ref, of_ref, x_ref, gn_ref, gf_ref, w_ref, lg_ref, lb_ref, h_ref):
    def rms(o, g):
        return (o * lax.rsqrt(jnp.mean(o * o, -1, keepdims=True) + LN_EPS) * g).astype(jnp.bfloat16)

    mix = (_dot(rms(on_ref[...], gn_ref[...]), w_ref[:NSA_Q, :])
           + _dot(rms(of_ref[...], gf_ref[...]), w_ref[NSA_Q:, :]))
    h_ref[...] = _ln(ALPHA * x_ref[...] + mix, lg_ref[...], lb_ref[...])


def _post_attention(o_n, o_f, x, g_nsa, g_fox, w_out, ln_g, ln_b, tm=256):
    m, d = x.shape
    tm = min(tm, m)
    row = lambda n: pl.BlockSpec((1, n), lambda i: (0, 0))
    return pl.pallas_call(
        _post_kernel,
        grid=(m // tm,),
        in_specs=[pl.BlockSpec((tm, NSA_Q), lambda i: (i, 0)),
                  pl.BlockSpec((tm, FOX_W), lambda i: (i, 0)),
                  pl.BlockSpec((tm, d), lambda i: (i, 0)),
                  row(NSA_Q), row(FOX_W),
                  pl.BlockSpec((NSA_Q + FOX_W, d), lambda i: (0, 0)),
                  row(d), row(d)],
        out_specs=pl.BlockSpec((tm, d), lambda i: (i, 0)),
        out_shape=jax.ShapeDtypeStruct((m, d), jnp.float32),
        compiler_params=pltpu.CompilerParams(
            dimension_semantics=("parallel",), vmem_limit_bytes=VMEM_LIMIT),
        name="post_attention",
    )(o_n, o_f, x, g_nsa[None], g_fox[None], w_out.astype(jnp.bfloat16), ln_g[None], ln_b[None])


def _add_ln_kernel(h_ref, f_ref, g_ref, b_ref, o_ref):
    o_ref[...] = _ln(ALPHA * h_ref[...] + f_ref[...], g_ref[...], b_ref[...])


def _add_ln(h, f, ln_g, ln_b, tm=512):
    m, d = h.shape
    tm = min(tm, m)
    return pl.pallas_call(
        _add_ln_kernel,
        grid=(m // tm,),
        in_specs=[pl.BlockSpec((tm, d), lambda i: (i, 0)),
                  pl.BlockSpec((tm, d), lambda i: (i, 0)),
                  pl.BlockSpec((1, d), lambda i: (0, 0)),
                  pl.BlockSpec((1, d), lambda i: (0, 0))],
        out_specs=pl.BlockSpec((tm, d), lambda i: (i, 0)),
        out_shape=jax.ShapeDtypeStruct((m, d), jnp.float32),
        compiler_params=pltpu.CompilerParams(
            dimension_semantics=("parallel",), vmem_limit_bytes=VMEM_LIMIT),
        name="add_layer_norm",
    )(h, f, ln_g[None], ln_b[None])


PEER_TILE = 128
N_ROUTES = PEER_HEADS * PEER_TOPK


def _top_rows(vals, row_id, n_out, payload=None):
    big = float(vals.shape[0])
    out_v, out_i = [], []
    for _ in range(n_out):
        m = jnp.max(vals, axis=0, keepdims=True)
        win = jnp.min(jnp.where(vals == m, row_id, big), axis=0, keepdims=True)
        hit = row_id == win
        out_v.append(m)
        if payload is None:
            out_i.append(win)
        else:
            out_i.append(jnp.sum(jnp.where(hit, payload, 0.0), axis=0, keepdims=True))
        vals = jnp.where(hit, -jnp.inf, vals)
    return jnp.concatenate(out_v, axis=0), jnp.concatenate(out_i, axis=0)


def _peer_route_kernel(h_ref, wq_ref, keys_ref, g_ref, e_ref, sv_s, si_s):
    bf16 = jnp.bfloat16
    tm = PEER_TILE
    half = PEER_QDIM // 2
    q = _dot(h_ref[...].astype(bf16), wq_ref[...]).astype(bf16)
    key_id = lax.broadcasted_iota(jnp.int32, (N_KEYS, tm), 0).astype(jnp.float32)
    for hp in range(2 * PEER_HEADS):
        s_t = _dot_nt(keys_ref[hp], q[:, hp * half:(hp + 1) * half])
        sv, si = _top_rows(s_t, key_id, PEER_TOPK)
        sv_s[hp] = sv
        si_s[hp] = si
    pair_id = lax.broadcasted_iota(jnp.int32, (PEER_TOPK * PEER_TOPK, tm), 0).astype(jnp.float32)
    for h in range(PEER_HEADS):
        sv0, sv1 = sv_s[2 * h], sv_s[2 * h + 1]
        si0, si1 = si_s[2 * h], si_s[2 * h + 1]
        cand = jnp.concatenate([sv0[a:a + 1, :] + sv1 for a in range(PEER_TOPK)], axis=0)
        expert = jnp.concatenate([si0[a:a + 1, :] * float(N_KEYS) + si1 for a in range(PEER_TOPK)], axis=0)
        best, eid = _top_rows(cand, pair_id, PEER_TOPK, payload=expert)
        ex = jnp.exp(best - best[0:1, :])
        g_ref[0, h * PEER_TOPK:(h + 1) * PEER_TOPK, :] = ex / jnp.sum(ex, axis=0, keepdims=True)
        e_ref[0, h * PEER_TOPK:(h + 1) * PEER_TOPK, :] = eid.astype(jnp.int32)


def _peer_route(h, w_q, sub_keys):
    n, d = h.shape
    tm = PEER_TILE
    nb = n // tm
    n_hp = 2 * PEER_HEADS
    half = PEER_QDIM // 2
    out = jax.ShapeDtypeStruct((nb, N_ROUTES, tm), jnp.float32)
    return pl.pallas_call(
        _peer_route_kernel,
        grid=(nb,),
        in_specs=[pl.BlockSpec((tm, d), lambda i: (i, 0)),
                  pl.BlockSpec((d, PEER_HEADS * PEER_QDIM), lambda i: (0, 0)),
                  pl.BlockSpec((n_hp, N_KEYS, half), lambda i: (0, 0, 0))],
        out_specs=[pl.BlockSpec((1, N_ROUTES, tm), lambda i: (i, 0, 0)),
                   pl.BlockSpec((1, N_ROUTES, tm), lambda i: (i, 0, 0))],
        out_shape=[out, jax.ShapeDtypeStruct((nb, N_ROUTES, tm), jnp.int32)],
        scratch_shapes=[pltpu.VMEM((n_hp, PEER_TOPK, tm), jnp.float32),
                        pltpu.VMEM((n_hp, PEER_TOPK, tm), jnp.float32)],
        compiler_params=pltpu.CompilerParams(
            dimension_semantics=("parallel",), vmem_limit_bytes=VMEM_LIMIT),
        name="peer_route",
    )(h, w_q.astype(jnp.bfloat16), sub_keys.reshape(n_hp, N_KEYS, half).astype(jnp.bfloat16))


def _peer_expert_kernel(e_ref, g_ref, x_ref, uv_hbm, o_ref, e_smem, buf, sem, esem):
    tm = PEER_TILE
    d = D_MODEL
    ids = pltpu.make_async_copy(e_ref.at[0], e_smem, esem)
    ids.start()
    ids.wait()

    def issue(t, slot):
        for k in range(N_ROUTES):
            pltpu.make_async_copy(uv_hbm.at[pl.ds(e_smem[k, t], 1), :],
                                  buf.at[slot, pl.ds(k, 1), :], sem.at[slot]).start()

    def wait_rows(slot):
        pltpu.make_async_copy(uv_hbm.at[pl.ds(0, N_ROUTES), :], buf.at[slot], sem.at[slot]).wait()

    issue(0, 0)

    def body(t, carry):
        slot = t & 1

        @pl.when(t + 1 < tm)
        def _():
            issue(t + 1, 1 - slot)

        wait_rows(slot)
        x_row = x_ref[pl.ds(t, 1), :]
        acc = buf[slot, :, 0:LANE] * x_row[:, 0:LANE]
        for c in range(1, d // LANE):
            acc = acc + buf[slot, :, c * LANE:(c + 1) * LANE] * x_row[:, c * LANE:(c + 1) * LANE]
        s = jnp.sum(acc, axis=1, keepdims=True)
        gate = pltpu.roll(g_ref[0], jnp.where(t == 0, 0, tm - t), 1)[:, 0:1]
        coef = gate * _gelu_tanh(s)
        pieces = [jnp.sum(buf[slot, :, d + c * LANE:d + (c + 1) * LANE] * coef, axis=0, keepdims=True)
                  for c in range(d // LANE)]
        o_ref[pl.ds(t, 1), :] = jnp.concatenate(pieces, axis=1)
        return carry

    lax.fori_loop(0, tm, body, 0)


def _peer_experts(h, gates, experts, uv):
    n, d = h.shape
    tm = PEER_TILE
    nb = n // tm
    return pl.pallas_call(
        _peer_expert_kernel,
        grid=(nb,),
        in_specs=[pl.BlockSpec((1, N_ROUTES, tm), lambda i: (i, 0, 0)),
                  pl.BlockSpec((1, N_ROUTES, tm), lambda i: (i, 0, 0)),
                  pl.BlockSpec((tm, d), lambda i: (i, 0)),
                  pl.BlockSpec(memory_space=pl.ANY)],
        out_specs=pl.BlockSpec((tm, d), lambda i: (i, 0)),
        out_shape=jax.ShapeDtypeStruct((n, d), jnp.float32),
        scratch_shapes=[pltpu.SMEM((N_ROUTES, tm), jnp.int32),
                        pltpu.VMEM((2, N_ROUTES, 2 * d), jnp.float32),
                        pltpu.SemaphoreType.DMA((2,)),
                        pltpu.SemaphoreType.DMA],
        compiler_params=pltpu.CompilerParams(
            dimension_semantics=("arbitrary",), vmem_limit_bytes=VMEM_LIMIT),
        name="peer_experts",
    )(experts, gates, h, uv)


def _peer(h, w_q, sub_keys, uv):
    gates, experts = _peer_route(h, w_q, sub_keys)
    return _peer_experts(h, gates, experts, uv)


def layer_norm(x, g, b):
    xf = x.astype(jnp.float32)
    mu = jnp.mean(xf, -1, keepdims=True)
    var = jnp.mean(jnp.square(xf - mu), -1, keepdims=True)
    return ((xf - mu) * lax.rsqrt(var + LN_EPS) * g + b).astype(x.dtype)


def rms_norm(x, g):
    xf = x.astype(jnp.float32)
    return (xf * lax.rsqrt(jnp.mean(xf * xf, -1, keepdims=True) + LN_EPS) * g).astype(x.dtype)


def masked_softmax(s, mask):
    s = jnp.where(mask, s, NEG)
    m = jnp.max(s, -1, keepdims=True)
    p = jnp.where(mask, jnp.exp(s - m), 0.0)
    return p / jnp.maximum(jnp.sum(p, -1, keepdims=True), 1e-30)


def t5_bucket(dist):
    max_exact = N_BUCKETS // 2
    d = jnp.maximum(dist, 0)
    large = max_exact + (jnp.log(jnp.maximum(d, 1).astype(jnp.float32) / max_exact)
                         / math.log(MAX_DISTANCE / max_exact) * (N_BUCKETS - max_exact)).astype(jnp.int32)
    return jnp.where(d < max_exact, d, jnp.minimum(large, N_BUCKETS - 1))


def head_bias(table, dist):
    b = jnp.moveaxis(table[t5_bucket(dist)].astype(jnp.float32), -1, 0)
    return b.reshape((NSA_KV_GROUPS, NSA_HPG) + dist.shape)


def query_block(t, cap):
    return t if t <= cap else cap


def to_blocks(x, axis, qb):
    n = x.shape[axis] // qb
    return jnp.moveaxis(x.reshape(x.shape[:axis] + (n, qb) + x.shape[axis + 1:]), axis, 0)


def from_blocks(y):
    y = jnp.moveaxis(y, 0, 1)
    return y.reshape((y.shape[0], y.shape[1] * y.shape[2]) + y.shape[3:])


def gather_pages(pool, page_table):
    g = pool[page_table]
    return g.reshape((g.shape[0], g.shape[1] * g.shape[2]) + g.shape[3:])


def cmp_to_slc(n_cmp, n_slc):
    c0 = np.arange(n_cmp) * CMP_STRIDE
    s0 = np.arange(n_slc) * SLC_BLOCK
    ov = np.minimum(c0[:, None] + CMP_BLOCK, s0[None, :] + SLC_BLOCK) - np.maximum(c0[:, None], s0[None, :])
    return jnp.asarray(np.maximum(ov, 0) / CMP_STRIDE, dtype=jnp.float32)


def project(x, w_in, b_forget):
    B, T, D = x.shape
    offs = [int(o) for o in np.cumsum(IN_SIZES)[:-1]]
    y = _matmul(x.reshape(B * T, D), w_in).reshape(B, T, -1)
    q_n, kv_n, gate_n, q_f, kv_f, f_f = jnp.split(y, offs, axis=-1)
    qn = q_n.reshape(B, T, NSA_HEADS, HEAD_DIM)
    kvn = kv_n.reshape(B, T, 3, 2, NSA_KV_GROUPS, HEAD_DIM)
    gates = jax.nn.sigmoid(gate_n).reshape(B, T, NSA_HEADS, 3)
    qf = q_f.reshape(B, T, FOX_HEADS, HEAD_DIM)
    kvf = kv_f.reshape(B, T, 2, FOX_HEADS, HEAD_DIM)
    logf = jax.nn.log_sigmoid((f_f + b_forget).astype(jnp.float32))
    return qn, kvn, gates, qf, kvf, logf


def compress(k, pos, w1, w2):
    B, T, G, dk = k.shape
    n_cmp = (T - CMP_BLOCK) // CMP_STRIDE + 1
    halves = k[:, :(n_cmp + 1) * CMP_STRIDE].reshape(B, n_cmp + 1, CMP_STRIDE, G, dk)
    pe = pos.reshape(2, CMP_STRIDE, 1, dk)
    w1r = w1.reshape(2, CMP_STRIDE, dk, w1.shape[-1])
    h = (jnp.einsum('bnsgd,sdh->bngh', halves[:, :-1] + pe[0], w1r[0])
         + jnp.einsum('bnsgd,sdh->bngh', halves[:, 1:] + pe[1], w1r[1]))
    return jax.nn.gelu(h) @ w2


def nsa(q, nsa_full, win_ext, gates, bias_table, cmp_pos, cmp_w1, cmp_w2):
    B, Tq = q.shape[:2]
    Tk = nsa_full.shape[1]
    p0 = Tk - Tq
    dt = q.dtype
    scale = HEAD_DIM ** -0.5
    G, HPG = NSA_KV_GROUPS, NSA_HPG
    qg = q.reshape(B, Tq, G, HPG, HEAD_DIM)
    q_pos = p0 + jnp.arange(Tq)

    kc = compress(nsa_full[:, :, 0, 0], cmp_pos[0], cmp_w1[0], cmp_w2[0])
    vc = compress(nsa_full[:, :, 0, 1], cmp_pos[1], cmp_w1[1], cmp_w2[1])
    n_cmp = kc.shape[1]
    dist_c = q_pos[:, None] - (jnp.arange(n_cmp) * CMP_STRIDE + CMP_BLOCK - 1)[None, :]
    s_c = jnp.einsum('bqghd,bngd->bghqn', qg, kc).astype(jnp.float32) * scale + head_bias(bias_table, dist_c)
    p_c = masked_softmax(s_c, dist_c >= 0)
    o_c = jnp.einsum('bghqn,bngd->bqghd', p_c.astype(dt), vc)

    n_slc = -(-Tk // SLC_BLOCK)
    imp = jnp.einsum('bghqn,nj->bgqj', p_c, cmp_to_slc(n_cmp, n_slc))
    blk = jnp.arange(n_slc)[None, :]
    cur = (q_pos // SLC_BLOCK)[:, None]
    forced = (blk == 0) | (blk == cur) | (blk == cur - 1)
    imp = jnp.where(blk * SLC_BLOCK <= q_pos[:, None], imp + FORCE_BONUS * forced, NEG)
    _, sel = lax.top_k(imp, min(N_SELECT, n_slc))
    tok = (sel[..., None] * SLC_BLOCK + jnp.arange(SLC_BLOCK)).reshape(B, G, Tq, -1)
    kv_s = jnp.pad(nsa_full[:, :, 1], ((0, 0), (0, n_slc * SLC_BLOCK - Tk), (0, 0), (0, 0), (0, 0)))
    kv_s = kv_s.transpose(0, 3, 1, 2, 4)
    b_ix = jnp.arange(B)[:, None, None, None]
    g_ix = jnp.arange(G)[None, :, None, None]
    table_g = bias_table.reshape(N_BUCKETS, G, HPG)

    def slc_block(args):
        qc, tc, pc = args
        kv = kv_s[b_ix, g_ix, tc]
        dist = pc[None, None, :, None] - tc
        bias = jnp.moveaxis(table_g[t5_bucket(dist), g_ix].astype(jnp.float32), -1, 2)
        s = jnp.einsum('bqghd,bgqnd->bghqn', qc, kv[..., 0, :]).astype(jnp.float32) * scale + bias
        p = masked_softmax(s, (dist >= 0)[:, :, None])
        return jnp.einsum('bghqn,bgqnd->bqghd', p.astype(dt), kv[..., 1, :])

    qs = query_block(Tq, SLC_Q_BLOCK)
    o_s = from_blocks(lax.map(slc_block, (to_blocks(qg, 1, qs), to_blocks(tok, 2, qs), q_pos.reshape(-1, qs))))

    qw = query_block(Tq, Q_BLOCK)

    def win_block(args):
        c, qc = args
        kv = lax.dynamic_slice_in_dim(win_ext, c * qw, WINDOW + qw, axis=1)
        pos = p0 + c * qw + jnp.arange(qw)
        kpos = p0 - WINDOW + c * qw + jnp.arange(WINDOW + qw)
        dist = pos[:, None] - kpos[None, :]
        mask = (dist >= 0) & (dist <= WINDOW) & (kpos >= 0)[None, :]
        s = jnp.einsum('bqghd,bkgd->bghqk', qc, kv[:, :, 0]).astype(jnp.float32) * scale + head_bias(bias_table, dist)
        p = masked_softmax(s, mask)
        return jnp.einsum('bghqk,bkgd->bqghd', p.astype(dt), kv[:, :, 1])

    o_w = from_blocks(lax.map(win_block, (jnp.arange(Tq // qw), to_blocks(qg, 1, qw))))

    g = gates.reshape(B, Tq, G, HPG, 3)
    o = g[..., 0:1] * o_c + g[..., 1:2] * o_s + g[..., 2:3] * o_w
    return o.reshape(B, Tq, NSA_Q)


def fox(q, kv, logf):
    B, Tq = q.shape[:2]
    Tk = kv.shape[1]
    p0 = Tk - Tq
    dt = q.dtype
    scale = HEAD_DIM ** -0.5
    cum = jnp.cumsum(logf.astype(jnp.float32), axis=1)
    cum_k = jnp.moveaxis(cum, 1, 2)
    k_all, v_all = kv[:, :, 0], kv[:, :, 1]
    kpos = jnp.arange(Tk)

    def blk(args):
        qc, cq, pos = args
        s = jnp.einsum('bqhd,bkhd->bhqk', qc, k_all).astype(jnp.float32) * scale
        s = s + jnp.moveaxis(cq, 1, 2)[..., None] - cum_k[:, :, None, :]
        p = masked_softmax(s, kpos[None, :] <= pos[:, None])
        return jnp.einsum('bhqk,bkhd->bqhd', p.astype(dt), v_all)

    qb = query_block(Tq, Q_BLOCK)
    o = from_blocks(lax.map(blk, (to_blocks(q, 1, qb), to_blocks(cum[:, p0:], 1, qb),
                                  (p0 + jnp.arange(Tq)).reshape(-1, qb))))
    return o.reshape(B, Tq, FOX_W)


def layer_forward(x, past, w_in, b_forget, cmp_pos, cmp_w1, cmp_w2, bias_table, g_nsa, g_fox, w_out,
                  ln1_g, ln1_b, peer_w_q, peer_keys, peer_uv, ln2_g, ln2_b):
    B, T, D = x.shape
    qn, kvn, gates, qf, kvf, logf = project(x, w_in, b_forget)
    nsa_rows, win_rows = kvn[:, :, :2], kvn[:, :, 2]
    nsa_past, win_past, fox_past, logf_past = past
    buf_len = win_past.shape[1]
    nsa_full = jnp.concatenate([nsa_past, nsa_rows], axis=1)
    fox_full = jnp.concatenate([fox_past, kvf], axis=1)
    logf_full = jnp.concatenate([logf_past.astype(jnp.float32), logf], axis=1)
    pad = jnp.zeros((B, WINDOW - buf_len) + win_rows.shape[2:], win_rows.dtype)
    win_ext = jnp.concatenate([pad, win_past, win_rows], axis=1)
    o_n = nsa(qn, nsa_full, win_ext, gates, bias_table, cmp_pos, cmp_w1, cmp_w2)
    o_f = fox(qf, fox_full, logf_full)
    h = _post_attention(o_n.reshape(B * T, -1), o_f.reshape(B * T, -1), x.reshape(B * T, D),
                        g_nsa, g_fox, w_out, ln1_g, ln1_b)
    f = _peer(h, peer_w_q, peer_keys, peer_uv)
    y = _add_ln(h, f, ln2_g, ln2_b).reshape(B, T, D)
    return y, nsa_rows, win_ext[:, win_ext.shape[1] - buf_len:], kvf, logf


def prompt_forward(x, w_in, b_forget, cmp_pos, cmp_w1, cmp_w2, bias_table, g_nsa, g_fox, w_out,
                   ln1_g, ln1_b, peer_w_q, peer_keys, peer_uv, ln2_g, ln2_b):
    B, T, D = x.shape
    G = NSA_KV_GROUPS
    x2 = x.reshape(B * T, D)
    w_big, w_small, b_small = _permute_w_in(w_in, b_forget)
    yb = _matmul(x2, w_big)
    ys = _proj_small(x2, w_small, b_small)
    yb3 = yb.reshape(B, T, BIG_WIDTH)
    ys3 = ys.reshape(B, T, SMALL_WIDTH)
    logf = ys3[:, :, 2 * LANE:2 * LANE + FOX_HEADS]
    kvn = yb3[:, :, CB_KVN * LANE:CB_QF * LANE].reshape(B, T, 3, 2, G, HEAD_DIM)
    kvf = yb3[:, :, CB_KF * LANE:].reshape(B, T, 2, FOX_HEADS, HEAD_DIM)

    kcvc = _compress_prompt(yb3, cmp_pos, cmp_w1, cmp_w2)
    o_n = _nsa_prompt(yb3, ys3, kcvc, bias_table)
    o_f = _fox_prompt(yb3, logf)
    h = _post_attention(o_n.reshape(B * T, NSA_Q), o_f.reshape(B * T, FOX_W), x2,
                        g_nsa, g_fox, w_out, ln1_g, ln1_b)
    f = _peer(h, peer_w_q, peer_keys, peer_uv)
    y = _add_ln(h, f, ln2_g, ln2_b).reshape(B, T, D)
    buf_len = min(WINDOW, T)
    return y, kvn[:, :, :2], kvn[:, T - buf_len:, 2], kvf, logf


def kernel(x_prompt, x_sample, cache_nsa_kv, cache_nsa_win, cache_fox_kv, cache_fox_logf, page_table,
           w_in, b_forget, nsa_cmp_pos, nsa_cmp_w1, nsa_cmp_w2, rel_bias_table, g_nsa, g_fox, w_out,
           ln1_g, ln1_b, peer_w_q, peer_sub_keys, peer_u, peer_v, ln2_g, ln2_b):
    layer = 0
    w = (w_in[layer], b_forget[layer], nsa_cmp_pos[layer], nsa_cmp_w1[layer], nsa_cmp_w2[layer],
         rel_bias_table, g_nsa[layer], g_fox[layer], w_out[layer], ln1_g[layer], ln1_b[layer],
         peer_w_q[layer], peer_sub_keys[layer], jnp.concatenate([peer_u[layer], peer_v[layer]], axis=1),
         ln2_g[layer], ln2_b[layer])
    yp, a_nsa, a_win, a_fox, a_logf = prompt_forward(x_prompt, *w)
    past = (gather_pages(cache_nsa_kv[layer], page_table), cache_nsa_win[layer],
            gather_pages(cache_fox_kv[layer], page_table), gather_pages(cache_fox_logf[layer], page_table))
    ys, b_nsa, b_win, b_fox, b_logf = layer_forward(x_sample, past, *w)
    return (yp, ys, a_nsa[None], a_win[None], a_fox[None], a_logf[None],
            b_nsa[None], b_win[None], b_fox[None], b_logf[None])
```

```python
import functools
import math

import jax
import jax.numpy as jnp
import numpy as np
from jax import lax
from jax.experimental import pallas as pl
from jax.experimental.pallas import tpu as pltpu

D_MODEL = 2048
HEAD_DIM = 128
NSA_HEADS = 8
NSA_KV_GROUPS = 2
NSA_HPG = NSA_HEADS // NSA_KV_GROUPS
CMP_BLOCK = 32
CMP_STRIDE = 16
SLC_BLOCK = 64
N_SELECT = 16
WINDOW = 512
FOX_HEADS = 8
NSA_Q = NSA_HEADS * HEAD_DIM
NSA_KV = NSA_KV_GROUPS * HEAD_DIM
FOX_W = FOX_HEADS * HEAD_DIM
IN_SIZES = (NSA_Q, 6 * NSA_KV, 3 * NSA_HEADS, FOX_W, 2 * FOX_W, FOX_HEADS)
N_BUCKETS = 32
MAX_DISTANCE = 128
PEER_HEADS = 8
N_KEYS = 128
PEER_TOPK = 16
PEER_QDIM = 256
Q_BLOCK = 128
SLC_Q_BLOCK = 32
TOKEN_CHUNK = 128
DEPTH = 1
ALPHA = (2.0 * DEPTH) ** 0.25
LN_EPS = 1e-5
NEG = -1e30
FORCE_BONUS = 1e4
SCALE = HEAD_DIM ** -0.5

LANE = 128
VMEM_LIMIT = 48 * 1024 * 1024

BIG_WIDTH = NSA_Q + 6 * NSA_KV + FOX_W + 2 * FOX_W
CB_QN = 0
CB_KVN = NSA_Q // LANE
CB_QF = CB_KVN + 6 * NSA_KV // LANE
CB_KF = CB_QF + FOX_W // LANE
CB_VF = CB_KF + FOX_W // LANE
SMALL_WIDTH = 3 * LANE


def _dot_nt(a, b):
    return lax.dot_general(a, b, (((1,), (1,)), ((), ())), preferred_element_type=jnp.float32)


def _dot(a, b):
    return jnp.dot(a, b, preferred_element_type=jnp.float32)


def _mm_kernel(x_ref, w_ref, o_ref):
    o_ref[...] = _dot(x_ref[...].astype(jnp.bfloat16), w_ref[...])


def _matmul(x, w, tm=512, tn=512):
    m, k = x.shape
    n = w.shape[1]
    tm = min(tm, m)
    n_pad = -(-n // tn) * tn
    wb = w.astype(jnp.bfloat16)
    if n_pad != n:
        wb = jnp.pad(wb, ((0, 0), (0, n_pad - n)))
    out = pl.pallas_call(
        _mm_kernel,
        grid=(m // tm, n_pad // tn),
        in_specs=[pl.BlockSpec((tm, k), lambda i, j: (i, 0)),
                  pl.BlockSpec((k, tn), lambda i, j: (0, j))],
        out_specs=pl.BlockSpec((tm, tn), lambda i, j: (i, j)),
        out_shape=jax.ShapeDtypeStruct((m, n_pad), jnp.float32),
        compiler_params=pltpu.CompilerParams(
            dimension_semantics=("parallel", "arbitrary"),
            vmem_limit_bytes=VMEM_LIMIT),
        name="dense_matmul",
    )(x, wb)
    return out[:, :n] if n_pad != n else out


def _proj_small_kernel(x_ref, w_ref, b_ref, o_ref):
    y = _dot(x_ref[...].astype(jnp.bfloat16), w_ref[...]) + b_ref[...]
    gates = y[:, :2 * LANE]
    o_ref[:, :2 * LANE] = 1.0 / (1.0 + jnp.exp(-gates))
    f = y[:, 2 * LANE:]
    o_ref[:, 2 * LANE:] = -(jnp.maximum(-f, 0.0) + jnp.log1p(jnp.exp(-jnp.abs(f))))


def _proj_small(x, w_small, b_small, tm=512):
    m, k = x.shape
    tm = min(tm, m)
    return pl.pallas_call(
        _proj_small_kernel,
        grid=(m // tm,),
        in_specs=[pl.BlockSpec((tm, k), lambda i: (i, 0)),
                  pl.BlockSpec((k, SMALL_WIDTH), lambda i: (0, 0)),
                  pl.BlockSpec((1, SMALL_WIDTH), lambda i: (0, 0))],
        out_specs=pl.BlockSpec((tm, SMALL_WIDTH), lambda i: (i, 0)),
        out_shape=jax.ShapeDtypeStruct((m, SMALL_WIDTH), jnp.float32),
        compiler_params=pltpu.CompilerParams(
            dimension_semantics=("parallel",), vmem_limit_bytes=VMEM_LIMIT),
        name="proj_small",
    )(x, w_small, b_small)


def _permute_w_in(w_in, b_forget):
    offs = [0] + [int(o) for o in np.cumsum(IN_SIZES)]
    q_n, kv_n, gate, q_f, kv_f, f_f = (w_in[:, offs[i]:offs[i + 1]] for i in range(6))
    w_big = jnp.concatenate([q_n, kv_n, q_f, kv_f], axis=1).astype(jnp.bfloat16)
    d = w_in.shape[0]
    n_gate = 3 * NSA_HPG
    zg = jnp.zeros((d, LANE - n_gate), w_in.dtype)
    zf = jnp.zeros((d, LANE - FOX_HEADS), w_in.dtype)
    w_small = jnp.concatenate([gate[:, :n_gate], zg, gate[:, n_gate:], zg, f_f, zf], axis=1).astype(jnp.bfloat16)
    b_small = jnp.concatenate([jnp.zeros((2 * LANE,), jnp.float32), b_forget.astype(jnp.float32),
                               jnp.zeros((LANE - FOX_HEADS,), jnp.float32)])[None]
    return w_big, w_small, b_small


def _gelu_tanh(h):
    return 0.5 * h * (1.0 + jnp.tanh(math.sqrt(2.0 / math.pi) * (h + 0.044715 * (h * h * h))))


def _compress_kernel(k_ref, pe_ref, w1_ref, w2_ref, o_ref, *, nh):
    def half(s0):
        acc = jnp.zeros((nh, HEAD_DIM), jnp.float32)
        for s in range(CMP_STRIDE):
            rows = k_ref[0, pl.ds(s, nh, stride=CMP_STRIDE), :] + pe_ref[0, s0 + s:s0 + s + 1, :]
            acc = acc + _dot(rows.astype(jnp.bfloat16), w1_ref[0, s0 + s])
        return acc

    first = half(0)
    second = half(CMP_STRIDE)
    h = first + pltpu.roll(second, nh - 1, 0)
    o_ref[0, 0] = _dot(_gelu_tanh(h).astype(jnp.bfloat16), w2_ref[0])


def _compress_prompt(yb3, cmp_pos, cmp_w1, cmp_w2):
    b, t, _ = yb3.shape
    nh = t // CMP_STRIDE
    n_kg = 2 * NSA_KV_GROUPS
    return pl.pallas_call(
        functools.partial(_compress_kernel, nh=nh),
        grid=(b, n_kg),
        in_specs=[pl.BlockSpec((1, t, LANE), lambda i, c: (i, 0, CB_KVN + c)),
                  pl.BlockSpec((1, CMP_BLOCK, HEAD_DIM), lambda i, c: (c // NSA_KV_GROUPS, 0, 0)),
                  pl.BlockSpec((1, CMP_BLOCK, HEAD_DIM, HEAD_DIM), lambda i, c: (c // NSA_KV_GROUPS, 0, 0, 0)),
                  pl.BlockSpec((1, HEAD_DIM, HEAD_DIM), lambda i, c: (c // NSA_KV_GROUPS, 0, 0))],
        out_specs=pl.BlockSpec((1, 1, nh, HEAD_DIM), lambda i, c: (i, c, 0, 0)),
        out_shape=jax.ShapeDtypeStruct((b, n_kg, nh, HEAD_DIM), jnp.float32),
        compiler_params=pltpu.CompilerParams(
            dimension_semantics=("parallel", "arbitrary"), vmem_limit_bytes=VMEM_LIMIT),
        name="nsa_compress",
    )(yb3, cmp_pos, cmp_w1.astype(jnp.bfloat16), cmp_w2.astype(jnp.bfloat16))


def _nsa_kernel(q_ref, kc_ref, vc_ref, ks_ref, vs_ref, kw_ref, vw_ref, gate_ref, biasc_ref, biast_ref,
                c2s_ref, expand_ref, o_ref, q_s, sel_s, m_s, l_s, acc_s, out_s, *, n_slc, n_top):
    i = pl.program_id(2)
    rows_q = NSA_HPG * LANE
    bf16 = jnp.bfloat16

    qt = q_ref[0]
    q_s[...] = jnp.concatenate([qt[:, h * LANE:(h + 1) * LANE] for h in range(NSA_HPG)], axis=0).astype(bf16)

    r = lax.broadcasted_iota(jnp.int32, (rows_q, LANE), 0) & (LANE - 1)
    c = lax.broadcasted_iota(jnp.int32, (rows_q, LANE), 1)
    q_pos = i * LANE + r
    s = _dot_nt(q_s[...], kc_ref[0, 0].astype(bf16)) * SCALE + biasc_ref[...].reshape(rows_q, LANE)
    mask = c * CMP_STRIDE + (CMP_BLOCK - 1) <= q_pos
    s = jnp.where(mask, s, NEG)
    p = jnp.where(mask, jnp.exp(s - jnp.max(s, -1, keepdims=True)), 0.0)
    p = p / jnp.maximum(jnp.sum(p, -1, keepdims=True), 1e-30)
    out_s[0] = _dot(p.astype(bf16), vc_ref[0, 0].astype(bf16))

    p_sum = p[0:LANE]
    for h in range(1, NSA_HPG):
        p_sum = p_sum + p[h * LANE:(h + 1) * LANE]
    p_hi = p_sum.astype(bf16)
    p_lo = (p_sum - p_hi.astype(jnp.float32)).astype(bf16)
    imp = _dot(p_hi, c2s_ref[...]) + _dot(p_lo, c2s_ref[...])
    jj = lax.broadcasted_iota(jnp.int32, (LANE, LANE), 1)
    qp = i * LANE + lax.broadcasted_iota(jnp.int32, (LANE, LANE), 0)
    cur = qp >> int(math.log2(SLC_BLOCK))
    forced = (jj == 0) | (jj == cur) | (jj == cur - 1)
    val = jnp.where(jj * SLC_BLOCK <= qp, imp + jnp.where(forced, FORCE_BONUS, 0.0), NEG)
    val = jnp.where(jj < n_slc, val, -3e38)
    rank = jnp.zeros((LANE, LANE), jnp.int32)
    for t in range(n_slc):
        col = val[:, t:t + 1]
        ahead = (col > val) | ((col == val) & (jj > t))
        rank = rank + ahead.astype(jnp.int32)
    sel_s[...] = jnp.where(rank < n_top, 1.0, 0.0).astype(bf16)

    def attend(k_ref, v_ref, lo, use_sel, use_win, slot):
        m_s[...] = jnp.full((rows_q, 1), NEG, jnp.float32)
        l_s[...] = jnp.zeros((rows_q, 1), jnp.float32)
        acc_s[...] = jnp.zeros((rows_q, HEAD_DIM), jnp.float32)

        def body(kt, carry):
            off = pl.multiple_of(kt * LANE, LANE)
            k = k_ref[0, pl.ds(off, LANE), :].astype(bf16)
            v = v_ref[0, pl.ds(off, LANE), :].astype(bf16)
            dq = i - kt
            sc = _dot_nt(q_s[...], k) * SCALE + biast_ref[0, jnp.minimum(dq, 2)]
            rr = lax.broadcasted_iota(jnp.int32, (rows_q, LANE), 0) & (LANE - 1)
            cc = lax.broadcasted_iota(jnp.int32, (rows_q, LANE), 1)
            dist = rr - cc + dq * LANE
            msk = dist >= 0
            if use_win:
                msk = msk & (dist <= WINDOW)
            if use_sel:
                se = _dot(sel_s[...], expand_ref[kt])
                msk = msk & (jnp.concatenate([se] * NSA_HPG, axis=0) > 0.5)
            sc = jnp.where(msk, sc, NEG)
            m_old = m_s[...]
            m_new = jnp.maximum(m_old, jnp.max(sc, -1, keepdims=True))
            pe = jnp.where(msk, jnp.exp(sc - m_new), 0.0)
            alpha = jnp.exp(m_old - m_new)
            l_s[...] = alpha * l_s[...] + jnp.sum(pe, -1, keepdims=True)
            acc_s[...] = alpha * acc_s[...] + _dot(pe.astype(bf16), v)
            m_s[...] = m_new
            return carry

        lax.fori_loop(lo, i + 1, body, 0)
        out_s[slot] = acc_s[...] / jnp.maximum(l_s[...], 1e-30)

    attend(ks_ref, vs_ref, 0, True, False, 1)
    attend(kw_ref, vw_ref, jnp.maximum(i - WINDOW // LANE, 0), False, True, 2)

    gt = gate_ref[0]
    for h in range(NSA_HPG):
        sl = slice(h * LANE, (h + 1) * LANE)
        o_ref[0, :, sl] = (gt[:, 3 * h:3 * h + 1] * out_s[0, sl, :]
                           + gt[:, 3 * h + 1:3 * h + 2] * out_s[1, sl, :]
                           + gt[:, 3 * h + 2:3 * h + 3] * out_s[2, sl, :])


def _t5_bucket_np(d):
    max_exact = N_BUCKETS // 2
    d = np.maximum(d, 0)
    large = max_exact + (np.log(np.maximum(d, 1).astype(np.float32) / np.float32(max_exact))
                         / np.float32(math.log(MAX_DISTANCE / max_exact)) * (N_BUCKETS - max_exact)).astype(np.int32)
    return np.where(d < max_exact, d, np.minimum(large, N_BUCKETS - 1)).astype(np.int32)


def _nsa_prompt(yb3, ys3, kcvc, bias_table):
    b, t, _ = yb3.shape
    n_t = t // LANE
    n_slc = t // SLC_BLOCK
    n_cmp = (t - CMP_BLOCK) // CMP_STRIDE + 1
    n_top = min(N_SELECT, n_slc)
    rows_q = NSA_HPG * LANE
    assert t % LANE == 0 and n_cmp <= LANE and n_slc <= LANE

    table = bias_table.astype(jnp.float32)
    dist_c = np.arange(t)[:, None] - (np.arange(LANE) * CMP_STRIDE + CMP_BLOCK - 1)[None, :]
    bias_c = jnp.moveaxis(table[_t5_bucket_np(dist_c)], -1, 0)
    rc = np.arange(LANE)[:, None] - np.arange(LANE)[None, :]
    buckets_t = np.stack([_t5_bucket_np(rc), _t5_bucket_np(rc + LANE), _t5_bucket_np(rc + 2 * LANE)])
    assert (_t5_bucket_np(np.arange(LANE + 1, 4 * LANE)) == N_BUCKETS - 1).all()
    bias_t = jnp.moveaxis(table[buckets_t], -1, 0)
    bias_t = bias_t.reshape(NSA_KV_GROUPS, NSA_HPG, 3, LANE, LANE).transpose(0, 2, 1, 3, 4)
    bias_t = bias_t.reshape(NSA_KV_GROUPS, 3, rows_q, LANE)

    c0 = np.arange(n_cmp) * CMP_STRIDE
    s0 = np.arange(n_slc) * SLC_BLOCK
    ov = np.minimum(c0[:, None] + CMP_BLOCK, s0[None, :] + SLC_BLOCK) - np.maximum(c0[:, None], s0[None, :])
    c2s = np.zeros((LANE, LANE), np.float32)
    c2s[:n_cmp, :n_slc] = np.maximum(ov, 0) / CMP_STRIDE
    expand = np.zeros((n_t, LANE, LANE), np.float32)
    for kt in range(n_t):
        tok_blk = (kt * LANE + np.arange(LANE)) // SLC_BLOCK
        expand[kt, tok_blk, np.arange(LANE)] = 1.0

    kv_spec = lambda cb: pl.BlockSpec((1, t, LANE), lambda bi, g, i: (bi, 0, cb + g))
    g_n = NSA_KV_GROUPS
    return pl.pallas_call(
        functools.partial(_nsa_kernel, n_slc=n_slc, n_top=n_top),
        grid=(b, g_n, n_t),
        in_specs=[
            pl.BlockSpec((1, LANE, rows_q), lambda bi, g, i: (bi, i, g)),
            pl.BlockSpec((1, 1, t // CMP_STRIDE, HEAD_DIM), lambda bi, g, i: (bi, g, 0, 0)),
            pl.BlockSpec((1, 1, t // CMP_STRIDE, HEAD_DIM), lambda bi, g, i: (bi, g_n + g, 0, 0)),
            kv_spec(CB_KVN + 2 * g_n), kv_spec(CB_KVN + 3 * g_n),
            kv_spec(CB_KVN + 4 * g_n), kv_spec(CB_KVN + 5 * g_n),
            pl.BlockSpec((1, LANE, LANE), lambda bi, g, i: (bi, i, g)),
            pl.BlockSpec((NSA_HPG, LANE, LANE), lambda bi, g, i: (g, i, 0)),
            pl.BlockSpec((1, 3, rows_q, LANE), lambda bi, g, i: (g, 0, 0, 0)),
            pl.BlockSpec((LANE, LANE), lambda bi, g, i: (0, 0)),
            pl.BlockSpec((n_t, LANE, LANE), lambda bi, g, i: (0, 0, 0)),
        ],
        out_specs=pl.BlockSpec((1, LANE, rows_q), lambda bi, g, i: (bi, i, g)),
        out_shape=jax.ShapeDtypeStruct((b, t, NSA_Q), jnp.float32),
        scratch_shapes=[pltpu.VMEM((rows_q, HEAD_DIM), jnp.bfloat16),
                        pltpu.VMEM((LANE, LANE), jnp.bfloat16),
                        pltpu.VMEM((rows_q, 1), jnp.float32),
                        pltpu.VMEM((rows_q, 1), jnp.float32),
                        pltpu.VMEM((rows_q, HEAD_DIM), jnp.float32),
                        pltpu.VMEM((3, rows_q, HEAD_DIM), jnp.float32)],
        compiler_params=pltpu.CompilerParams(
            dimension_semantics=("parallel", "parallel", "arbitrary"), vmem_limit_bytes=VMEM_LIMIT),
        name="nsa_prompt",
    )(yb3, kcvc, kcvc, yb3, yb3, yb3, yb3, ys3, bias_c, bias_t,
      jnp.asarray(c2s, jnp.bfloat16), jnp.asarray(expand, jnp.bfloat16))


FOX_TILE = 256


def _fox_kernel(q_ref, k_ref, v_ref, cq_ref, ck_ref, o_ref, m_s, l_s, acc_s):
    i = pl.program_id(2)
    tq = FOX_TILE
    bf16 = jnp.bfloat16
    q = q_ref[0].astype(bf16)
    cq = cq_ref[0, 0]
    m_s[...] = jnp.full((tq, 1), NEG, jnp.float32)
    l_s[...] = jnp.zeros((tq, 1), jnp.float32)
    acc_s[...] = jnp.zeros((tq, HEAD_DIM), jnp.float32)

    def body(kt, carry):
        off = pl.multiple_of(kt * tq, tq)
        k = k_ref[0, pl.ds(off, tq), :].astype(bf16)
        v = v_ref[0, pl.ds(off, tq), :].astype(bf16)
        sc = _dot_nt(q, k) * SCALE + cq - ck_ref[0, 0, kt]
        rr = lax.broadcasted_iota(jnp.int32, (tq, tq), 0)
        cc = lax.broadcasted_iota(jnp.int32, (tq, tq), 1)
        msk = cc + (kt - i) * tq <= rr
        sc = jnp.where(msk, sc, NEG)
        m_old = m_s[...]
        m_new = jnp.maximum(m_old, jnp.max(sc, -1, keepdims=True))
        pe = jnp.where(msk, jnp.exp(sc - m_new), 0.0)
        alpha = jnp.exp(m_old - m_new)
        l_s[...] = alpha * l_s[...] + jnp.sum(pe, -1, keepdims=True)
        acc_s[...] = alpha * acc_s[...] + _dot(pe.astype(bf16), v)
        m_s[...] = m_new
        return carry

    lax.fori_loop(0, i + 1, body, 0)
    o_ref[0] = acc_s[...] / jnp.maximum(l_s[...], 1e-30)


def _fox_prompt(yb3, logf):
    b, t, _ = yb3.shape
    tq = FOX_TILE
    n_t = t // tq
    cum = jnp.moveaxis(jnp.cumsum(logf.astype(jnp.float32), axis=1), 1, 2)
    cum_q = cum[..., None]
    cum_k = cum.reshape(b, FOX_HEADS, n_t, 1, tq)
    return pl.pallas_call(
        _fox_kernel,
        grid=(b, FOX_HEADS, n_t),
        in_specs=[pl.BlockSpec((1, tq, LANE), lambda bi, h, i: (bi, i, CB_QF + h)),
                  pl.BlockSpec((1, t, LANE), lambda bi, h, i: (bi, 0, CB_KF + h)),
                  pl.BlockSpec((1, t, LANE), lambda bi, h, i: (bi, 0, CB_VF + h)),
                  pl.BlockSpec((1, 1, tq, 1), lambda bi, h, i: (bi, h, i, 0)),
                  pl.BlockSpec((1, 1, n_t, 1, tq), lambda bi, h, i: (bi, h, 0, 0, 0))],
        out_specs=pl.BlockSpec((1, tq, LANE), lambda bi, h, i: (bi, i, h)),
        out_shape=jax.ShapeDtypeStruct((b, t, FOX_W), jnp.float32),
        scratch_shapes=[pltpu.VMEM((tq, 1), jnp.float32),
                        pltpu.VMEM((tq, 1), jnp.float32),
                        pltpu.VMEM((tq, HEAD_DIM), jnp.float32)],
        compiler_params=pltpu.CompilerParams(
            dimension_semantics=("parallel", "parallel", "arbitrary"), vmem_limit_bytes=VMEM_LIMIT),
        name="fox_prompt",
    )(yb3, yb3, yb3, cum_q, cum_k)


def _ln(z, g, b):
    mu = jnp.mean(z, -1, keepdims=True)
    zc = z - mu
    var = jnp.mean(zc * zc, -1, keepdims=True)
    return zc * lax.rsqrt(var + LN_EPS) * g + b


def _post_kernel(on_ref, of_ref, x_ref, gn_ref, gf_ref, w_ref, lg_ref, lb_ref, h_ref):
    def rms(o, g):
        return (o * lax.rsqrt(jnp.mean(o * o, -1, keepdims=True) + LN_EPS) * g).astype(jnp.bfloat16)

    mix = (_dot(rms(on_ref[...], gn_ref[...]), w_ref[:NSA_Q, :])
           + _dot(rms(of_ref[...], gf_ref[...]), w_ref[NSA_Q:, :]))
    h_ref[...] = _ln(ALPHA * x_ref[...] + mix, lg_ref[...], lb_ref[...])


def _post_attention(o_n, o_f, x, g_nsa, g_fox, w_out, ln_g, ln_b, tm=256):
    m, d = x.shape
    tm = min(tm, m)
    row = lambda n: pl.BlockSpec((1, n), lambda i: (0, 0))
    return pl.pallas_call(
        _post_kernel,
        grid=(m // tm,),
        in_specs=[pl.BlockSpec((tm, NSA_Q), lambda i: (i, 0)),
                  pl.BlockSpec((tm, FOX_W), lambda i: (i, 0)),
                  pl.BlockSpec((tm, d), lambda i: (i, 0)),
                  row(NSA_Q), row(FOX_W),
                  pl.BlockSpec((NSA_Q + FOX_W, d), lambda i: (0, 0)),
                  row(d), row(d)],
        out_specs=pl.BlockSpec((tm, d), lambda i: (i, 0)),
        out_shape=jax.ShapeDtypeStruct((m, d), jnp.float32),
        compiler_params=pltpu.CompilerParams(
            dimension_semantics=("parallel",), vmem_limit_bytes=VMEM_LIMIT),
        name="post_attention",
    )(o_n, o_f, x, g_nsa[None], g_fox[None], w_out.astype(jnp.bfloat16), ln_g[None], ln_b[None])


def _add_ln_kernel(h_ref, f_ref, g_ref, b_ref, o_ref):
    o_ref[...] = _ln(ALPHA * h_ref[...] + f_ref[...], g_ref[...], b_ref[...])


def _add_ln(h, f, ln_g, ln_b, tm=512):
    m, d = h.shape
    tm = min(tm, m)
    return pl.pallas_call(
        _add_ln_kernel,
        grid=(m // tm,),
        in_specs=[pl.BlockSpec((tm, d), lambda i: (i, 0)),
                  pl.BlockSpec((tm, d), lambda i: (i, 0)),
                  pl.BlockSpec((1, d), lambda i: (0, 0)),
                  pl.BlockSpec((1, d), lambda i: (0, 0))],
        out_specs=pl.BlockSpec((tm, d), lambda i: (i, 0)),
        out_shape=jax.ShapeDtypeStruct((m, d), jnp.float32),
        compiler_params=pltpu.CompilerParams(
            dimension_semantics=("parallel",), vmem_limit_bytes=VMEM_LIMIT),
        name="add_layer_norm",
    )(h, f, ln_g[None], ln_b[None])


PEER_TILE = 128
N_ROUTES = PEER_HEADS * PEER_TOPK


def _top_rows(vals, row_id, n_out, payload=None):
    big = float(vals.shape[0])
    out_v, out_i = [], []
    for _ in range(n_out):
        m = jnp.max(vals, axis=0, keepdims=True)
        win = jnp.min(jnp.where(vals == m, row_id, big), axis=0, keepdims=True)
        hit = row_id == win
        out_v.append(m)
        if payload is None:
            out_i.append(win)
        else:
            out_i.append(jnp.sum(jnp.where(hit, payload, 0.0), axis=0, keepdims=True))
        vals = jnp.where(hit, -jnp.inf, vals)
    return jnp.concatenate(out_v, axis=0), jnp.concatenate(out_i, axis=0)


def _peer_route_kernel(h_ref, wq_ref, keys_ref, g_ref, e_ref, sv_s, si_s):
    bf16 = jnp.bfloat16
    tm = PEER_TILE
    half = PEER_QDIM // 2
    q = _dot(h_ref[...].astype(bf16), wq_ref[...]).astype(bf16)
    key_id = lax.broadcasted_iota(jnp.int32, (N_KEYS, tm), 0).astype(jnp.float32)
    for hp in range(2 * PEER_HEADS):
        s_t = _dot_nt(keys_ref[hp], q[:, hp * half:(hp + 1) * half])
        sv, si = _top_rows(s_t, key_id, PEER_TOPK)
        sv_s[hp] = sv
        si_s[hp] = si
    pair_id = lax.broadcasted_iota(jnp.int32, (PEER_TOPK * PEER_TOPK, tm), 0).astype(jnp.float32)
    for h in range(PEER_HEADS):
        sv0, sv1 = sv_s[2 * h], sv_s[2 * h + 1]
        si0, si1 = si_s[2 * h], si_s[2 * h + 1]
        cand = jnp.concatenate([sv0[a:a + 1, :] + sv1 for a in range(PEER_TOPK)], axis=0)
        expert = jnp.concatenate([si0[a:a + 1, :] * float(N_KEYS) + si1 for a in range(PEER_TOPK)], axis=0)
        best, eid = _top_rows(cand, pair_id, PEER_TOPK, payload=expert)
        ex = jnp.exp(best - best[0:1, :])
        g_ref[0, h * PEER_TOPK:(h + 1) * PEER_TOPK, :] = ex / jnp.sum(ex, axis=0, keepdims=True)
        e_ref[0, h * PEER_TOPK:(h + 1) * PEER_TOPK, :] = eid.astype(jnp.int32)


def _peer_route(h, w_q, sub_keys):
    n, d = h.shape
    tm = PEER_TILE
    nb = n // tm
    n_hp = 2 * PEER_HEADS
    half = PEER_QDIM // 2
    out = jax.ShapeDtypeStruct((nb, N_ROUTES, tm), jnp.float32)
    return pl.pallas_call(
        _peer_route_kernel,
        grid=(nb,),
        in_specs=[pl.BlockSpec((tm, d), lambda i: (i, 0)),
                  pl.BlockSpec((d, PEER_HEADS * PEER_QDIM), lambda i: (0, 0)),
                  pl.BlockSpec((n_hp, N_KEYS, half), lambda i: (0, 0, 0))],
        out_specs=[pl.BlockSpec((1, N_ROUTES, tm), lambda i: (i, 0, 0)),
                   pl.BlockSpec((1, N_ROUTES, tm), lambda i: (i, 0, 0))],
        out_shape=[out, jax.ShapeDtypeStruct((nb, N_ROUTES, tm), jnp.int32)],
        scratch_shapes=[pltpu.VMEM((n_hp, PEER_TOPK, tm), jnp.float32),
                        pltpu.VMEM((n_hp, PEER_TOPK, tm), jnp.float32)],
        compiler_params=pltpu.CompilerParams(
            dimension_semantics=("parallel",), vmem_limit_bytes=VMEM_LIMIT),
        name="peer_route",
    )(h, w_q.astype(jnp.bfloat16), sub_keys.reshape(n_hp, N_KEYS, half).astype(jnp.bfloat16))


def _peer_expert_kernel(e_ref, g_ref, x_ref, uv_hbm, o_ref, e_smem, buf, sem, esem):
    tm = PEER_TILE
    d = D_MODEL
    ids = pltpu.make_async_copy(e_ref.at[0], e_smem, esem)
    ids.start()
    ids.wait()

    def issue(t, slot):
        for k in range(N_ROUTES):
            pltpu.make_async_copy(uv_hbm.at[pl.ds(e_smem[k, t], 1), :],
                                  buf.at[slot, pl.ds(k, 1), :], sem.at[slot]).start()

    def wait_rows(slot):
        pltpu.make_async_copy(uv_hbm.at[pl.ds(0, N_ROUTES), :], buf.at[slot], sem.at[slot]).wait()

    issue(0, 0)

    def body(t, carry):
        slot = t & 1

        @pl.when(t + 1 < tm)
        def _():
            issue(t + 1, 1 - slot)

        wait_rows(slot)
        x_row = x_ref[pl.ds(t, 1), :]
        acc = buf[slot, :, 0:LANE] * x_row[:, 0:LANE]
        for c in range(1, d // LANE):
            acc = acc + buf[slot, :, c * LANE:(c + 1) * LANE] * x_row[:, c * LANE:(c + 1) * LANE]
        s = jnp.sum(acc, axis=1, keepdims=True)
        gate = pltpu.roll(g_ref[0], jnp.where(t == 0, 0, tm - t), 1)[:, 0:1]
        coef = gate * _gelu_tanh(s)
        pieces = [jnp.sum(buf[slot, :, d + c * LANE:d + (c + 1) * LANE] * coef, axis=0, keepdims=True)
                  for c in range(d // LANE)]
        o_ref[pl.ds(t, 1), :] = jnp.concatenate(pieces, axis=1)
        return carry

    lax.fori_loop(0, tm, body, 0)


def _peer_experts(h, gates, experts, uv):
    n, d = h.shape
    tm = PEER_TILE
    nb = n // tm
    return pl.pallas_call(
        _peer_expert_kernel,
        grid=(nb,),
        in_specs=[pl.BlockSpec((1, N_ROUTES, tm), lambda i: (i, 0, 0)),
                  pl.BlockSpec((1, N_ROUTES, tm), lambda i: (i, 0, 0)),
                  pl.BlockSpec((tm, d), lambda i: (i, 0)),
                  pl.BlockSpec(memory_space=pl.ANY)],
        out_specs=pl.BlockSpec((tm, d), lambda i: (i, 0)),
        out_shape=jax.ShapeDtypeStruct((n, d), jnp.float32),
        scratch_shapes=[pltpu.SMEM((N_ROUTES, tm), jnp.int32),
                        pltpu.VMEM((2, N_ROUTES, 2 * d), jnp.float32),
                        pltpu.SemaphoreType.DMA((2,)),
                        pltpu.SemaphoreType.DMA],
        compiler_params=pltpu.CompilerParams(
            dimension_semantics=("arbitrary",), vmem_limit_bytes=VMEM_LIMIT),
        name="peer_experts",
    )(experts, gates, h, uv)


def _peer(h, w_q, sub_keys, uv):
    gates, experts = _peer_route(h, w_q, sub_keys)
    return _peer_experts(h, gates, experts, uv)


QSLOT = 8
DEC_GROUP = 4


def _nsa_decode_kernel(pt_ref, q_ref, ks_new_ref, vs_new_ref, kw_new_ref, vw_new_ref, kwin_ref, vwin_ref,
                       gate_ref, biasc_ref, biast_ref, c2s_ref, expand_ref, pe_ref, w1_ref, w2_ref, pool_hbm,
                       o_ref, kbuf, wbuf, kc_s, q_s, sel_s, m_s, l_s, acc_s, out_s, sem,
                       *, n_pages, page, tq, n_slc, n_top):
    b = pl.program_id(0)
    g = pl.program_id(1)
    bf16 = jnp.bfloat16
    rows_q = NSA_HPG * QSLOT
    p0 = n_pages * page
    n_half = p0 // CMP_STRIDE
    i_slc = p0 // LANE
    i_win = WINDOW // LANE

    def page_copy(p, c, gg):
        col = (2 * c + gg) * LANE
        return pltpu.make_async_copy(pool_hbm.at[pt_ref[b, p], :, pl.ds(col, LANE)],
                                     kbuf.at[c, pl.ds(p * page, page), :], sem.at[c])

    def for_group(fn):
        for gg in range(NSA_KV_GROUPS):
            @pl.when(g == gg)
            def _():
                fn(gg)

    def start_pages(gg):
        for c in range(4):
            for p in range(n_pages):
                page_copy(p, c, gg).start()

    def wait_pages(cs):
        def fn(gg):
            for c in cs:
                for p in range(n_pages):
                    page_copy(p, c, gg).wait()
        return fn

    for_group(start_pages)

    q_s[...] = jnp.zeros((rows_q, HEAD_DIM), jnp.float32)
    qt = q_ref[0]
    for h in range(NSA_HPG):
        q_s[h * QSLOT:h * QSLOT + tq, :] = qt[:, h * LANE:(h + 1) * LANE]
    zeros_tile = jnp.zeros((LANE, HEAD_DIM), jnp.float32)
    wbuf[0, 0:WINDOW, :] = kwin_ref[0]
    wbuf[1, 0:WINDOW, :] = vwin_ref[0]
    wbuf[0, WINDOW:WINDOW + LANE, :] = zeros_tile
    wbuf[1, WINDOW:WINDOW + LANE, :] = zeros_tile
    wbuf[0, WINDOW:WINDOW + tq, :] = kw_new_ref[0]
    wbuf[1, WINDOW:WINDOW + tq, :] = vw_new_ref[0]

    for_group(wait_pages((0, 1)))
    for kv in range(2):
        def half(s0):
            acc = jnp.zeros((n_half, HEAD_DIM), jnp.float32)
            for s in range(CMP_STRIDE):
                rows = kbuf[kv, pl.ds(s, n_half, stride=CMP_STRIDE), :] + pe_ref[kv, s0 + s:s0 + s + 1, :]
                acc = acc + _dot(rows.astype(bf16), w1_ref[kv, s0 + s])
            return acc

        first = half(0)
        second = half(CMP_STRIDE)
        hmid = first + pltpu.roll(second, n_half - 1, 0)
        kc_s[kv] = _dot(_gelu_tanh(hmid).astype(bf16), w2_ref[kv])

    qb = q_s[...].astype(bf16)
    r = lax.broadcasted_iota(jnp.int32, (rows_q, n_half), 0) & (QSLOT - 1)
    c = lax.broadcasted_iota(jnp.int32, (rows_q, n_half), 1)
    s = _dot_nt(qb, kc_s[0].astype(bf16)) * SCALE + biasc_ref[...].reshape(rows_q, n_half)
    mask = c * CMP_STRIDE + (CMP_BLOCK - 1) <= p0 + r
    s = jnp.where(mask, s, NEG)
    p = jnp.where(mask, jnp.exp(s - jnp.max(s, -1, keepdims=True)), 0.0)
    p = p / jnp.maximum(jnp.sum(p, -1, keepdims=True), 1e-30)
    out_s[0] = _dot(p.astype(bf16), kc_s[1].astype(bf16))

    p_sum = p[0:QSLOT]
    for h in range(1, NSA_HPG):
        p_sum = p_sum + p[h * QSLOT:(h + 1) * QSLOT]
    p_hi = p_sum.astype(bf16)
    p_lo = (p_sum - p_hi.astype(jnp.float32)).astype(bf16)
    imp = _dot(p_hi, c2s_ref[...]) + _dot(p_lo, c2s_ref[...])
    wide = 2 * LANE
    jj = lax.broadcasted_iota(jnp.int32, (QSLOT, wide), 1)
    qp = p0 + lax.broadcasted_iota(jnp.int32, (QSLOT, wide), 0)
    cur = qp >> int(math.log2(SLC_BLOCK))
    forced = (jj == 0) | (jj == cur) | (jj == cur - 1)
    val = jnp.where(jj * SLC_BLOCK <= qp, imp + jnp.where(forced, FORCE_BONUS, 0.0), NEG)
    val = jnp.where(jj < n_slc, val, -3e38)
    rank = jnp.zeros((QSLOT, wide), jnp.int32)
    for t in range(n_slc):
        col = val[:, t:t + 1]
        ahead = (col > val) | ((col == val) & (jj > t))
        rank = rank + ahead.astype(jnp.int32)
    sel_s[...] = jnp.where(rank < n_top, 1.0, 0.0)[:, :LANE]

    def attend(buf, kslot, i_tile, use_sel, use_win, slot):
        m_s[...] = jnp.full((rows_q, 1), NEG, jnp.float32)
        l_s[...] = jnp.zeros((rows_q, 1), jnp.float32)
        acc_s[...] = jnp.zeros((rows_q, HEAD_DIM), jnp.float32)

        def tiles(k0, w, with_sel):
            off = pl.multiple_of(k0 * LANE, LANE)
            k = buf[kslot, pl.ds(off, w * LANE), :].astype(bf16)
            v = buf[kslot + 1, pl.ds(off, w * LANE), :].astype(bf16)
            bias = [biast_ref[0, jnp.minimum(i_tile - k0 - j, 2)] for j in range(w)]
            sc = _dot_nt(qb, k) * SCALE + (bias[0] if w == 1 else jnp.concatenate(bias, axis=1))
            rr = lax.broadcasted_iota(jnp.int32, (rows_q, w * LANE), 0) & (QSLOT - 1)
            cc = lax.broadcasted_iota(jnp.int32, (rows_q, w * LANE), 1)
            dist = rr - cc + (i_tile - k0) * LANE
            msk = dist >= 0
            if use_win:
                msk = msk & (dist <= WINDOW)
            if with_sel:
                sb = sel_s[...].astype(bf16)
                se = jnp.concatenate([_dot(sb, expand_ref[k0 + j]) for j in range(w)], axis=1)
                msk = msk & (jnp.concatenate([se] * NSA_HPG, axis=0) > 0.5)
            sc = jnp.where(msk, sc, NEG)
            m_old = m_s[...]
            m_new = jnp.maximum(m_old, jnp.max(sc, -1, keepdims=True))
            pe = jnp.where(msk, jnp.exp(sc - m_new), 0.0)
            alpha = jnp.exp(m_old - m_new)
            l_s[...] = alpha * l_s[...] + jnp.sum(pe, -1, keepdims=True)
            acc_s[...] = alpha * acc_s[...] + _dot(pe.astype(bf16), v)
            m_s[...] = m_new

        def body(grp, carry):
            tiles(grp * DEC_GROUP, DEC_GROUP, use_sel)
            return carry

        lax.fori_loop(0, i_tile // DEC_GROUP, body, 0)
        tiles(i_tile, 1, False)
        out_s[slot] = acc_s[...] / jnp.maximum(l_s[...], 1e-30)

    for_group(wait_pages((2, 3)))
    kbuf[2, p0:p0 + LANE, :] = zeros_tile
    kbuf[3, p0:p0 + LANE, :] = zeros_tile
    kbuf[2, p0:p0 + tq, :] = ks_new_ref[0]
    kbuf[3, p0:p0 + tq, :] = vs_new_ref[0]
    attend(kbuf, 2, i_slc, True, False, 1)
    attend(wbuf, 0, i_win, False, True, 2)

    gt = gate_ref[0]
    for h in range(NSA_HPG):
        sl = slice(h * QSLOT, h * QSLOT + tq)
        o_ref[0, :, h * LANE:(h + 1) * LANE] = (gt[:, 3 * h:3 * h + 1] * out_s[0, sl, :]
                                                 + gt[:, 3 * h + 1:3 * h + 2] * out_s[1, sl, :]
                                                 + gt[:, 3 * h + 2:3 * h + 3] * out_s[2, sl, :])


def _nsa_decode(yb3, ys3, pool, win_cache, page_table, bias_table, cmp_pos, cmp_w1, cmp_w2):
    b, tq, _ = yb3.shape
    n_pages = page_table.shape[1]
    page = pool.shape[1]
    p0 = n_pages * page
    tk = p0 + tq
    n_slc = -(-tk // SLC_BLOCK)
    n_cmp = (tk - CMP_BLOCK) // CMP_STRIDE + 1
    n_half = p0 // CMP_STRIDE
    n_top = min(N_SELECT, n_slc)
    rows_q = NSA_HPG * QSLOT
    g_n = NSA_KV_GROUPS
    wide = 2 * LANE
    assert p0 % LANE == 0 and tq <= QSLOT and tq <= SLC_BLOCK and p0 % SLC_BLOCK == 0
    assert n_cmp + 1 == n_half and n_slc <= wide and (n_slc - 1) * SLC_BLOCK == p0
    assert (p0 // LANE) % DEC_GROUP == 0 and (WINDOW // LANE) % DEC_GROUP == 0
    assert win_cache.shape[1] == WINDOW

    table = bias_table.astype(jnp.float32)
    dist_c = p0 + np.arange(QSLOT)[:, None] - (np.arange(n_half) * CMP_STRIDE + CMP_BLOCK - 1)[None, :]
    bias_c = jnp.moveaxis(table[_t5_bucket_np(dist_c)], -1, 0)
    rc = np.arange(QSLOT)[:, None] - np.arange(LANE)[None, :]
    buckets_t = np.stack([_t5_bucket_np(rc), _t5_bucket_np(rc + LANE), _t5_bucket_np(rc + 2 * LANE)])
    bias_t = jnp.moveaxis(table[buckets_t], -1, 0)
    bias_t = bias_t.reshape(g_n, NSA_HPG, 3, QSLOT, LANE).transpose(0, 2, 1, 3, 4).reshape(g_n, 3, rows_q, LANE)

    c0 = np.arange(n_cmp) * CMP_STRIDE
    s0 = np.arange(n_slc) * SLC_BLOCK
    ov = np.minimum(c0[:, None] + CMP_BLOCK, s0[None, :] + SLC_BLOCK) - np.maximum(c0[:, None], s0[None, :])
    c2s = np.zeros((n_half, wide), np.float32)
    c2s[:n_cmp, :n_slc] = np.maximum(ov, 0) / CMP_STRIDE
    n_t = p0 // LANE
    expand = np.zeros((n_t, LANE, LANE), np.float32)
    for kt in range(n_t):
        expand[kt, (kt * LANE + np.arange(LANE)) // SLC_BLOCK, np.arange(LANE)] = 1.0

    new_spec = lambda cb: pl.BlockSpec((1, tq, LANE), lambda bi, g, pt: (bi, 0, cb + g))
    full = lambda shape: pl.BlockSpec(shape, lambda bi, g, pt: (0,) * len(shape))
    grid_spec = pltpu.PrefetchScalarGridSpec(
        num_scalar_prefetch=1,
        grid=(b, g_n),
        in_specs=[
            pl.BlockSpec((1, tq, NSA_HPG * LANE), lambda bi, g, pt: (bi, 0, g)),
            new_spec(CB_KVN + 2 * g_n), new_spec(CB_KVN + 3 * g_n),
            new_spec(CB_KVN + 4 * g_n), new_spec(CB_KVN + 5 * g_n),
            pl.BlockSpec((1, WINDOW, LANE), lambda bi, g, pt: (bi, 0, g)),
            pl.BlockSpec((1, WINDOW, LANE), lambda bi, g, pt: (bi, 0, g_n + g)),
            pl.BlockSpec((1, tq, LANE), lambda bi, g, pt: (bi, 0, g)),
            pl.BlockSpec((NSA_HPG, QSLOT, n_half), lambda bi, g, pt: (g, 0, 0)),
            pl.BlockSpec((1, 3, rows_q, LANE), lambda bi, g, pt: (g, 0, 0, 0)),
            full((n_half, wide)), full((n_t, LANE, LANE)),
            full((2, CMP_BLOCK, HEAD_DIM)), full((2, CMP_BLOCK, HEAD_DIM, HEAD_DIM)), full((2, HEAD_DIM, HEAD_DIM)),
            pl.BlockSpec(memory_space=pl.ANY),
        ],
        out_specs=pl.BlockSpec((1, tq, NSA_HPG * LANE), lambda bi, g, pt: (bi, 0, g)),
        scratch_shapes=[pltpu.VMEM((4, p0 + LANE, HEAD_DIM), jnp.float32),
                        pltpu.VMEM((2, WINDOW + LANE, HEAD_DIM), jnp.float32),
                        pltpu.VMEM((2, n_half, HEAD_DIM), jnp.float32),
                        pltpu.VMEM((rows_q, HEAD_DIM), jnp.float32),
                        pltpu.VMEM((QSLOT, LANE), jnp.float32),
                        pltpu.VMEM((rows_q, 1), jnp.float32),
                        pltpu.VMEM((rows_q, 1), jnp.float32),
                        pltpu.VMEM((rows_q, HEAD_DIM), jnp.float32),
                        pltpu.VMEM((3, rows_q, HEAD_DIM), jnp.float32),
                        pltpu.SemaphoreType.DMA((4,))],
    )
    return pl.pallas_call(
        functools.partial(_nsa_decode_kernel, n_pages=n_pages, page=page, tq=tq, n_slc=n_slc, n_top=n_top),
        grid_spec=grid_spec,
        out_shape=jax.ShapeDtypeStruct((b, tq, NSA_Q), jnp.float32),
        compiler_params=pltpu.CompilerParams(
            dimension_semantics=("arbitrary", "arbitrary"), vmem_limit_bytes=VMEM_LIMIT),
        name="nsa_decode",
    )(page_table, yb3, yb3, yb3, yb3, yb3, win_cache, win_cache, ys3, bias_c, bias_t,
      jnp.asarray(c2s, jnp.bfloat16), jnp.asarray(expand, jnp.bfloat16),
      cmp_pos, cmp_w1.astype(jnp.bfloat16), cmp_w2.astype(jnp.bfloat16), pool)


def _fox_decode_kernel(pt_ref, qa_ref, qb_ref, ka_ref, kb_ref, va_ref, vb_ref, cq_ref, ck_ref, kv_ref,
                       o_ref, q_s, new_s, m_s, l_s, acc_s, *, n_pages, tq):
    p = pl.program_id(1)
    bf16 = jnp.bfloat16
    half_h = FOX_HEADS // 2

    rows = FOX_HEADS * QSLOT

    @pl.when(p == 0)
    def _():
        m_s[...] = jnp.full(m_s.shape, NEG, jnp.float32)
        l_s[...] = jnp.zeros(l_s.shape, jnp.float32)
        acc_s[...] = jnp.zeros(acc_s.shape, jnp.float32)
        q_s[...] = jnp.zeros(q_s.shape, jnp.float32)
        new_s[...] = jnp.zeros(new_s.shape, jnp.float32)
        for h in range(FOX_HEADS):
            src_q, src_k, src_v = (qa_ref, ka_ref, va_ref) if h < half_h else (qb_ref, kb_ref, vb_ref)
            lo = (h % half_h) * LANE
            q_s[h * QSLOT:h * QSLOT + tq, :] = src_q[0, :, lo:lo + LANE]
            new_s[0:tq, h * LANE:(h + 1) * LANE] = src_k[0, :, lo:lo + LANE]
            new_s[0:tq, FOX_W + h * LANE:FOX_W + (h + 1) * LANE] = src_v[0, :, lo:lo + LANE]

    def step(kv_at, is_new):
        qb = q_s[...].astype(bf16)
        sc = jnp.concatenate(
            [_dot_nt(qb[h * QSLOT:(h + 1) * QSLOT], kv_at(h * LANE).astype(bf16)) for h in range(FOX_HEADS)], axis=0)
        ck = jnp.concatenate([jnp.broadcast_to(ck_ref[0, h, 0], (QSLOT, LANE)) for h in range(FOX_HEADS)], axis=0)
        sc = sc * SCALE + cq_ref[0].reshape(rows, 1) - ck
        if is_new:
            rr = lax.broadcasted_iota(jnp.int32, (rows, LANE), 0) & (QSLOT - 1)
            cc = lax.broadcasted_iota(jnp.int32, (rows, LANE), 1)
            msk = cc <= rr
            sc = jnp.where(msk, sc, NEG)
        m_old = m_s[...]
        m_new = jnp.maximum(m_old, jnp.max(sc, -1, keepdims=True))
        pe = jnp.exp(sc - m_new)
        if is_new:
            pe = jnp.where(msk, pe, 0.0)
        alpha = jnp.exp(m_old - m_new)
        l_s[...] = alpha * l_s[...] + jnp.sum(pe, -1, keepdims=True)
        pb = pe.astype(bf16)
        pv = jnp.concatenate(
            [_dot(pb[h * QSLOT:(h + 1) * QSLOT], kv_at(FOX_W + h * LANE).astype(bf16)) for h in range(FOX_HEADS)],
            axis=0)
        acc_s[...] = alpha * acc_s[...] + pv
        m_s[...] = m_new

    @pl.when(p < n_pages)
    def _():
        step(lambda col: kv_ref[0, :, col:col + LANE], False)

    @pl.when(p == n_pages)
    def _():
        step(lambda col: new_s[:, col:col + LANE], True)
        res = acc_s[...] / jnp.maximum(l_s[...], 1e-30)
        for h in range(FOX_HEADS):
            o_ref[0, :, h * LANE:(h + 1) * LANE] = res[h * QSLOT:h * QSLOT + tq, :]


def _fox_decode(yb3, logf_new, pool, logf_past, page_table):
    b, tq, _ = yb3.shape
    n_pages = page_table.shape[1]
    page = pool.shape[1]
    p0 = n_pages * page
    assert page == LANE and tq <= QSLOT
    cum = jnp.cumsum(jnp.concatenate([logf_past.astype(jnp.float32), logf_new], axis=1), axis=1)
    cum_q = jnp.pad(jnp.moveaxis(cum[:, p0:], 1, 2), ((0, 0), (0, 0), (0, QSLOT - tq)))[..., None]
    cum_k = jnp.pad(jnp.moveaxis(cum, 1, 2), ((0, 0), (0, 0), (0, LANE - tq)))
    cum_k = cum_k.reshape(b, FOX_HEADS, n_pages + 1, 1, LANE)
    wq = FOX_W // 2
    blk = lambda col: pl.BlockSpec((1, tq, wq), lambda bi, p, pt: (bi, 0, col))
    base_q, base_k, base_v = CB_QF * LANE // wq, CB_KF * LANE // wq, CB_VF * LANE // wq
    grid_spec = pltpu.PrefetchScalarGridSpec(
        num_scalar_prefetch=1,
        grid=(b, n_pages + 1),
        in_specs=[blk(base_q), blk(base_q + 1), blk(base_k), blk(base_k + 1), blk(base_v), blk(base_v + 1),
                  pl.BlockSpec((1, FOX_HEADS, QSLOT, 1), lambda bi, p, pt: (bi, 0, 0, 0)),
                  pl.BlockSpec((1, FOX_HEADS, 1, 1, LANE), lambda bi, p, pt: (bi, 0, p, 0, 0)),
                  pl.BlockSpec((1, page, 2 * FOX_W),
                               lambda bi, p, pt: (pt[bi, jnp.minimum(p, n_pages - 1)], 0, 0))],
        out_specs=pl.BlockSpec((1, tq, FOX_W), lambda bi, p, pt: (bi, 0, 0)),
        scratch_shapes=[pltpu.VMEM((FOX_HEADS * QSLOT, HEAD_DIM), jnp.float32),
                        pltpu.VMEM((LANE, 2 * FOX_W), jnp.float32),
                        pltpu.VMEM((FOX_HEADS * QSLOT, 1), jnp.float32),
                        pltpu.VMEM((FOX_HEADS * QSLOT, 1), jnp.float32),
                        pltpu.VMEM((FOX_HEADS * QSLOT, HEAD_DIM), jnp.float32)],
    )
    return pl.pallas_call(
        functools.partial(_fox_decode_kernel, n_pages=n_pages, tq=tq),
        grid_spec=grid_spec,
        out_shape=jax.ShapeDtypeStruct((b, tq, FOX_W), jnp.float32),
        compiler_params=pltpu.CompilerParams(
            dimension_semantics=("arbitrary", "arbitrary"), vmem_limit_bytes=VMEM_LIMIT),
        name="fox_decode",
    )(page_table, yb3, yb3, yb3, yb3, yb3, yb3, cum_q, cum_k, pool)


def layer_norm(x, g, b):
    xf = x.astype(jnp.float32)
    mu = jnp.mean(xf, -1, keepdims=True)
    var = jnp.mean(jnp.square(xf - mu), -1, keepdims=True)
    return ((xf - mu) * lax.rsqrt(var + LN_EPS) * g + b).astype(x.dtype)


def rms_norm(x, g):
    xf = x.astype(jnp.float32)
    return (xf * lax.rsqrt(jnp.mean(xf * xf, -1, keepdims=True) + LN_EPS) * g).astype(x.dtype)


def masked_softmax(s, mask):
    s = jnp.where(mask, s, NEG)
    m = jnp.max(s, -1, keepdims=True)
    p = jnp.where(mask, jnp.exp(s - m), 0.0)
    return p / jnp.maximum(jnp.sum(p, -1, keepdims=True), 1e-30)


def t5_bucket(dist):
    max_exact = N_BUCKETS // 2
    d = jnp.maximum(dist, 0)
    large = max_exact + (jnp.log(jnp.maximum(d, 1).astype(jnp.float32) / max_exact)
                         / math.log(MAX_DISTANCE / max_exact) * (N_BUCKETS - max_exact)).astype(jnp.int32)
    return jnp.where(d < max_exact, d, jnp.minimum(large, N_BUCKETS - 1))


def head_bias(table, dist):
    b = jnp.moveaxis(table[t5_bucket(dist)].astype(jnp.float32), -1, 0)
    return b.reshape((NSA_KV_GROUPS, NSA_HPG) + dist.shape)


def query_block(t, cap):
    return t if t <= cap else cap


def to_blocks(x, axis, qb):
    n = x.shape[axis] // qb
    return jnp.moveaxis(x.reshape(x.shape[:axis] + (n, qb) + x.shape[axis + 1:]), axis, 0)


def from_blocks(y):
    y = jnp.moveaxis(y, 0, 1)
    return y.reshape((y.shape[0], y.shape[1] * y.shape[2]) + y.shape[3:])


def gather_pages(pool, page_table):
    g = pool[page_table]
    return g.reshape((g.shape[0], g.shape[1] * g.shape[2]) + g.shape[3:])


def cmp_to_slc(n_cmp, n_slc):
    c0 = np.arange(n_cmp) * CMP_STRIDE
    s0 = np.arange(n_slc) * SLC_BLOCK
    ov = np.minimum(c0[:, None] + CMP_BLOCK, s0[None, :] + SLC_BLOCK) - np.maximum(c0[:, None], s0[None, :])
    return jnp.asarray(np.maximum(ov, 0) / CMP_STRIDE, dtype=jnp.float32)


def project(x, w_in, b_forget):
    B, T, D = x.shape
    offs = [int(o) for o in np.cumsum(IN_SIZES)[:-1]]
    y = _matmul(x.reshape(B * T, D), w_in).reshape(B, T, -1)
    q_n, kv_n, gate_n, q_f, kv_f, f_f = jnp.split(y, offs, axis=-1)
    qn = q_n.reshape(B, T, NSA_HEADS, HEAD_DIM)
    kvn = kv_n.reshape(B, T, 3, 2, NSA_KV_GROUPS, HEAD_DIM)
    gates = jax.nn.sigmoid(gate_n).reshape(B, T, NSA_HEADS, 3)
    qf = q_f.reshape(B, T, FOX_HEADS, HEAD_DIM)
    kvf = kv_f.reshape(B, T, 2, FOX_HEADS, HEAD_DIM)
    logf = jax.nn.log_sigmoid((f_f + b_forget).astype(jnp.float32))
    return qn, kvn, gates, qf, kvf, logf


def compress(k, pos, w1, w2):
    B, T, G, dk = k.shape
    n_cmp = (T - CMP_BLOCK) // CMP_STRIDE + 1
    halves = k[:, :(n_cmp + 1) * CMP_STRIDE].reshape(B, n_cmp + 1, CMP_STRIDE, G, dk)
    pe = pos.reshape(2, CMP_STRIDE, 1, dk)
    w1r = w1.reshape(2, CMP_STRIDE, dk, w1.shape[-1])
    h = (jnp.einsum('bnsgd,sdh->bngh', halves[:, :-1] + pe[0], w1r[0])
         + jnp.einsum('bnsgd,sdh->bngh', halves[:, 1:] + pe[1], w1r[1]))
    return jax.nn.gelu(h) @ w2


def nsa(q, nsa_full, win_ext, gates, bias_table, cmp_pos, cmp_w1, cmp_w2):
    B, Tq = q.shape[:2]
    Tk = nsa_full.shape[1]
    p0 = Tk - Tq
    dt = q.dtype
    scale = HEAD_DIM ** -0.5
    G, HPG = NSA_KV_GROUPS, NSA_HPG
    qg = q.reshape(B, Tq, G, HPG, HEAD_DIM)
    q_pos = p0 + jnp.arange(Tq)

    kc = compress(nsa_full[:, :, 0, 0], cmp_pos[0], cmp_w1[0], cmp_w2[0])
    vc = compress(nsa_full[:, :, 0, 1], cmp_pos[1], cmp_w1[1], cmp_w2[1])
    n_cmp = kc.shape[1]
    dist_c = q_pos[:, None] - (jnp.arange(n_cmp) * CMP_STRIDE + CMP_BLOCK - 1)[None, :]
    s_c = jnp.einsum('bqghd,bngd->bghqn', qg, kc).astype(jnp.float32) * scale + head_bias(bias_table, dist_c)
    p_c = masked_softmax(s_c, dist_c >= 0)
    o_c = jnp.einsum('bghqn,bngd->bqghd', p_c.astype(dt), vc)

    n_slc = -(-Tk // SLC_BLOCK)
    imp = jnp.einsum('bghqn,nj->bgqj', p_c, cmp_to_slc(n_cmp, n_slc))
    blk = jnp.arange(n_slc)[None, :]
    cur = (q_pos // SLC_BLOCK)[:, None]
    forced = (blk == 0) | (blk == cur) | (blk == cur - 1)
    imp = jnp.where(blk * SLC_BLOCK <= q_pos[:, None], imp + FORCE_BONUS * forced, NEG)
    _, sel = lax.top_k(imp, min(N_SELECT, n_slc))
    tok = (sel[..., None] * SLC_BLOCK + jnp.arange(SLC_BLOCK)).reshape(B, G, Tq, -1)
    kv_s = jnp.pad(nsa_full[:, :, 1], ((0, 0), (0, n_slc * SLC_BLOCK - Tk), (0, 0), (0, 0), (0, 0)))
    kv_s = kv_s.transpose(0, 3, 1, 2, 4)
    b_ix = jnp.arange(B)[:, None, None, None]
    g_ix = jnp.arange(G)[None, :, None, None]
    table_g = bias_table.reshape(N_BUCKETS, G, HPG)

    def slc_block(args):
        qc, tc, pc = args
        kv = kv_s[b_ix, g_ix, tc]
        dist = pc[None, None, :, None] - tc
        bias = jnp.moveaxis(table_g[t5_bucket(dist), g_ix].astype(jnp.float32), -1, 2)
        s = jnp.einsum('bqghd,bgqnd->bghqn', qc, kv[..., 0, :]).astype(jnp.float32) * scale + bias
        p = masked_softmax(s, (dist >= 0)[:, :, None])
        return jnp.einsum('bghqn,bgqnd->bqghd', p.astype(dt), kv[..., 1, :])

    qs = query_block(Tq, SLC_Q_BLOCK)
    o_s = from_blocks(lax.map(slc_block, (to_blocks(qg, 1, qs), to_blocks(tok, 2, qs), q_pos.reshape(-1, qs))))

    qw = query_block(Tq, Q_BLOCK)

    def win_block(args):
        c, qc = args
        kv = lax.dynamic_slice_in_dim(win_ext, c * qw, WINDOW + qw, axis=1)
        pos = p0 + c * qw + jnp.arange(qw)
        kpos = p0 - WINDOW + c * qw + jnp.arange(WINDOW + qw)
        dist = pos[:, None] - kpos[None, :]
        mask = (dist >= 0) & (dist <= WINDOW) & (kpos >= 0)[None, :]
        s = jnp.einsum('bqghd,bkgd->bghqk', qc, kv[:, :, 0]).astype(jnp.float32) * scale + head_bias(bias_table, dist)
        p = masked_softmax(s, mask)
        return jnp.einsum('bghqk,bkgd->bqghd', p.astype(dt), kv[:, :, 1])

    o_w = from_blocks(lax.map(win_block, (jnp.arange(Tq // qw), to_blocks(qg, 1, qw))))

    g = gates.reshape(B, Tq, G, HPG, 3)
    o = g[..., 0:1] * o_c + g[..., 1:2] * o_s + g[..., 2:3] * o_w
    return o.reshape(B, Tq, NSA_Q)


def fox(q, kv, logf):
    B, Tq = q.shape[:2]
    Tk = kv.shape[1]
    p0 = Tk - Tq
    dt = q.dtype
    scale = HEAD_DIM ** -0.5
    cum = jnp.cumsum(logf.astype(jnp.float32), axis=1)
    cum_k = jnp.moveaxis(cum, 1, 2)
    k_all, v_all = kv[:, :, 0], kv[:, :, 1]
    kpos = jnp.arange(Tk)

    def blk(args):
        qc, cq, pos = args
        s = jnp.einsum('bqhd,bkhd->bhqk', qc, k_all).astype(jnp.float32) * scale
        s = s + jnp.moveaxis(cq, 1, 2)[..., None] - cum_k[:, :, None, :]
        p = masked_softmax(s, kpos[None, :] <= pos[:, None])
        return jnp.einsum('bhqk,bkhd->bqhd', p.astype(dt), v_all)

    qb = query_block(Tq, Q_BLOCK)
    o = from_blocks(lax.map(blk, (to_blocks(q, 1, qb), to_blocks(cum[:, p0:], 1, qb),
                                  (p0 + jnp.arange(Tq)).reshape(-1, qb))))
    return o.reshape(B, Tq, FOX_W)


def layer_forward(x, past, w_in, b_forget, cmp_pos, cmp_w1, cmp_w2, bias_table, g_nsa, g_fox, w_out,
                  ln1_g, ln1_b, peer_w_q, peer_keys, peer_uv, ln2_g, ln2_b):
    B, T, D = x.shape
    qn, kvn, gates, qf, kvf, logf = project(x, w_in, b_forget)
    nsa_rows, win_rows = kvn[:, :, :2], kvn[:, :, 2]
    nsa_past, win_past, fox_past, logf_past = past
    buf_len = win_past.shape[1]
    nsa_full = jnp.concatenate([nsa_past, nsa_rows], axis=1)
    fox_full = jnp.concatenate([fox_past, kvf], axis=1)
    logf_full = jnp.concatenate([logf_past.astype(jnp.float32), logf], axis=1)
    pad = jnp.zeros((B, WINDOW - buf_len) + win_rows.shape[2:], win_rows.dtype)
    win_ext = jnp.concatenate([pad, win_past, win_rows], axis=1)
    o_n = nsa(qn, nsa_full, win_ext, gates, bias_table, cmp_pos, cmp_w1, cmp_w2)
    o_f = fox(qf, fox_full, logf_full)
    h = _post_attention(o_n.reshape(B * T, -1), o_f.reshape(B * T, -1), x.reshape(B * T, D),
                        g_nsa, g_fox, w_out, ln1_g, ln1_b)
    f = _peer(h, peer_w_q, peer_keys, peer_uv)
    y = _add_ln(h, f, ln2_g, ln2_b).reshape(B, T, D)
    return y, nsa_rows, win_ext[:, win_ext.shape[1] - buf_len:], kvf, logf


def sample_forward(x, caches, page_table, w_proj, cmp_pos, cmp_w1, cmp_w2, bias_table, g_nsa, g_fox, w_out,
                   ln1_g, ln1_b, peer_w_q, peer_keys, peer_uv, ln2_g, ln2_b):
    B, T, D = x.shape
    G = NSA_KV_GROUPS
    cache_nsa, cache_win, cache_fox, cache_logf = caches
    n_pool, page = cache_nsa.shape[:2]
    x2 = x.reshape(B * T, D)
    w_big, w_small, b_small = w_proj
    yb3 = _matmul(x2, w_big).reshape(B, T, BIG_WIDTH)
    ys3 = _proj_small(x2, w_small, b_small).reshape(B, T, SMALL_WIDTH)
    logf = ys3[:, :, 2 * LANE:2 * LANE + FOX_HEADS]
    kvn = yb3[:, :, CB_KVN * LANE:CB_QF * LANE].reshape(B, T, 3, 2, G, HEAD_DIM)
    kvf = yb3[:, :, CB_KF * LANE:].reshape(B, T, 2, FOX_HEADS, HEAD_DIM)

    o_n = _nsa_decode(yb3, ys3, cache_nsa.reshape(n_pool, page, 4 * NSA_KV), cache_win.reshape(B, -1, 2 * NSA_KV),
                      page_table, bias_table, cmp_pos, cmp_w1, cmp_w2)
    logf_past = cache_logf[page_table].reshape(B, -1, FOX_HEADS)
    o_f = _fox_decode(yb3, logf, cache_fox.reshape(n_pool, page, 2 * FOX_W), logf_past, page_table)
    h = _post_attention(o_n.reshape(B * T, NSA_Q), o_f.reshape(B * T, FOX_W), x2,
                        g_nsa, g_fox, w_out, ln1_g, ln1_b)
    f = _peer(h, peer_w_q, peer_keys, peer_uv)
    y = _add_ln(h, f, ln2_g, ln2_b).reshape(B, T, D)
    win_buf = jnp.concatenate([cache_win[:, T:], kvn[:, :, 2]], axis=1)
    return y, kvn[:, :, :2], win_buf, kvf, logf


def prompt_forward(x, w_proj, cmp_pos, cmp_w1, cmp_w2, bias_table, g_nsa, g_fox, w_out,
                   ln1_g, ln1_b, peer_w_q, peer_keys, peer_uv, ln2_g, ln2_b):
    B, T, D = x.shape
    G = NSA_KV_GROUPS
    x2 = x.reshape(B * T, D)
    w_big, w_small, b_small = w_proj
    yb = _matmul(x2, w_big)
    ys = _proj_small(x2, w_small, b_small)
    yb3 = yb.reshape(B, T, BIG_WIDTH)
    ys3 = ys.reshape(B, T, SMALL_WIDTH)
    logf = ys3[:, :, 2 * LANE:2 * LANE + FOX_HEADS]
    kvn = yb3[:, :, CB_KVN * LANE:CB_QF * LANE].reshape(B, T, 3, 2, G, HEAD_DIM)
    kvf = yb3[:, :, CB_KF * LANE:].reshape(B, T, 2, FOX_HEADS, HEAD_DIM)

    kcvc = _compress_prompt(yb3, cmp_pos, cmp_w1, cmp_w2)
    o_n = _nsa_prompt(yb3, ys3, kcvc, bias_table)
    o_f = _fox_prompt(yb3, logf)
    h = _post_attention(o_n.reshape(B * T, NSA_Q), o_f.reshape(B * T, FOX_W), x2,
                        g_nsa, g_fox, w_out, ln1_g, ln1_b)
    f = _peer(h, peer_w_q, peer_keys, peer_uv)
    y = _add_ln(h, f, ln2_g, ln2_b).reshape(B, T, D)
    buf_len = min(WINDOW, T)
    return y, kvn[:, :, :2], kvn[:, T - buf_len:, 2], kvf, logf


def kernel(x_prompt, x_sample, cache_nsa_kv, cache_nsa_win, cache_fox_kv, cache_fox_logf, page_table,
           w_in, b_forget, nsa_cmp_pos, nsa_cmp_w1, nsa_cmp_w2, rel_bias_table, g_nsa, g_fox, w_out,
           ln1_g, ln1_b, peer_w_q, peer_sub_keys, peer_u, peer_v, ln2_g, ln2_b):
    layer = 0
    w = (_permute_w_in(w_in[layer], b_forget[layer]), nsa_cmp_pos[layer], nsa_cmp_w1[layer], nsa_cmp_w2[layer],
         rel_bias_table, g_nsa[layer], g_fox[layer], w_out[layer], ln1_g[layer], ln1_b[layer],
         peer_w_q[layer], peer_sub_keys[layer], jnp.concatenate([peer_u[layer], peer_v[layer]], axis=1),
         ln2_g[layer], ln2_b[layer])
    yp, a_nsa, a_win, a_fox, a_logf = prompt_forward(x_prompt, *w)
    caches = (cache_nsa_kv[layer], cache_nsa_win[layer], cache_fox_kv[layer], cache_fox_logf[layer])
    ys, b_nsa, b_win, b_fox, b_logf = sample_forward(x_sample, caches, page_table, *w)
    return (yp, ys, a_nsa[None], a_win[None], a_fox[None], a_logf[None],
            b_nsa[None], b_win[None], b_fox[None], b_logf[None])
```

```python
import functools
import math

import jax
import jax.numpy as jnp
import numpy as np
from jax import lax
from jax.experimental import pallas as pl
from jax.experimental.pallas import tpu as pltpu

D_MODEL = 2048
HEAD_DIM = 128
NSA_HEADS = 8
NSA_KV_GROUPS = 2
NSA_HPG = NSA_HEADS // NSA_KV_GROUPS
CMP_BLOCK = 32
CMP_STRIDE = 16
SLC_BLOCK = 64
N_SELECT = 16
WINDOW = 512
FOX_HEADS = 8
NSA_Q = NSA_HEADS * HEAD_DIM
NSA_KV = NSA_KV_GROUPS * HEAD_DIM
FOX_W = FOX_HEADS * HEAD_DIM
IN_SIZES = (NSA_Q, 6 * NSA_KV, 3 * NSA_HEADS, FOX_W, 2 * FOX_W, FOX_HEADS)
N_BUCKETS = 32
MAX_DISTANCE = 128
PEER_HEADS = 8
N_KEYS = 128
PEER_TOPK = 16
PEER_QDIM = 256
Q_BLOCK = 128
SLC_Q_BLOCK = 32
TOKEN_CHUNK = 128
DEPTH = 1
ALPHA = (2.0 * DEPTH) ** 0.25
LN_EPS = 1e-5
NEG = -1e30
FORCE_BONUS = 1e4
SCALE = HEAD_DIM ** -0.5

LANE = 128
VMEM_LIMIT = 48 * 1024 * 1024

BIG_WIDTH = NSA_Q + 6 * NSA_KV + FOX_W + 2 * FOX_W
CB_QN = 0
CB_KVN = NSA_Q // LANE
CB_QF = CB_KVN + 6 * NSA_KV // LANE
CB_KF = CB_QF + FOX_W // LANE
CB_VF = CB_KF + FOX_W // LANE
SMALL_WIDTH = 3 * LANE


def _dot_nt(a, b):
    return lax.dot_general(a, b, (((1,), (1,)), ((), ())), preferred_element_type=jnp.float32)


def _dot(a, b):
    return jnp.dot(a, b, preferred_element_type=jnp.float32)


def _mm_kernel(x_ref, w_ref, o_ref):
    o_ref[...] = _dot(x_ref[...].astype(jnp.bfloat16), w_ref[...])


def _matmul(x, w, tm=512, tn=512):
    m, k = x.shape
    n = w.shape[1]
    tm = min(tm, m)
    n_pad = -(-n // tn) * tn
    wb = w.astype(jnp.bfloat16)
    if n_pad != n:
        wb = jnp.pad(wb, ((0, 0), (0, n_pad - n)))
    out = pl.pallas_call(
        _mm_kernel,
        grid=(m // tm, n_pad // tn),
        in_specs=[pl.BlockSpec((tm, k), lambda i, j: (i, 0)),
                  pl.BlockSpec((k, tn), lambda i, j: (0, j))],
        out_specs=pl.BlockSpec((tm, tn), lambda i, j: (i, j)),
        out_shape=jax.ShapeDtypeStruct((m, n_pad), jnp.float32),
        compiler_params=pltpu.CompilerParams(
            dimension_semantics=("parallel", "arbitrary"),
            vmem_limit_bytes=VMEM_LIMIT),
        name="dense_matmul",
    )(x, wb)
    return out[:, :n] if n_pad != n else out


def _proj_small_kernel(x_ref, w_ref, b_ref, o_ref):
    y = _dot(x_ref[...].astype(jnp.bfloat16), w_ref[...]) + b_ref[...]
    gates = y[:, :2 * LANE]
    o_ref[:, :2 * LANE] = 1.0 / (1.0 + jnp.exp(-gates))
    f = y[:, 2 * LANE:]
    o_ref[:, 2 * LANE:] = -(jnp.maximum(-f, 0.0) + jnp.log1p(jnp.exp(-jnp.abs(f))))


def _proj_small(x, w_small, b_small, tm=512):
    m, k = x.shape
    tm = min(tm, m)
    return pl.pallas_call(
        _proj_small_kernel,
        grid=(m // tm,),
        in_specs=[pl.BlockSpec((tm, k), lambda i: (i, 0)),
                  pl.BlockSpec((k, SMALL_WIDTH), lambda i: (0, 0)),
                  pl.BlockSpec((1, SMALL_WIDTH), lambda i: (0, 0))],
        out_specs=pl.BlockSpec((tm, SMALL_WIDTH), lambda i: (i, 0)),
        out_shape=jax.ShapeDtypeStruct((m, SMALL_WIDTH), jnp.float32),
        compiler_params=pltpu.CompilerParams(
            dimension_semantics=("parallel",), vmem_limit_bytes=VMEM_LIMIT),
        name="proj_small",
    )(x, w_small, b_small)


def _permute_w_in(w_in, b_forget):
    offs = [0] + [int(o) for o in np.cumsum(IN_SIZES)]
    q_n, kv_n, gate, q_f, kv_f, f_f = (w_in[:, offs[i]:offs[i + 1]] for i in range(6))
    w_big = jnp.concatenate([q_n, kv_n, q_f, kv_f], axis=1).astype(jnp.bfloat16)
    d = w_in.shape[0]
    n_gate = 3 * NSA_HPG
    zg = jnp.zeros((d, LANE - n_gate), w_in.dtype)
    zf = jnp.zeros((d, LANE - FOX_HEADS), w_in.dtype)
    w_small = jnp.concatenate([gate[:, :n_gate], zg, gate[:, n_gate:], zg, f_f, zf], axis=1).astype(jnp.bfloat16)
    b_small = jnp.concatenate([jnp.zeros((2 * LANE,), jnp.float32), b_forget.astype(jnp.float32),
                               jnp.zeros((LANE - FOX_HEADS,), jnp.float32)])[None]
    return w_big, w_small, b_small


def _gelu_tanh(h):
    return 0.5 * h * (1.0 + jnp.tanh(math.sqrt(2.0 / math.pi) * (h + 0.044715 * (h * h * h))))


def _compress_kernel(k_ref, pe_ref, w1_ref, w2_ref, o_ref, *, nh):
    def half(s0):
        acc = jnp.zeros((nh, HEAD_DIM), jnp.float32)
        for s in range(CMP_STRIDE):
            rows = k_ref[0, pl.ds(s, nh, stride=CMP_STRIDE), :] + pe_ref[0, s0 + s:s0 + s + 1, :]
            acc = acc + _dot(rows.astype(jnp.bfloat16), w1_ref[0, s0 + s])
        return acc

    first = half(0)
    second = half(CMP_STRIDE)
    h = first + pltpu.roll(second, nh - 1, 0)
    o_ref[0, 0] = _dot(_gelu_tanh(h).astype(jnp.bfloat16), w2_ref[0])


def _compress_prompt(yb3, cmp_pos, cmp_w1, cmp_w2):
    b, t, _ = yb3.shape
    nh = t // CMP_STRIDE
    n_kg = 2 * NSA_KV_GROUPS
    return pl.pallas_call(
        functools.partial(_compress_kernel, nh=nh),
        grid=(b, n_kg),
        in_specs=[pl.BlockSpec((1, t, LANE), lambda i, c: (i, 0, CB_KVN + c)),
                  pl.BlockSpec((1, CMP_BLOCK, HEAD_DIM), lambda i, c: (c // NSA_KV_GROUPS, 0, 0)),
                  pl.BlockSpec((1, CMP_BLOCK, HEAD_DIM, HEAD_DIM), lambda i, c: (c // NSA_KV_GROUPS, 0, 0, 0)),
                  pl.BlockSpec((1, HEAD_DIM, HEAD_DIM), lambda i, c: (c // NSA_KV_GROUPS, 0, 0))],
        out_specs=pl.BlockSpec((1, 1, nh, HEAD_DIM), lambda i, c: (i, c, 0, 0)),
        out_shape=jax.ShapeDtypeStruct((b, n_kg, nh, HEAD_DIM), jnp.float32),
        compiler_params=pltpu.CompilerParams(
            dimension_semantics=("parallel", "arbitrary"), vmem_limit_bytes=VMEM_LIMIT),
        name="nsa_compress",
    )(yb3, cmp_pos, cmp_w1.astype(jnp.bfloat16), cmp_w2.astype(jnp.bfloat16))


def _nsa_kernel(q_ref, kc_ref, vc_ref, ks_ref, vs_ref, kw_ref, vw_ref, gate_ref, biasc_ref, biast_ref,
                c2s_ref, expand_ref, o_ref, q_s, sel_s, m_s, l_s, acc_s, out_s, *, n_slc, n_top):
    i = pl.program_id(2)
    rows_q = NSA_HPG * LANE
    bf16 = jnp.bfloat16

    qt = q_ref[0]
    q_s[...] = jnp.concatenate([qt[:, h * LANE:(h + 1) * LANE] for h in range(NSA_HPG)], axis=0).astype(bf16)

    r = lax.broadcasted_iota(jnp.int32, (rows_q, LANE), 0) & (LANE - 1)
    c = lax.broadcasted_iota(jnp.int32, (rows_q, LANE), 1)
    q_pos = i * LANE + r
    s = _dot_nt(q_s[...], kc_ref[0, 0].astype(bf16)) * SCALE + biasc_ref[...].reshape(rows_q, LANE)
    mask = c * CMP_STRIDE + (CMP_BLOCK - 1) <= q_pos
    s = jnp.where(mask, s, NEG)
    p = jnp.where(mask, jnp.exp(s - jnp.max(s, -1, keepdims=True)), 0.0)
    p = p / jnp.maximum(jnp.sum(p, -1, keepdims=True), 1e-30)
    out_s[0] = _dot(p.astype(bf16), vc_ref[0, 0].astype(bf16))

    p_sum = p[0:LANE]
    for h in range(1, NSA_HPG):
        p_sum = p_sum + p[h * LANE:(h + 1) * LANE]
    p_hi = p_sum.astype(bf16)
    p_lo = (p_sum - p_hi.astype(jnp.float32)).astype(bf16)
    imp = _dot(p_hi, c2s_ref[...]) + _dot(p_lo, c2s_ref[...])
    jj = lax.broadcasted_iota(jnp.int32, (LANE, LANE), 1)
    qp = i * LANE + lax.broadcasted_iota(jnp.int32, (LANE, LANE), 0)
    cur = qp >> int(math.log2(SLC_BLOCK))
    forced = (jj == 0) | (jj == cur) | (jj == cur - 1)
    val = jnp.where(jj * SLC_BLOCK <= qp, imp + jnp.where(forced, FORCE_BONUS, 0.0), NEG)
    val = jnp.where(jj < n_slc, val, -3e38)
    rank = jnp.zeros((LANE, LANE), jnp.int32)
    for t in range(n_slc):
        col = val[:, t:t + 1]
        ahead = (col > val) | ((col == val) & (jj > t))
        rank = rank + ahead.astype(jnp.int32)
    sel_s[...] = jnp.where(rank < n_top, 1.0, 0.0).astype(bf16)

    def attend(k_ref, v_ref, lo, use_sel, use_win, slot):
        m_s[...] = jnp.full((rows_q, 1), NEG, jnp.float32)
        l_s[...] = jnp.zeros((rows_q, 1), jnp.float32)
        acc_s[...] = jnp.zeros((rows_q, HEAD_DIM), jnp.float32)

        def body(kt, carry):
            off = pl.multiple_of(kt * LANE, LANE)
            k = k_ref[0, pl.ds(off, LANE), :].astype(bf16)
            v = v_ref[0, pl.ds(off, LANE), :].astype(bf16)
            dq = i - kt
            sc = _dot_nt(q_s[...], k) * SCALE + biast_ref[0, jnp.minimum(dq, 2)]
            rr = lax.broadcasted_iota(jnp.int32, (rows_q, LANE), 0) & (LANE - 1)
            cc = lax.broadcasted_iota(jnp.int32, (rows_q, LANE), 1)
            dist = rr - cc + dq * LANE
            msk = dist >= 0
            if use_win:
                msk = msk & (dist <= WINDOW)
            if use_sel:
                se = _dot(sel_s[...], expand_ref[kt])
                msk = msk & (jnp.concatenate([se] * NSA_HPG, axis=0) > 0.5)
            sc = jnp.where(msk, sc, NEG)
            m_old = m_s[...]
            m_new = jnp.maximum(m_old, jnp.max(sc, -1, keepdims=True))
            pe = jnp.where(msk, jnp.exp(sc - m_new), 0.0)
            alpha = jnp.exp(m_old - m_new)
            l_s[...] = alpha * l_s[...] + jnp.sum(pe, -1, keepdims=True)
            acc_s[...] = alpha * acc_s[...] + _dot(pe.astype(bf16), v)
            m_s[...] = m_new
            return carry

        lax.fori_loop(lo, i + 1, body, 0)
        out_s[slot] = acc_s[...] / jnp.maximum(l_s[...], 1e-30)

    attend(ks_ref, vs_ref, 0, True, False, 1)
    attend(kw_ref, vw_ref, jnp.maximum(i - WINDOW // LANE, 0), False, True, 2)

    gt = gate_ref[0]
    for h in range(NSA_HPG):
        sl = slice(h * LANE, (h + 1) * LANE)
        o_ref[0, :, sl] = (gt[:, 3 * h:3 * h + 1] * out_s[0, sl, :]
                           + gt[:, 3 * h + 1:3 * h + 2] * out_s[1, sl, :]
                           + gt[:, 3 * h + 2:3 * h + 3] * out_s[2, sl, :])


def _t5_bucket_np(d):
    max_exact = N_BUCKETS // 2
    d = np.maximum(d, 0)
    large = max_exact + (np.log(np.maximum(d, 1).astype(np.float32) / np.float32(max_exact))
                         / np.float32(math.log(MAX_DISTANCE / max_exact)) * (N_BUCKETS - max_exact)).astype(np.int32)
    return np.where(d < max_exact, d, np.minimum(large, N_BUCKETS - 1)).astype(np.int32)


def _nsa_prompt(yb3, ys3, kcvc, bias_table):
    b, t, _ = yb3.shape
    n_t = t // LANE
    n_slc = t // SLC_BLOCK
    n_cmp = (t - CMP_BLOCK) // CMP_STRIDE + 1
    n_top = min(N_SELECT, n_slc)
    rows_q = NSA_HPG * LANE
    assert t % LANE == 0 and n_cmp <= LANE and n_slc <= LANE

    table = bias_table.astype(jnp.float32)
    dist_c = np.arange(t)[:, None] - (np.arange(LANE) * CMP_STRIDE + CMP_BLOCK - 1)[None, :]
    bias_c = jnp.moveaxis(table[_t5_bucket_np(dist_c)], -1, 0)
    rc = np.arange(LANE)[:, None] - np.arange(LANE)[None, :]
    buckets_t = np.stack([_t5_bucket_np(rc), _t5_bucket_np(rc + LANE), _t5_bucket_np(rc + 2 * LANE)])
    assert (_t5_bucket_np(np.arange(LANE + 1, 4 * LANE)) == N_BUCKETS - 1).all()
    bias_t = jnp.moveaxis(table[buckets_t], -1, 0)
    bias_t = bias_t.reshape(NSA_KV_GROUPS, NSA_HPG, 3, LANE, LANE).transpose(0, 2, 1, 3, 4)
    bias_t = bias_t.reshape(NSA_KV_GROUPS, 3, rows_q, LANE)

    c0 = np.arange(n_cmp) * CMP_STRIDE
    s0 = np.arange(n_slc) * SLC_BLOCK
    ov = np.minimum(c0[:, None] + CMP_BLOCK, s0[None, :] + SLC_BLOCK) - np.maximum(c0[:, None], s0[None, :])
    c2s = np.zeros((LANE, LANE), np.float32)
    c2s[:n_cmp, :n_slc] = np.maximum(ov, 0) / CMP_STRIDE
    expand = np.zeros((n_t, LANE, LANE), np.float32)
    for kt in range(n_t):
        tok_blk = (kt * LANE + np.arange(LANE)) // SLC_BLOCK
        expand[kt, tok_blk, np.arange(LANE)] = 1.0

    kv_spec = lambda cb: pl.BlockSpec((1, t, LANE), lambda bi, g, i: (bi, 0, cb + g))
    g_n = NSA_KV_GROUPS
    return pl.pallas_call(
        functools.partial(_nsa_kernel, n_slc=n_slc, n_top=n_top),
        grid=(b, g_n, n_t),
        in_specs=[
            pl.BlockSpec((1, LANE, rows_q), lambda bi, g, i: (bi, i, g)),
            pl.BlockSpec((1, 1, t // CMP_STRIDE, HEAD_DIM), lambda bi, g, i: (bi, g, 0, 0)),
            pl.BlockSpec((1, 1, t // CMP_STRIDE, HEAD_DIM), lambda bi, g, i: (bi, g_n + g, 0, 0)),
            kv_spec(CB_KVN + 2 * g_n), kv_spec(CB_KVN + 3 * g_n),
            kv_spec(CB_KVN + 4 * g_n), kv_spec(CB_KVN + 5 * g_n),
            pl.BlockSpec((1, LANE, LANE), lambda bi, g, i: (bi, i, g)),
            pl.BlockSpec((NSA_HPG, LANE, LANE), lambda bi, g, i: (g, i, 0)),
            pl.BlockSpec((1, 3, rows_q, LANE), lambda bi, g, i: (g, 0, 0, 0)),
            pl.BlockSpec((LANE, LANE), lambda bi, g, i: (0, 0)),
            pl.BlockSpec((n_t, LANE, LANE), lambda bi, g, i: (0, 0, 0)),
        ],
        out_specs=pl.BlockSpec((1, LANE, rows_q), lambda bi, g, i: (bi, i, g)),
        out_shape=jax.ShapeDtypeStruct((b, t, NSA_Q), jnp.float32),
        scratch_shapes=[pltpu.VMEM((rows_q, HEAD_DIM), jnp.bfloat16),
                        pltpu.VMEM((LANE, LANE), jnp.bfloat16),
                        pltpu.VMEM((rows_q, 1), jnp.float32),
                        pltpu.VMEM((rows_q, 1), jnp.float32),
                        pltpu.VMEM((rows_q, HEAD_DIM), jnp.float32),
                        pltpu.VMEM((3, rows_q, HEAD_DIM), jnp.float32)],
        compiler_params=pltpu.CompilerParams(
            dimension_semantics=("parallel", "parallel", "arbitrary"), vmem_limit_bytes=VMEM_LIMIT),
        name="nsa_prompt",
    )(yb3, kcvc, kcvc, yb3, yb3, yb3, yb3, ys3, bias_c, bias_t,
      jnp.asarray(c2s, jnp.bfloat16), jnp.asarray(expand, jnp.bfloat16))


FOX_TILE = 256


def _fox_kernel(q_ref, k_ref, v_ref, cq_ref, ck_ref, o_ref, m_s, l_s, acc_s):
    i = pl.program_id(2)
    tq = FOX_TILE
    bf16 = jnp.bfloat16
    q = q_ref[0].astype(bf16)
    cq = cq_ref[0, 0]
    m_s[...] = jnp.full((tq, 1), NEG, jnp.float32)
    l_s[...] = jnp.zeros((tq, 1), jnp.float32)
    acc_s[...] = jnp.zeros((tq, HEAD_DIM), jnp.float32)

    def body(kt, carry):
        off = pl.multiple_of(kt * tq, tq)
        k = k_ref[0, pl.ds(off, tq), :].astype(bf16)
        v = v_ref[0, pl.ds(off, tq), :].astype(bf16)
        sc = _dot_nt(q, k) * SCALE + cq - ck_ref[0, 0, kt]
        rr = lax.broadcasted_iota(jnp.int32, (tq, tq), 0)
        cc = lax.broadcasted_iota(jnp.int32, (tq, tq), 1)
        msk = cc + (kt - i) * tq <= rr
        sc = jnp.where(msk, sc, NEG)
        m_old = m_s[...]
        m_new = jnp.maximum(m_old, jnp.max(sc, -1, keepdims=True))
        pe = jnp.where(msk, jnp.exp(sc - m_new), 0.0)
        alpha = jnp.exp(m_old - m_new)
        l_s[...] = alpha * l_s[...] + jnp.sum(pe, -1, keepdims=True)
        acc_s[...] = alpha * acc_s[...] + _dot(pe.astype(bf16), v)
        m_s[...] = m_new
        return carry

    lax.fori_loop(0, i + 1, body, 0)
    o_ref[0] = acc_s[...] / jnp.maximum(l_s[...], 1e-30)


def _fox_prompt(yb3, logf):
    b, t, _ = yb3.shape
    tq = FOX_TILE
    n_t = t // tq
    cum = jnp.moveaxis(jnp.cumsum(logf.astype(jnp.float32), axis=1), 1, 2)
    cum_q = cum[..., None]
    cum_k = cum.reshape(b, FOX_HEADS, n_t, 1, tq)
    return pl.pallas_call(
        _fox_kernel,
        grid=(b, FOX_HEADS, n_t),
        in_specs=[pl.BlockSpec((1, tq, LANE), lambda bi, h, i: (bi, i, CB_QF + h)),
                  pl.BlockSpec((1, t, LANE), lambda bi, h, i: (bi, 0, CB_KF + h)),
                  pl.BlockSpec((1, t, LANE), lambda bi, h, i: (bi, 0, CB_VF + h)),
                  pl.BlockSpec((1, 1, tq, 1), lambda bi, h, i: (bi, h, i, 0)),
                  pl.BlockSpec((1, 1, n_t, 1, tq), lambda bi, h, i: (bi, h, 0, 0, 0))],
        out_specs=pl.BlockSpec((1, tq, LANE), lambda bi, h, i: (bi, i, h)),
        out_shape=jax.ShapeDtypeStruct((b, t, FOX_W), jnp.float32),
        scratch_shapes=[pltpu.VMEM((tq, 1), jnp.float32),
                        pltpu.VMEM((tq, 1), jnp.float32),
                        pltpu.VMEM((tq, HEAD_DIM), jnp.float32)],
        compiler_params=pltpu.CompilerParams(
            dimension_semantics=("parallel", "parallel", "arbitrary"), vmem_limit_bytes=VMEM_LIMIT),
        name="fox_prompt",
    )(yb3, yb3, yb3, cum_q, cum_k)


def _ln(z, g, b):
    mu = jnp.mean(z, -1, keepdims=True)
    zc = z - mu
    var = jnp.mean(zc * zc, -1, keepdims=True)
    return zc * lax.rsqrt(var + LN_EPS) * g + b


def _post_kernel(on_ref, of_ref, x_ref, gn_ref, gf_ref, w_ref, lg_ref, lb_ref, h_ref):
    def rms(o, g):
        return (o * lax.rsqrt(jnp.mean(o * o, -1, keepdims=True) + LN_EPS) * g).astype(jnp.bfloat16)

    mix = (_dot(rms(on_ref[...], gn_ref[...]), w_ref[:NSA_Q, :])
           + _dot(rms(of_ref[...], gf_ref[...]), w_ref[NSA_Q:, :]))
    h_ref[...] = _ln(ALPHA * x_ref[...] + mix, lg_ref[...], lb_ref[...])


def _post_attention(o_n, o_f, x, g_nsa, g_fox, w_out, ln_g, ln_b, tm=256):
    m, d = x.shape
    tm = min(tm, m)
    row = lambda n: pl.BlockSpec((1, n), lambda i: (0, 0))
    return pl.pallas_call(
        _post_kernel,
        grid=(m // tm,),
        in_specs=[pl.BlockSpec((tm, NSA_Q), lambda i: (i, 0)),
                  pl.BlockSpec((tm, FOX_W), lambda i: (i, 0)),
                  pl.BlockSpec((tm, d), lambda i: (i, 0)),
                  row(NSA_Q), row(FOX_W),
                  pl.BlockSpec((NSA_Q + FOX_W, d), lambda i: (0, 0)),
                  row(d), row(d)],
        out_specs=pl.BlockSpec((tm, d), lambda i: (i, 0)),
        out_shape=jax.ShapeDtypeStruct((m, d), jnp.float32),
        compiler_params=pltpu.CompilerParams(
            dimension_semantics=("parallel",), vmem_limit_bytes=VMEM_LIMIT),
        name="post_attention",
    )(o_n, o_f, x, g_nsa[None], g_fox[None], w_out.astype(jnp.bfloat16), ln_g[None], ln_b[None])


def _add_ln_kernel(h_ref, f_ref, g_ref, b_ref, o_ref):
    o_ref[...] = _ln(ALPHA * h_ref[...] + f_ref[...], g_ref[...], b_ref[...])


def _add_ln(h, f, ln_g, ln_b, tm=512):
    m, d = h.shape
    tm = min(tm, m)
    return pl.pallas_call(
        _add_ln_kernel,
        grid=(m // tm,),
        in_specs=[pl.BlockSpec((tm, d), lambda i: (i, 0)),
                  pl.BlockSpec((tm, d), lambda i: (i, 0)),
                  pl.BlockSpec((1, d), lambda i: (0, 0)),
                  pl.BlockSpec((1, d), lambda i: (0, 0))],
        out_specs=pl.BlockSpec((tm, d), lambda i: (i, 0)),
        out_shape=jax.ShapeDtypeStruct((m, d), jnp.float32),
        compiler_params=pltpu.CompilerParams(
            dimension_semantics=("parallel",), vmem_limit_bytes=VMEM_LIMIT),
        name="add_layer_norm",
    )(h, f, ln_g[None], ln_b[None])


PEER_TILE = 128
N_ROUTES = PEER_HEADS * PEER_TOPK


def _top_rows(vals, row_id, n_out, payload=None):
    big = float(vals.shape[0])
    out_v, out_i = [], []
    for _ in range(n_out):
        m = jnp.max(vals, axis=0, keepdims=True)
        win = jnp.min(jnp.where(vals == m, row_id, big), axis=0, keepdims=True)
        hit = row_id == win
        out_v.append(m)
        if payload is None:
            out_i.append(win)
        else:
            out_i.append(jnp.sum(jnp.where(hit, payload, 0.0), axis=0, keepdims=True))
        vals = jnp.where(hit, -jnp.inf, vals)
    return jnp.concatenate(out_v, axis=0), jnp.concatenate(out_i, axis=0)


def _peer_route_kernel(h_ref, wq_ref, keys_ref, g_ref, e_ref, sv_s, si_s):
    bf16 = jnp.bfloat16
    tm = PEER_TILE
    half = PEER_QDIM // 2
    q = _dot(h_ref[...].astype(bf16), wq_ref[...]).astype(bf16)
    key_id = lax.broadcasted_iota(jnp.int32, (N_KEYS, tm), 0).astype(jnp.float32)
    for hp in range(2 * PEER_HEADS):
        s_t = _dot_nt(keys_ref[hp], q[:, hp * half:(hp + 1) * half])
        sv, si = _top_rows(s_t, key_id, PEER_TOPK)
        sv_s[hp] = sv
        si_s[hp] = si
    pair_id = lax.broadcasted_iota(jnp.int32, (PEER_TOPK * PEER_TOPK, tm), 0).astype(jnp.float32)
    for h in range(PEER_HEADS):
        sv0, sv1 = sv_s[2 * h], sv_s[2 * h + 1]
        si0, si1 = si_s[2 * h], si_s[2 * h + 1]
        cand = jnp.concatenate([sv0[a:a + 1, :] + sv1 for a in range(PEER_TOPK)], axis=0)
        expert = jnp.concatenate([si0[a:a + 1, :] * float(N_KEYS) + si1 for a in range(PEER_TOPK)], axis=0)
        best, eid = _top_rows(cand, pair_id, PEER_TOPK, payload=expert)
        ex = jnp.exp(best - best[0:1, :])
        g_ref[0, h * PEER_TOPK:(h + 1) * PEER_TOPK, :] = ex / jnp.sum(ex, axis=0, keepdims=True)
        e_ref[0, h * PEER_TOPK:(h + 1) * PEER_TOPK, :] = eid.astype(jnp.int32)


def _peer_route(h, w_q, sub_keys):
    n, d = h.shape
    tm = PEER_TILE
    nb = n // tm
    n_hp = 2 * PEER_HEADS
    half = PEER_QDIM // 2
    out = jax.ShapeDtypeStruct((nb, N_ROUTES, tm), jnp.float32)
    return pl.pallas_call(
        _peer_route_kernel,
        grid=(nb,),
        in_specs=[pl.BlockSpec((tm, d), lambda i: (i, 0)),
                  pl.BlockSpec((d, PEER_HEADS * PEER_QDIM), lambda i: (0, 0)),
                  pl.BlockSpec((n_hp, N_KEYS, half), lambda i: (0, 0, 0))],
        out_specs=[pl.BlockSpec((1, N_ROUTES, tm), lambda i: (i, 0, 0)),
                   pl.BlockSpec((1, N_ROUTES, tm), lambda i: (i, 0, 0))],
        out_shape=[out, jax.ShapeDtypeStruct((nb, N_ROUTES, tm), jnp.int32)],
        scratch_shapes=[pltpu.VMEM((n_hp, PEER_TOPK, tm), jnp.float32),
                        pltpu.VMEM((n_hp, PEER_TOPK, tm), jnp.float32)],
        compiler_params=pltpu.CompilerParams(
            dimension_semantics=("parallel",), vmem_limit_bytes=VMEM_LIMIT),
        name="peer_route",
    )(h, w_q.astype(jnp.bfloat16), sub_keys.reshape(n_hp, N_KEYS, half).astype(jnp.bfloat16))


PACK_DTYPE = jnp.dtype("bfloat16")


def _pack_expert_tables(u, v):
    def pack(t):
        bits = lax.bitcast_convert_type(t.astype(PACK_DTYPE), jnp.uint16).astype(jnp.uint32)
        half = t.shape[1] // 2
        return bits[:, :half] | (bits[:, half:] << 16)
    return jnp.concatenate([pack(u), pack(v)], axis=1)


def _unpack_words(w):
    lo = lax.bitcast_convert_type(w << 16, jnp.float32)
    hi = lax.bitcast_convert_type(w & jnp.uint32(0xFFFF0000), jnp.float32)
    return lo, hi


def _peer_expert_kernel(e_ref, g_ref, x_ref, uv_hbm, o_ref, e_smem, buf, sem, esem):
    tm = PEER_TILE
    hw = D_MODEL // 2
    n_chunk = hw // LANE
    per_point = N_ROUTES // (2 * n_chunk)
    ids = pltpu.make_async_copy(e_ref.at[0], e_smem, esem)
    ids.start()
    ids.wait()

    def issue(t, slot, ks):
        for k in ks:
            pltpu.make_async_copy(uv_hbm.at[pl.ds(e_smem[k, t], 1), :],
                                  buf.at[slot, pl.ds(k, 1), :], sem.at[slot]).start()

    def wait_rows(slot):
        pltpu.make_async_copy(uv_hbm.at[pl.ds(0, N_ROUTES), :], buf.at[slot], sem.at[slot]).wait()

    def compute(t, slot, t_next):
        other = 1 - slot
        point = [0]

        def issue_some():
            issue(t_next, other, range(point[0] * per_point, (point[0] + 1) * per_point))
            point[0] += 1

        x_row = x_ref[pl.ds(t, 1), :]
        acc = None
        for c in range(n_chunk):
            lo, hi = _unpack_words(buf[slot, :, c * LANE:(c + 1) * LANE])
            term = lo * x_row[:, c * LANE:(c + 1) * LANE] + hi * x_row[:, hw + c * LANE:hw + (c + 1) * LANE]
            acc = term if acc is None else acc + term
            issue_some()
        s = jnp.sum(acc, axis=1, keepdims=True)
        gate = pltpu.roll(g_ref[0], jnp.where(t == 0, 0, tm - t), 1)[:, 0:1]
        coef = gate * _gelu_tanh(s)
        lo_out, hi_out = [], []
        for c in range(n_chunk):
            lo, hi = _unpack_words(buf[slot, :, hw + c * LANE:hw + (c + 1) * LANE])
            lo_out.append(jnp.sum(lo * coef, axis=0, keepdims=True))
            hi_out.append(jnp.sum(hi * coef, axis=0, keepdims=True))
            issue_some()
        o_ref[pl.ds(t, 1), :] = jnp.concatenate(lo_out + hi_out, axis=1)

    issue(0, 0, range(N_ROUTES))

    def body(i, carry):
        t = 2 * i
        wait_rows(0)
        compute(t, 0, t + 1)
        wait_rows(1)
        compute(t + 1, 1, jnp.minimum(t + 2, tm - 1))
        return carry

    lax.fori_loop(0, tm // 2, body, 0)
    wait_rows(0)


def _peer_experts(h, gates, experts, uv):
    n, d = h.shape
    tm = PEER_TILE
    nb = n // tm
    return pl.pallas_call(
        _peer_expert_kernel,
        grid=(nb,),
        in_specs=[pl.BlockSpec((1, N_ROUTES, tm), lambda i: (i, 0, 0)),
                  pl.BlockSpec((1, N_ROUTES, tm), lambda i: (i, 0, 0)),
                  pl.BlockSpec((tm, d), lambda i: (i, 0)),
                  pl.BlockSpec(memory_space=pl.ANY)],
        out_specs=pl.BlockSpec((tm, d), lambda i: (i, 0)),
        out_shape=jax.ShapeDtypeStruct((n, d), jnp.float32),
        scratch_shapes=[pltpu.SMEM((N_ROUTES, tm), jnp.int32),
                        pltpu.VMEM((2, N_ROUTES, d), jnp.uint32),
                        pltpu.SemaphoreType.DMA((2,)),
                        pltpu.SemaphoreType.DMA],
        compiler_params=pltpu.CompilerParams(
            dimension_semantics=("arbitrary",), vmem_limit_bytes=VMEM_LIMIT),
        name="peer_experts",
    )(experts, gates, h, uv)


def _peer(h, w_q, sub_keys, uv):
    gates, experts = _peer_route(h, w_q, sub_keys)
    return _peer_experts(h, gates, experts, uv)


QSLOT = 8
DEC_GROUP = 4


def _nsa_decode_kernel(pt_ref, q_ref, ks_new_ref, vs_new_ref, kw_new_ref, vw_new_ref, kwin_ref, vwin_ref,
                       gate_ref, biasc_ref, biast_ref, c2s_ref, expand_ref, pe_ref, w1_ref, w2_ref, pool_hbm,
                       o_ref, kbuf, wbuf, kc_s, q_s, sel_s, m_s, l_s, acc_s, out_s, sem,
                       *, n_pages, page, tq, n_slc, n_top):
    b = pl.program_id(0)
    g = pl.program_id(1)
    bf16 = jnp.bfloat16
    rows_q = NSA_HPG * QSLOT
    p0 = n_pages * page
    n_half = p0 // CMP_STRIDE
    i_slc = p0 // LANE
    i_win = WINDOW // LANE

    def page_copy(p, c, gg):
        col = (2 * c + gg) * LANE
        return pltpu.make_async_copy(pool_hbm.at[pt_ref[b, p], :, pl.ds(col, LANE)],
                                     kbuf.at[c, pl.ds(p * page, page), :], sem.at[c])

    def for_group(fn):
        for gg in range(NSA_KV_GROUPS):
            @pl.when(g == gg)
            def _():
                fn(gg)

    def start_pages(gg):
        for c in range(4):
            for p in range(n_pages):
                page_copy(p, c, gg).start()

    def wait_pages(cs):
        def fn(gg):
            for c in cs:
                for p in range(n_pages):
                    page_copy(p, c, gg).wait()
        return fn

    for_group(start_pages)

    q_s[...] = jnp.zeros((rows_q, HEAD_DIM), jnp.float32)
    qt = q_ref[0]
    for h in range(NSA_HPG):
        q_s[h * QSLOT:h * QSLOT + tq, :] = qt[:, h * LANE:(h + 1) * LANE]
    zeros_tile = jnp.zeros((LANE, HEAD_DIM), jnp.float32)
    wbuf[0, 0:WINDOW, :] = kwin_ref[0]
    wbuf[1, 0:WINDOW, :] = vwin_ref[0]
    wbuf[0, WINDOW:WINDOW + LANE, :] = zeros_tile
    wbuf[1, WINDOW:WINDOW + LANE, :] = zeros_tile
    wbuf[0, WINDOW:WINDOW + tq, :] = kw_new_ref[0]
    wbuf[1, WINDOW:WINDOW + tq, :] = vw_new_ref[0]

    for_group(wait_pages((0, 1)))
    for kv in range(2):
        def half(s0):
            acc = jnp.zeros((n_half, HEAD_DIM), jnp.float32)
            for s in range(CMP_STRIDE):
                rows = kbuf[kv, pl.ds(s, n_half, stride=CMP_STRIDE), :] + pe_ref[kv, s0 + s:s0 + s + 1, :]
                acc = acc + _dot(rows.astype(bf16), w1_ref[kv, s0 + s])
            return acc

        first = half(0)
        second = half(CMP_STRIDE)
        hmid = first + pltpu.roll(second, n_half - 1, 0)
        kc_s[kv] = _dot(_gelu_tanh(hmid).astype(bf16), w2_ref[kv])

    qb = q_s[...].astype(bf16)
    r = lax.broadcasted_iota(jnp.int32, (rows_q, n_half), 0) & (QSLOT - 1)
    c = lax.broadcasted_iota(jnp.int32, (rows_q, n_half), 1)
    s = _dot_nt(qb, kc_s[0].astype(bf16)) * SCALE + biasc_ref[...].reshape(rows_q, n_half)
    mask = c * CMP_STRIDE + (CMP_BLOCK - 1) <= p0 + r
    s = jnp.where(mask, s, NEG)
    p = jnp.where(mask, jnp.exp(s - jnp.max(s, -1, keepdims=True)), 0.0)
    p = p / jnp.maximum(jnp.sum(p, -1, keepdims=True), 1e-30)
    out_s[0] = _dot(p.astype(bf16), kc_s[1].astype(bf16))

    p_sum = p[0:QSLOT]
    for h in range(1, NSA_HPG):
        p_sum = p_sum + p[h * QSLOT:(h + 1) * QSLOT]
    p_hi = p_sum.astype(bf16)
    p_lo = (p_sum - p_hi.astype(jnp.float32)).astype(bf16)
    imp = _dot(p_hi, c2s_ref[...]) + _dot(p_lo, c2s_ref[...])
    wide = 2 * LANE
    jj = lax.broadcasted_iota(jnp.int32, (QSLOT, wide), 1)
    qp = p0 + lax.broadcasted_iota(jnp.int32, (QSLOT, wide), 0)
    cur = qp >> int(math.log2(SLC_BLOCK))
    forced = (jj == 0) | (jj == cur) | (jj == cur - 1)
    val = jnp.where(jj * SLC_BLOCK <= qp, imp + jnp.where(forced, FORCE_BONUS, 0.0), NEG)
    val = jnp.where(jj < n_slc, val, -3e38)
    rank = jnp.zeros((QSLOT, wide), jnp.int32)
    for t in range(n_slc):
        col = val[:, t:t + 1]
        ahead = (col > val) | ((col == val) & (jj > t))
        rank = rank + ahead.astype(jnp.int32)
    sel_s[...] = jnp.where(rank < n_top, 1.0, 0.0)[:, :LANE]

    def attend(buf, kslot, i_tile, use_sel, use_win, slot):
        m_s[...] = jnp.full((rows_q, 1), NEG, jnp.float32)
        l_s[...] = jnp.zeros((rows_q, 1), jnp.float32)
        acc_s[...] = jnp.zeros((rows_q, HEAD_DIM), jnp.float32)

        def tiles(k0, w, with_sel):
            off = pl.multiple_of(k0 * LANE, LANE)
            k = buf[kslot, pl.ds(off, w * LANE), :].astype(bf16)
            v = buf[kslot + 1, pl.ds(off, w * LANE), :].astype(bf16)
            bias = [biast_ref[0, jnp.minimum(i_tile - k0 - j, 2)] for j in range(w)]
            sc = _dot_nt(qb, k) * SCALE + (bias[0] if w == 1 else jnp.concatenate(bias, axis=1))
            rr = lax.broadcasted_iota(jnp.int32, (rows_q, w * LANE), 0) & (QSLOT - 1)
            cc = lax.broadcasted_iota(jnp.int32, (rows_q, w * LANE), 1)
            dist = rr - cc + (i_tile - k0) * LANE
            msk = dist >= 0
            if use_win:
                msk = msk & (dist <= WINDOW)
            if with_sel:
                sb = sel_s[...].astype(bf16)
                se = jnp.concatenate([_dot(sb, expand_ref[k0 + j]) for j in range(w)], axis=1)
                msk = msk & (jnp.concatenate([se] * NSA_HPG, axis=0) > 0.5)
            sc = jnp.where(msk, sc, NEG)
            m_old = m_s[...]
            m_new = jnp.maximum(m_old, jnp.max(sc, -1, keepdims=True))
            pe = jnp.where(msk, jnp.exp(sc - m_new), 0.0)
            alpha = jnp.exp(m_old - m_new)
            l_s[...] = alpha * l_s[...] + jnp.sum(pe, -1, keepdims=True)
            acc_s[...] = alpha * acc_s[...] + _dot(pe.astype(bf16), v)
            m_s[...] = m_new

        def body(grp, carry):
            tiles(grp * DEC_GROUP, DEC_GROUP, use_sel)
            return carry

        lax.fori_loop(0, i_tile // DEC_GROUP, body, 0)
        tiles(i_tile, 1, False)
        out_s[slot] = acc_s[...] / jnp.maximum(l_s[...], 1e-30)

    for_group(wait_pages((2, 3)))
    kbuf[2, p0:p0 + LANE, :] = zeros_tile
    kbuf[3, p0:p0 + LANE, :] = zeros_tile
    kbuf[2, p0:p0 + tq, :] = ks_new_ref[0]
    kbuf[3, p0:p0 + tq, :] = vs_new_ref[0]
    attend(kbuf, 2, i_slc, True, False, 1)
    attend(wbuf, 0, i_win, False, True, 2)

    gt = gate_ref[0]
    for h in range(NSA_HPG):
        sl = slice(h * QSLOT, h * QSLOT + tq)
        o_ref[0, :, h * LANE:(h + 1) * LANE] = (gt[:, 3 * h:3 * h + 1] * out_s[0, sl, :]
                                                 + gt[:, 3 * h + 1:3 * h + 2] * out_s[1, sl, :]
                                                 + gt[:, 3 * h + 2:3 * h + 3] * out_s[2, sl, :])


def _nsa_decode(yb3, ys3, pool, win_cache, page_table, bias_table, cmp_pos, cmp_w1, cmp_w2):
    b, tq, _ = yb3.shape
    n_pages = page_table.shape[1]
    page = pool.shape[1]
    p0 = n_pages * page
    tk = p0 + tq
    n_slc = -(-tk // SLC_BLOCK)
    n_cmp = (tk - CMP_BLOCK) // CMP_STRIDE + 1
    n_half = p0 // CMP_STRIDE
    n_top = min(N_SELECT, n_slc)
    rows_q = NSA_HPG * QSLOT
    g_n = NSA_KV_GROUPS
    wide = 2 * LANE
    assert p0 % LANE == 0 and tq <= QSLOT and tq <= SLC_BLOCK and p0 % SLC_BLOCK == 0
    assert n_cmp + 1 == n_half and n_slc <= wide and (n_slc - 1) * SLC_BLOCK == p0
    assert (p0 // LANE) % DEC_GROUP == 0 and (WINDOW // LANE) % DEC_GROUP == 0
    assert win_cache.shape[1] == WINDOW

    table = bias_table.astype(jnp.float32)
    dist_c = p0 + np.arange(QSLOT)[:, None] - (np.arange(n_half) * CMP_STRIDE + CMP_BLOCK - 1)[None, :]
    bias_c = jnp.moveaxis(table[_t5_bucket_np(dist_c)], -1, 0)
    rc = np.arange(QSLOT)[:, None] - np.arange(LANE)[None, :]
    buckets_t = np.stack([_t5_bucket_np(rc), _t5_bucket_np(rc + LANE), _t5_bucket_np(rc + 2 * LANE)])
    bias_t = jnp.moveaxis(table[buckets_t], -1, 0)
    bias_t = bias_t.reshape(g_n, NSA_HPG, 3, QSLOT, LANE).transpose(0, 2, 1, 3, 4).reshape(g_n, 3, rows_q, LANE)

    c0 = np.arange(n_cmp) * CMP_STRIDE
    s0 = np.arange(n_slc) * SLC_BLOCK
    ov = np.minimum(c0[:, None] + CMP_BLOCK, s0[None, :] + SLC_BLOCK) - np.maximum(c0[:, None], s0[None, :])
    c2s = np.zeros((n_half, wide), np.float32)
    c2s[:n_cmp, :n_slc] = np.maximum(ov, 0) / CMP_STRIDE
    n_t = p0 // LANE
    expand = np.zeros((n_t, LANE, LANE), np.float32)
    for kt in range(n_t):
        expand[kt, (kt * LANE + np.arange(LANE)) // SLC_BLOCK, np.arange(LANE)] = 1.0

    new_spec = lambda cb: pl.BlockSpec((1, tq, LANE), lambda bi, g, pt: (bi, 0, cb + g))
    full = lambda shape: pl.BlockSpec(shape, lambda bi, g, pt: (0,) * len(shape))
    grid_spec = pltpu.PrefetchScalarGridSpec(
        num_scalar_prefetch=1,
        grid=(b, g_n),
        in_specs=[
            pl.BlockSpec((1, tq, NSA_HPG * LANE), lambda bi, g, pt: (bi, 0, g)),
            new_spec(CB_KVN + 2 * g_n), new_spec(CB_KVN + 3 * g_n),
            new_spec(CB_KVN + 4 * g_n), new_spec(CB_KVN + 5 * g_n),
            pl.BlockSpec((1, WINDOW, LANE), lambda bi, g, pt: (bi, 0, g)),
            pl.BlockSpec((1, WINDOW, LANE), lambda bi, g, pt: (bi, 0, g_n + g)),
            pl.BlockSpec((1, tq, LANE), lambda bi, g, pt: (bi, 0, g)),
            pl.BlockSpec((NSA_HPG, QSLOT, n_half), lambda bi, g, pt: (g, 0, 0)),
            pl.BlockSpec((1, 3, rows_q, LANE), lambda bi, g, pt: (g, 0, 0, 0)),
            full((n_half, wide)), full((n_t, LANE, LANE)),
            full((2, CMP_BLOCK, HEAD_DIM)), full((2, CMP_BLOCK, HEAD_DIM, HEAD_DIM)), full((2, HEAD_DIM, HEAD_DIM)),
            pl.BlockSpec(memory_space=pl.ANY),
        ],
        out_specs=pl.BlockSpec((1, tq, NSA_HPG * LANE), lambda bi, g, pt: (bi, 0, g)),
        scratch_shapes=[pltpu.VMEM((4, p0 + LANE, HEAD_DIM), jnp.float32),
                        pltpu.VMEM((2, WINDOW + LANE, HEAD_DIM), jnp.float32),
                        pltpu.VMEM((2, n_half, HEAD_DIM), jnp.float32),
                        pltpu.VMEM((rows_q, HEAD_DIM), jnp.float32),
                        pltpu.VMEM((QSLOT, LANE), jnp.float32),
                        pltpu.VMEM((rows_q, 1), jnp.float32),
                        pltpu.VMEM((rows_q, 1), jnp.float32),
                        pltpu.VMEM((rows_q, HEAD_DIM), jnp.float32),
                        pltpu.VMEM((3, rows_q, HEAD_DIM), jnp.float32),
                        pltpu.SemaphoreType.DMA((4,))],
    )
    return pl.pallas_call(
        functools.partial(_nsa_decode_kernel, n_pages=n_pages, page=page, tq=tq, n_slc=n_slc, n_top=n_top),
        grid_spec=grid_spec,
        out_shape=jax.ShapeDtypeStruct((b, tq, NSA_Q), jnp.float32),
        compiler_params=pltpu.CompilerParams(
            dimension_semantics=("arbitrary", "arbitrary"), vmem_limit_bytes=VMEM_LIMIT),
        name="nsa_decode",
    )(page_table, yb3, yb3, yb3, yb3, yb3, win_cache, win_cache, ys3, bias_c, bias_t,
      jnp.asarray(c2s, jnp.bfloat16), jnp.asarray(expand, jnp.bfloat16),
      cmp_pos, cmp_w1.astype(jnp.bfloat16), cmp_w2.astype(jnp.bfloat16), pool)


def _fox_decode_kernel(pt_ref, qa_ref, qb_ref, ka_ref, kb_ref, va_ref, vb_ref, cq_ref, ck_ref, kv_ref,
                       o_ref, q_s, new_s, m_s, l_s, acc_s, *, n_pages, tq):
    p = pl.program_id(1)
    bf16 = jnp.bfloat16
    half_h = FOX_HEADS // 2

    rows = FOX_HEADS * QSLOT

    @pl.when(p == 0)
    def _():
        m_s[...] = jnp.full(m_s.shape, NEG, jnp.float32)
        l_s[...] = jnp.zeros(l_s.shape, jnp.float32)
        acc_s[...] = jnp.zeros(acc_s.shape, jnp.float32)
        q_s[...] = jnp.zeros(q_s.shape, jnp.float32)
        new_s[...] = jnp.zeros(new_s.shape, jnp.float32)
        for h in range(FOX_HEADS):
            src_q, src_k, src_v = (qa_ref, ka_ref, va_ref) if h < half_h else (qb_ref, kb_ref, vb_ref)
            lo = (h % half_h) * LANE
            q_s[h * QSLOT:h * QSLOT + tq, :] = src_q[0, :, lo:lo + LANE]
            new_s[0:tq, h * LANE:(h + 1) * LANE] = src_k[0, :, lo:lo + LANE]
            new_s[0:tq, FOX_W + h * LANE:FOX_W + (h + 1) * LANE] = src_v[0, :, lo:lo + LANE]

    def step(kv_at, is_new):
        qb = q_s[...].astype(bf16)
        sc = jnp.concatenate(
            [_dot_nt(qb[h * QSLOT:(h + 1) * QSLOT], kv_at(h * LANE).astype(bf16)) for h in range(FOX_HEADS)], axis=0)
        ck = jnp.concatenate([jnp.broadcast_to(ck_ref[0, h, 0], (QSLOT, LANE)) for h in range(FOX_HEADS)], axis=0)
        sc = sc * SCALE + cq_ref[0].reshape(rows, 1) - ck
        if is_new:
            rr = lax.broadcasted_iota(jnp.int32, (rows, LANE), 0) & (QSLOT - 1)
            cc = lax.broadcasted_iota(jnp.int32, (rows, LANE), 1)
            msk = cc <= rr
            sc = jnp.where(msk, sc, NEG)
        m_old = m_s[...]
        m_new = jnp.maximum(m_old, jnp.max(sc, -1, keepdims=True))
        pe = jnp.exp(sc - m_new)
        if is_new:
            pe = jnp.where(msk, pe, 0.0)
        alpha = jnp.exp(m_old - m_new)
        l_s[...] = alpha * l_s[...] + jnp.sum(pe, -1, keepdims=True)
        pb = pe.astype(bf16)
        pv = jnp.concatenate(
            [_dot(pb[h * QSLOT:(h + 1) * QSLOT], kv_at(FOX_W + h * LANE).astype(bf16)) for h in range(FOX_HEADS)],
            axis=0)
        acc_s[...] = alpha * acc_s[...] + pv
        m_s[...] = m_new

    @pl.when(p < n_pages)
    def _():
        step(lambda col: kv_ref[0, pl.ds(col // LANE, LANE, stride=2 * FOX_HEADS), :], False)

    @pl.when(p == n_pages)
    def _():
        step(lambda col: new_s[:, col:col + LANE], True)
        res = acc_s[...] / jnp.maximum(l_s[...], 1e-30)
        for h in range(FOX_HEADS):
            o_ref[0, :, h * LANE:(h + 1) * LANE] = res[h * QSLOT:h * QSLOT + tq, :]


def _fox_decode(yb3, logf_new, pool, logf_past, page_table):
    b, tq, _ = yb3.shape
    n_pages = page_table.shape[1]
    page = pool.shape[1] // (2 * FOX_HEADS)
    p0 = n_pages * page
    assert page == LANE and tq <= QSLOT
    cum = jnp.cumsum(jnp.concatenate([logf_past.astype(jnp.float32), logf_new], axis=1), axis=1)
    cum_q = jnp.pad(jnp.moveaxis(cum[:, p0:], 1, 2), ((0, 0), (0, 0), (0, QSLOT - tq)))[..., None]
    cum_k = jnp.pad(jnp.moveaxis(cum, 1, 2), ((0, 0), (0, 0), (0, LANE - tq)))
    cum_k = cum_k.reshape(b, FOX_HEADS, n_pages + 1, 1, LANE)
    wq = FOX_W // 2
    blk = lambda col: pl.BlockSpec((1, tq, wq), lambda bi, p, pt: (bi, 0, col))
    base_q, base_k, base_v = CB_QF * LANE // wq, CB_KF * LANE // wq, CB_VF * LANE // wq
    grid_spec = pltpu.PrefetchScalarGridSpec(
        num_scalar_prefetch=1,
        grid=(b, n_pages + 1),
        in_specs=[blk(base_q), blk(base_q + 1), blk(base_k), blk(base_k + 1), blk(base_v), blk(base_v + 1),
                  pl.BlockSpec((1, FOX_HEADS, QSLOT, 1), lambda bi, p, pt: (bi, 0, 0, 0)),
                  pl.BlockSpec((1, FOX_HEADS, 1, 1, LANE), lambda bi, p, pt: (bi, 0, p, 0, 0)),
                  pl.BlockSpec((1, page * 2 * FOX_HEADS, HEAD_DIM),
                               lambda bi, p, pt: (pt[bi, jnp.minimum(p, n_pages - 1)], 0, 0))],
        out_specs=pl.BlockSpec((1, tq, FOX_W), lambda bi, p, pt: (bi, 0, 0)),
        scratch_shapes=[pltpu.VMEM((FOX_HEADS * QSLOT, HEAD_DIM), jnp.float32),
                        pltpu.VMEM((LANE, 2 * FOX_W), jnp.float32),
                        pltpu.VMEM((FOX_HEADS * QSLOT, 1), jnp.float32),
                        pltpu.VMEM((FOX_HEADS * QSLOT, 1), jnp.float32),
                        pltpu.VMEM((FOX_HEADS * QSLOT, HEAD_DIM), jnp.float32)],
    )
    return pl.pallas_call(
        functools.partial(_fox_decode_kernel, n_pages=n_pages, tq=tq),
        grid_spec=grid_spec,
        out_shape=jax.ShapeDtypeStruct((b, tq, FOX_W), jnp.float32),
        compiler_params=pltpu.CompilerParams(
            dimension_semantics=("arbitrary", "arbitrary"), vmem_limit_bytes=VMEM_LIMIT),
        name="fox_decode",
    )(page_table, yb3, yb3, yb3, yb3, yb3, yb3, cum_q, cum_k, pool)


def layer_norm(x, g, b):
    xf = x.astype(jnp.float32)
    mu = jnp.mean(xf, -1, keepdims=True)
    var = jnp.mean(jnp.square(xf - mu), -1, keepdims=True)
    return ((xf - mu) * lax.rsqrt(var + LN_EPS) * g + b).astype(x.dtype)


def rms_norm(x, g):
    xf = x.astype(jnp.float32)
    return (xf * lax.rsqrt(jnp.mean(xf * xf, -1, keepdims=True) + LN_EPS) * g).astype(x.dtype)


def masked_softmax(s, mask):
    s = jnp.where(mask, s, NEG)
    m = jnp.max(s, -1, keepdims=True)
    p = jnp.where(mask, jnp.exp(s - m), 0.0)
    return p / jnp.maximum(jnp.sum(p, -1, keepdims=True), 1e-30)


def t5_bucket(dist):
    max_exact = N_BUCKETS // 2
    d = jnp.maximum(dist, 0)
    large = max_exact + (jnp.log(jnp.maximum(d, 1).astype(jnp.float32) / max_exact)
                         / math.log(MAX_DISTANCE / max_exact) * (N_BUCKETS - max_exact)).astype(jnp.int32)
    return jnp.where(d < max_exact, d, jnp.minimum(large, N_BUCKETS - 1))


def head_bias(table, dist):
    b = jnp.moveaxis(table[t5_bucket(dist)].astype(jnp.float32), -1, 0)
    return b.reshape((NSA_KV_GROUPS, NSA_HPG) + dist.shape)


def query_block(t, cap):
    return t if t <= cap else cap


def to_blocks(x, axis, qb):
    n = x.shape[axis] // qb
    return jnp.moveaxis(x.reshape(x.shape[:axis] + (n, qb) + x.shape[axis + 1:]), axis, 0)


def from_blocks(y):
    y = jnp.moveaxis(y, 0, 1)
    return y.reshape((y.shape[0], y.shape[1] * y.shape[2]) + y.shape[3:])


def gather_pages(pool, page_table):
    g = pool[page_table]
    return g.reshape((g.shape[0], g.shape[1] * g.shape[2]) + g.shape[3:])


def cmp_to_slc(n_cmp, n_slc):
    c0 = np.arange(n_cmp) * CMP_STRIDE
    s0 = np.arange(n_slc) * SLC_BLOCK
    ov = np.minimum(c0[:, None] + CMP_BLOCK, s0[None, :] + SLC_BLOCK) - np.maximum(c0[:, None], s0[None, :])
    return jnp.asarray(np.maximum(ov, 0) / CMP_STRIDE, dtype=jnp.float32)


def project(x, w_in, b_forget):
    B, T, D = x.shape
    offs = [int(o) for o in np.cumsum(IN_SIZES)[:-1]]
    y = _matmul(x.reshape(B * T, D), w_in).reshape(B, T, -1)
    q_n, kv_n, gate_n, q_f, kv_f, f_f = jnp.split(y, offs, axis=-1)
    qn = q_n.reshape(B, T, NSA_HEADS, HEAD_DIM)
    kvn = kv_n.reshape(B, T, 3, 2, NSA_KV_GROUPS, HEAD_DIM)
    gates = jax.nn.sigmoid(gate_n).reshape(B, T, NSA_HEADS, 3)
    qf = q_f.reshape(B, T, FOX_HEADS, HEAD_DIM)
    kvf = kv_f.reshape(B, T, 2, FOX_HEADS, HEAD_DIM)
    logf = jax.nn.log_sigmoid((f_f + b_forget).astype(jnp.float32))
    return qn, kvn, gates, qf, kvf, logf


def compress(k, pos, w1, w2):
    B, T, G, dk = k.shape
    n_cmp = (T - CMP_BLOCK) // CMP_STRIDE + 1
    halves = k[:, :(n_cmp + 1) * CMP_STRIDE].reshape(B, n_cmp + 1, CMP_STRIDE, G, dk)
    pe = pos.reshape(2, CMP_STRIDE, 1, dk)
    w1r = w1.reshape(2, CMP_STRIDE, dk, w1.shape[-1])
    h = (jnp.einsum('bnsgd,sdh->bngh', halves[:, :-1] + pe[0], w1r[0])
         + jnp.einsum('bnsgd,sdh->bngh', halves[:, 1:] + pe[1], w1r[1]))
    return jax.nn.gelu(h) @ w2


def nsa(q, nsa_full, win_ext, gates, bias_table, cmp_pos, cmp_w1, cmp_w2):
    B, Tq = q.shape[:2]
    Tk = nsa_full.shape[1]
    p0 = Tk - Tq
    dt = q.dtype
    scale = HEAD_DIM ** -0.5
    G, HPG = NSA_KV_GROUPS, NSA_HPG
    qg = q.reshape(B, Tq, G, HPG, HEAD_DIM)
    q_pos = p0 + jnp.arange(Tq)

    kc = compress(nsa_full[:, :, 0, 0], cmp_pos[0], cmp_w1[0], cmp_w2[0])
    vc = compress(nsa_full[:, :, 0, 1], cmp_pos[1], cmp_w1[1], cmp_w2[1])
    n_cmp = kc.shape[1]
    dist_c = q_pos[:, None] - (jnp.arange(n_cmp) * CMP_STRIDE + CMP_BLOCK - 1)[None, :]
    s_c = jnp.einsum('bqghd,bngd->bghqn', qg, kc).astype(jnp.float32) * scale + head_bias(bias_table, dist_c)
    p_c = masked_softmax(s_c, dist_c >= 0)
    o_c = jnp.einsum('bghqn,bngd->bqghd', p_c.astype(dt), vc)

    n_slc = -(-Tk // SLC_BLOCK)
    imp = jnp.einsum('bghqn,nj->bgqj', p_c, cmp_to_slc(n_cmp, n_slc))
    blk = jnp.arange(n_slc)[None, :]
    cur = (q_pos // SLC_BLOCK)[:, None]
    forced = (blk == 0) | (blk == cur) | (blk == cur - 1)
    imp = jnp.where(blk * SLC_BLOCK <= q_pos[:, None], imp + FORCE_BONUS * forced, NEG)
    _, sel = lax.top_k(imp, min(N_SELECT, n_slc))
    tok = (sel[..., None] * SLC_BLOCK + jnp.arange(SLC_BLOCK)).reshape(B, G, Tq, -1)
    kv_s = jnp.pad(nsa_full[:, :, 1], ((0, 0), (0, n_slc * SLC_BLOCK - Tk), (0, 0), (0, 0), (0, 0)))
    kv_s = kv_s.transpose(0, 3, 1, 2, 4)
    b_ix = jnp.arange(B)[:, None, None, None]
    g_ix = jnp.arange(G)[None, :, None, None]
    table_g = bias_table.reshape(N_BUCKETS, G, HPG)

    def slc_block(args):
        qc, tc, pc = args
        kv = kv_s[b_ix, g_ix, tc]
        dist = pc[None, None, :, None] - tc
        bias = jnp.moveaxis(table_g[t5_bucket(dist), g_ix].astype(jnp.float32), -1, 2)
        s = jnp.einsum('bqghd,bgqnd->bghqn', qc, kv[..., 0, :]).astype(jnp.float32) * scale + bias
        p = masked_softmax(s, (dist >= 0)[:, :, None])
        return jnp.einsum('bghqn,bgqnd->bqghd', p.astype(dt), kv[..., 1, :])

    qs = query_block(Tq, SLC_Q_BLOCK)
    o_s = from_blocks(lax.map(slc_block, (to_blocks(qg, 1, qs), to_blocks(tok, 2, qs), q_pos.reshape(-1, qs))))

    qw = query_block(Tq, Q_BLOCK)

    def win_block(args):
        c, qc = args
        kv = lax.dynamic_slice_in_dim(win_ext, c * qw, WINDOW + qw, axis=1)
        pos = p0 + c * qw + jnp.arange(qw)
        kpos = p0 - WINDOW + c * qw + jnp.arange(WINDOW + qw)
        dist = pos[:, None] - kpos[None, :]
        mask = (dist >= 0) & (dist <= WINDOW) & (kpos >= 0)[None, :]
        s = jnp.einsum('bqghd,bkgd->bghqk', qc, kv[:, :, 0]).astype(jnp.float32) * scale + head_bias(bias_table, dist)
        p = masked_softmax(s, mask)
        return jnp.einsum('bghqk,bkgd->bqghd', p.astype(dt), kv[:, :, 1])

    o_w = from_blocks(lax.map(win_block, (jnp.arange(Tq // qw), to_blocks(qg, 1, qw))))

    g = gates.reshape(B, Tq, G, HPG, 3)
    o = g[..., 0:1] * o_c + g[..., 1:2] * o_s + g[..., 2:3] * o_w
    return o.reshape(B, Tq, NSA_Q)


def fox(q, kv, logf):
    B, Tq = q.shape[:2]
    Tk = kv.shape[1]
    p0 = Tk - Tq
    dt = q.dtype
    scale = HEAD_DIM ** -0.5
    cum = jnp.cumsum(logf.astype(jnp.float32), axis=1)
    cum_k = jnp.moveaxis(cum, 1, 2)
    k_all, v_all = kv[:, :, 0], kv[:, :, 1]
    kpos = jnp.arange(Tk)

    def blk(args):
        qc, cq, pos = args
        s = jnp.einsum('bqhd,bkhd->bhqk', qc, k_all).astype(jnp.float32) * scale
        s = s + jnp.moveaxis(cq, 1, 2)[..., None] - cum_k[:, :, None, :]
        p = masked_softmax(s, kpos[None, :] <= pos[:, None])
        return jnp.einsum('bhqk,bkhd->bqhd', p.astype(dt), v_all)

    qb = query_block(Tq, Q_BLOCK)
    o = from_blocks(lax.map(blk, (to_blocks(q, 1, qb), to_blocks(cum[:, p0:], 1, qb),
                                  (p0 + jnp.arange(Tq)).reshape(-1, qb))))
    return o.reshape(B, Tq, FOX_W)


def layer_forward(x, past, w_in, b_forget, cmp_pos, cmp_w1, cmp_w2, bias_table, g_nsa, g_fox, w_out,
                  ln1_g, ln1_b, peer_w_q, peer_keys, peer_uv, ln2_g, ln2_b):
    B, T, D = x.shape
    qn, kvn, gates, qf, kvf, logf = project(x, w_in, b_forget)
    nsa_rows, win_rows = kvn[:, :, :2], kvn[:, :, 2]
    nsa_past, win_past, fox_past, logf_past = past
    buf_len = win_past.shape[1]
    nsa_full = jnp.concatenate([nsa_past, nsa_rows], axis=1)
    fox_full = jnp.concatenate([fox_past, kvf], axis=1)
    logf_full = jnp.concatenate([logf_past.astype(jnp.float32), logf], axis=1)
    pad = jnp.zeros((B, WINDOW - buf_len) + win_rows.shape[2:], win_rows.dtype)
    win_ext = jnp.concatenate([pad, win_past, win_rows], axis=1)
    o_n = nsa(qn, nsa_full, win_ext, gates, bias_table, cmp_pos, cmp_w1, cmp_w2)
    o_f = fox(qf, fox_full, logf_full)
    h = _post_attention(o_n.reshape(B * T, -1), o_f.reshape(B * T, -1), x.reshape(B * T, D),
                        g_nsa, g_fox, w_out, ln1_g, ln1_b)
    f = _peer(h, peer_w_q, peer_keys, peer_uv)
    y = _add_ln(h, f, ln2_g, ln2_b).reshape(B, T, D)
    return y, nsa_rows, win_ext[:, win_ext.shape[1] - buf_len:], kvf, logf


def sample_forward(x, caches, page_table, w_proj, cmp_pos, cmp_w1, cmp_w2, bias_table, g_nsa, g_fox, w_out,
                   ln1_g, ln1_b, peer_w_q, peer_keys, peer_uv, ln2_g, ln2_b):
    B, T, D = x.shape
    G = NSA_KV_GROUPS
    cache_nsa, cache_win, cache_fox, cache_logf = caches
    n_pool, page = cache_nsa.shape[:2]
    x2 = x.reshape(B * T, D)
    w_big, w_small, b_small = w_proj
    yb3 = _matmul(x2, w_big).reshape(B, T, BIG_WIDTH)
    ys3 = _proj_small(x2, w_small, b_small).reshape(B, T, SMALL_WIDTH)
    logf = ys3[:, :, 2 * LANE:2 * LANE + FOX_HEADS]
    kvn = yb3[:, :, CB_KVN * LANE:CB_QF * LANE].reshape(B, T, 3, 2, G, HEAD_DIM)
    kvf = yb3[:, :, CB_KF * LANE:].reshape(B, T, 2, FOX_HEADS, HEAD_DIM)

    o_n = _nsa_decode(yb3, ys3, cache_nsa.reshape(n_pool, page, 4 * NSA_KV), cache_win.reshape(B, -1, 2 * NSA_KV),
                      page_table, bias_table, cmp_pos, cmp_w1, cmp_w2)
    logf_past = cache_logf[page_table].reshape(B, -1, FOX_HEADS)
    o_f = _fox_decode(yb3, logf, cache_fox.reshape(n_pool, page * 2 * FOX_HEADS, HEAD_DIM), logf_past, page_table)
    h = _post_attention(o_n.reshape(B * T, NSA_Q), o_f.reshape(B * T, FOX_W), x2,
                        g_nsa, g_fox, w_out, ln1_g, ln1_b)
    f = _peer(h, peer_w_q, peer_keys, peer_uv)
    y = _add_ln(h, f, ln2_g, ln2_b).reshape(B, T, D)
    win_buf = jnp.concatenate([cache_win[:, T:], kvn[:, :, 2]], axis=1)
    return y, kvn[:, :, :2], win_buf, kvf, logf


def prompt_forward(x, w_proj, cmp_pos, cmp_w1, cmp_w2, bias_table, g_nsa, g_fox, w_out,
                   ln1_g, ln1_b, peer_w_q, peer_keys, peer_uv, ln2_g, ln2_b):
    B, T, D = x.shape
    G = NSA_KV_GROUPS
    x2 = x.reshape(B * T, D)
    w_big, w_small, b_small = w_proj
    yb = _matmul(x2, w_big)
    ys = _proj_small(x2, w_small, b_small)
    yb3 = yb.reshape(B, T, BIG_WIDTH)
    ys3 = ys.reshape(B, T, SMALL_WIDTH)
    logf = ys3[:, :, 2 * LANE:2 * LANE + FOX_HEADS]
    kvn = yb3[:, :, CB_KVN * LANE:CB_QF * LANE].reshape(B, T, 3, 2, G, HEAD_DIM)
    kvf = yb3[:, :, CB_KF * LANE:].reshape(B, T, 2, FOX_HEADS, HEAD_DIM)

    kcvc = _compress_prompt(yb3, cmp_pos, cmp_w1, cmp_w2)
    o_n = _nsa_prompt(yb3, ys3, kcvc, bias_table)
    o_f = _fox_prompt(yb3, logf)
    h = _post_attention(o_n.reshape(B * T, NSA_Q), o_f.reshape(B * T, FOX_W), x2,
                        g_nsa, g_fox, w_out, ln1_g, ln1_b)
    f = _peer(h, peer_w_q, peer_keys, peer_uv)
    y = _add_ln(h, f, ln2_g, ln2_b).reshape(B, T, D)
    buf_len = min(WINDOW, T)
    return y, kvn[:, :, :2], kvn[:, T - buf_len:, 2], kvf, logf


def kernel(x_prompt, x_sample, cache_nsa_kv, cache_nsa_win, cache_fox_kv, cache_fox_logf, page_table,
           w_in, b_forget, nsa_cmp_pos, nsa_cmp_w1, nsa_cmp_w2, rel_bias_table, g_nsa, g_fox, w_out,
           ln1_g, ln1_b, peer_w_q, peer_sub_keys, peer_u, peer_v, ln2_g, ln2_b):
    layer = 0
    w = (_permute_w_in(w_in[layer], b_forget[layer]), nsa_cmp_pos[layer], nsa_cmp_w1[layer], nsa_cmp_w2[layer],
         rel_bias_table, g_nsa[layer], g_fox[layer], w_out[layer], ln1_g[layer], ln1_b[layer],
         peer_w_q[layer], peer_sub_keys[layer], _pack_expert_tables(peer_u[layer], peer_v[layer]),
         ln2_g[layer], ln2_b[layer])
    yp, a_nsa, a_win, a_fox, a_logf = prompt_forward(x_prompt, *w)
    caches = (cache_nsa_kv[layer], cache_nsa_win[layer], cache_fox_kv[layer], cache_fox_logf[layer])
    ys, b_nsa, b_win, b_fox, b_logf = sample_forward(x_sample, caches, page_table, *w)
    return (yp, ys, a_nsa[None], a_win[None], a_fox[None], a_logf[None],
            b_nsa[None], b_win[None], b_fox[None], b_logf[None])
```

```python
import functools
import math

import jax
import jax.numpy as jnp
import numpy as np
from jax import lax
from jax.experimental import pallas as pl
from jax.experimental.pallas import tpu as pltpu

D_MODEL = 2048
HEAD_DIM = 128
NSA_HEADS = 8
NSA_KV_GROUPS = 2
NSA_HPG = NSA_HEADS // NSA_KV_GROUPS
CMP_BLOCK = 32
CMP_STRIDE = 16
SLC_BLOCK = 64
N_SELECT = 16
WINDOW = 512
FOX_HEADS = 8
NSA_Q = NSA_HEADS * HEAD_DIM
NSA_KV = NSA_KV_GROUPS * HEAD_DIM
FOX_W = FOX_HEADS * HEAD_DIM
IN_SIZES = (NSA_Q, 6 * NSA_KV, 3 * NSA_HEADS, FOX_W, 2 * FOX_W, FOX_HEADS)
N_BUCKETS = 32
MAX_DISTANCE = 128
PEER_HEADS = 8
N_KEYS = 128
PEER_TOPK = 16
PEER_QDIM = 256
Q_BLOCK = 128
SLC_Q_BLOCK = 32
TOKEN_CHUNK = 128
DEPTH = 1
ALPHA = (2.0 * DEPTH) ** 0.25
LN_EPS = 1e-5
NEG = -1e30
FORCE_BONUS = 1e4
SCALE = HEAD_DIM ** -0.5

LANE = 128
VMEM_LIMIT = 48 * 1024 * 1024

BIG_WIDTH = NSA_Q + 6 * NSA_KV + FOX_W + 2 * FOX_W
CB_QN = 0
CB_KVN = NSA_Q // LANE
CB_QF = CB_KVN + 6 * NSA_KV // LANE
CB_KF = CB_QF + FOX_W // LANE
CB_VF = CB_KF + FOX_W // LANE
SMALL_WIDTH = 3 * LANE


def _dot_nt(a, b):
    return lax.dot_general(a, b, (((1,), (1,)), ((), ())), preferred_element_type=jnp.float32)


def _dot(a, b):
    return jnp.dot(a, b, preferred_element_type=jnp.float32)


def _mm_kernel(x_ref, w_ref, o_ref):
    o_ref[...] = _dot(x_ref[...].astype(jnp.bfloat16), w_ref[...])


def _matmul(x, w, tm=512, tn=512):
    m, k = x.shape
    n = w.shape[1]
    tm = min(tm, m)
    n_pad = -(-n // tn) * tn
    wb = w.astype(jnp.bfloat16)
    if n_pad != n:
        wb = jnp.pad(wb, ((0, 0), (0, n_pad - n)))
    out = pl.pallas_call(
        _mm_kernel,
        grid=(m // tm, n_pad // tn),
        in_specs=[pl.BlockSpec((tm, k), lambda i, j: (i, 0)),
                  pl.BlockSpec((k, tn), lambda i, j: (0, j))],
        out_specs=pl.BlockSpec((tm, tn), lambda i, j: (i, j)),
        out_shape=jax.ShapeDtypeStruct((m, n_pad), jnp.float32),
        compiler_params=pltpu.CompilerParams(
            dimension_semantics=("parallel", "arbitrary"),
            vmem_limit_bytes=VMEM_LIMIT),
        name="dense_matmul",
    )(x, wb)
    return out[:, :n] if n_pad != n else out


def _proj_small_kernel(x_ref, w_ref, b_ref, o_ref):
    y = _dot(x_ref[...].astype(jnp.bfloat16), w_ref[...]) + b_ref[...]
    gates = y[:, :2 * LANE]
    o_ref[:, :2 * LANE] = 1.0 / (1.0 + jnp.exp(-gates))
    f = y[:, 2 * LANE:]
    o_ref[:, 2 * LANE:] = -(jnp.maximum(-f, 0.0) + jnp.log1p(jnp.exp(-jnp.abs(f))))


def _proj_small(x, w_small, b_small, tm=512):
    m, k = x.shape
    tm = min(tm, m)
    return pl.pallas_call(
        _proj_small_kernel,
        grid=(m // tm,),
        in_specs=[pl.BlockSpec((tm, k), lambda i: (i, 0)),
                  pl.BlockSpec((k, SMALL_WIDTH), lambda i: (0, 0)),
                  pl.BlockSpec((1, SMALL_WIDTH), lambda i: (0, 0))],
        out_specs=pl.BlockSpec((tm, SMALL_WIDTH), lambda i: (i, 0)),
        out_shape=jax.ShapeDtypeStruct((m, SMALL_WIDTH), jnp.float32),
        compiler_params=pltpu.CompilerParams(
            dimension_semantics=("parallel",), vmem_limit_bytes=VMEM_LIMIT),
        name="proj_small",
    )(x, w_small, b_small)


def _permute_w_in(w_in, b_forget):
    offs = [0] + [int(o) for o in np.cumsum(IN_SIZES)]
    q_n, kv_n, gate, q_f, kv_f, f_f = (w_in[:, offs[i]:offs[i + 1]] for i in range(6))
    w_big = jnp.concatenate([q_n, kv_n, q_f, kv_f], axis=1).astype(jnp.bfloat16)
    d = w_in.shape[0]
    n_gate = 3 * NSA_HPG
    zg = jnp.zeros((d, LANE - n_gate), w_in.dtype)
    zf = jnp.zeros((d, LANE - FOX_HEADS), w_in.dtype)
    w_small = jnp.concatenate([gate[:, :n_gate], zg, gate[:, n_gate:], zg, f_f, zf], axis=1).astype(jnp.bfloat16)
    b_small = jnp.concatenate([jnp.zeros((2 * LANE,), jnp.float32), b_forget.astype(jnp.float32),
                               jnp.zeros((LANE - FOX_HEADS,), jnp.float32)])[None]
    return w_big, w_small, b_small


def _gelu_tanh(h):
    return 0.5 * h * (1.0 + jnp.tanh(math.sqrt(2.0 / math.pi) * (h + 0.044715 * (h * h * h))))


def _compress_kernel(k_ref, pe_ref, w1_ref, w2_ref, o_ref, *, nh):
    def half(s0):
        acc = jnp.zeros((nh, HEAD_DIM), jnp.float32)
        for s in range(CMP_STRIDE):
            rows = k_ref[0, pl.ds(s, nh, stride=CMP_STRIDE), :] + pe_ref[0, s0 + s:s0 + s + 1, :]
            acc = acc + _dot(rows.astype(jnp.bfloat16), w1_ref[0, s0 + s])
        return acc

    first = half(0)
    second = half(CMP_STRIDE)
    h = first + pltpu.roll(second, nh - 1, 0)
    o_ref[0, 0] = _dot(_gelu_tanh(h).astype(jnp.bfloat16), w2_ref[0])


def _compress_prompt(yb3, cmp_pos, cmp_w1, cmp_w2):
    b, t, _ = yb3.shape
    nh = t // CMP_STRIDE
    n_kg = 2 * NSA_KV_GROUPS
    return pl.pallas_call(
        functools.partial(_compress_kernel, nh=nh),
        grid=(b, n_kg),
        in_specs=[pl.BlockSpec((1, t, LANE), lambda i, c: (i, 0, CB_KVN + c)),
                  pl.BlockSpec((1, CMP_BLOCK, HEAD_DIM), lambda i, c: (c // NSA_KV_GROUPS, 0, 0)),
                  pl.BlockSpec((1, CMP_BLOCK, HEAD_DIM, HEAD_DIM), lambda i, c: (c // NSA_KV_GROUPS, 0, 0, 0)),
                  pl.BlockSpec((1, HEAD_DIM, HEAD_DIM), lambda i, c: (c // NSA_KV_GROUPS, 0, 0))],
        out_specs=pl.BlockSpec((1, 1, nh, HEAD_DIM), lambda i, c: (i, c, 0, 0)),
        out_shape=jax.ShapeDtypeStruct((b, n_kg, nh, HEAD_DIM), jnp.float32),
        compiler_params=pltpu.CompilerParams(
            dimension_semantics=("parallel", "arbitrary"), vmem_limit_bytes=VMEM_LIMIT),
        name="nsa_compress",
    )(yb3, cmp_pos, cmp_w1.astype(jnp.bfloat16), cmp_w2.astype(jnp.bfloat16))


def _nsa_kernel(q_ref, kc_ref, vc_ref, ks_ref, vs_ref, kw_ref, vw_ref, gate_ref, biasc_ref, biast_ref,
                c2s_ref, expand_ref, o_ref, q_s, sel_s, m_s, l_s, acc_s, out_s, *, n_slc, n_top):
    i = pl.program_id(2)
    rows_q = NSA_HPG * LANE
    bf16 = jnp.bfloat16

    qt = q_ref[0]
    q_s[...] = jnp.concatenate([qt[:, h * LANE:(h + 1) * LANE] for h in range(NSA_HPG)], axis=0).astype(bf16)

    r = lax.broadcasted_iota(jnp.int32, (rows_q, LANE), 0) & (LANE - 1)
    c = lax.broadcasted_iota(jnp.int32, (rows_q, LANE), 1)
    q_pos = i * LANE + r
    s = _dot_nt(q_s[...], kc_ref[0, 0].astype(bf16)) * SCALE + biasc_ref[...].reshape(rows_q, LANE)
    mask = c * CMP_STRIDE + (CMP_BLOCK - 1) <= q_pos
    s = jnp.where(mask, s, NEG)
    p = jnp.where(mask, jnp.exp(s - jnp.max(s, -1, keepdims=True)), 0.0)
    p = p / jnp.maximum(jnp.sum(p, -1, keepdims=True), 1e-30)
    out_s[0] = _dot(p.astype(bf16), vc_ref[0, 0].astype(bf16))

    p_sum = p[0:LANE]
    for h in range(1, NSA_HPG):
        p_sum = p_sum + p[h * LANE:(h + 1) * LANE]
    p_hi = p_sum.astype(bf16)
    p_lo = (p_sum - p_hi.astype(jnp.float32)).astype(bf16)
    imp = _dot(p_hi, c2s_ref[...]) + _dot(p_lo, c2s_ref[...])
    jj = lax.broadcasted_iota(jnp.int32, (LANE, LANE), 1)
    qp = i * LANE + lax.broadcasted_iota(jnp.int32, (LANE, LANE), 0)
    cur = qp >> int(math.log2(SLC_BLOCK))
    forced = (jj == 0) | (jj == cur) | (jj == cur - 1)
    val = jnp.where(jj * SLC_BLOCK <= qp, imp + jnp.where(forced, FORCE_BONUS, 0.0), NEG)
    val = jnp.where(jj < n_slc, val, -3e38)
    rank = jnp.zeros((LANE, LANE), jnp.int32)
    for t in range(n_slc):
        col = val[:, t:t + 1]
        ahead = (col > val) | ((col == val) & (jj > t))
        rank = rank + ahead.astype(jnp.int32)
    sel_s[...] = jnp.where(rank < n_top, 1.0, 0.0).astype(bf16)

    def attend(k_ref, v_ref, lo, use_sel, use_win, slot):
        m_s[...] = jnp.full((rows_q, 1), NEG, jnp.float32)
        l_s[...] = jnp.zeros((rows_q, 1), jnp.float32)
        acc_s[...] = jnp.zeros((rows_q, HEAD_DIM), jnp.float32)

        def body(kt, carry):
            off = pl.multiple_of(kt * LANE, LANE)
            k = k_ref[0, pl.ds(off, LANE), :].astype(bf16)
            v = v_ref[0, pl.ds(off, LANE), :].astype(bf16)
            dq = i - kt
            sc = _dot_nt(q_s[...], k) * SCALE + biast_ref[0, jnp.minimum(dq, 2)]
            rr = lax.broadcasted_iota(jnp.int32, (rows_q, LANE), 0) & (LANE - 1)
            cc = lax.broadcasted_iota(jnp.int32, (rows_q, LANE), 1)
            dist = rr - cc + dq * LANE
            msk = dist >= 0
            if use_win:
                msk = msk & (dist <= WINDOW)
            if use_sel:
                se = _dot(sel_s[...], expand_ref[kt])
                msk = msk & (jnp.concatenate([se] * NSA_HPG, axis=0) > 0.5)
            sc = jnp.where(msk, sc, NEG)
            m_old = m_s[...]
            m_new = jnp.maximum(m_old, jnp.max(sc, -1, keepdims=True))
            pe = jnp.where(msk, jnp.exp(sc - m_new), 0.0)
            alpha = jnp.exp(m_old - m_new)
            l_s[...] = alpha * l_s[...] + jnp.sum(pe, -1, keepdims=True)
            acc_s[...] = alpha * acc_s[...] + _dot(pe.astype(bf16), v)
            m_s[...] = m_new
            return carry

        lax.fori_loop(lo, i + 1, body, 0)
        out_s[slot] = acc_s[...] / jnp.maximum(l_s[...], 1e-30)

    attend(ks_ref, vs_ref, 0, True, False, 1)
    attend(kw_ref, vw_ref, jnp.maximum(i - WINDOW // LANE, 0), False, True, 2)

    gt = gate_ref[0]
    for h in range(NSA_HPG):
        sl = slice(h * LANE, (h + 1) * LANE)
        o_ref[0, :, sl] = (gt[:, 3 * h:3 * h + 1] * out_s[0, sl, :]
                           + gt[:, 3 * h + 1:3 * h + 2] * out_s[1, sl, :]
                           + gt[:, 3 * h + 2:3 * h + 3] * out_s[2, sl, :])


def _t5_bucket_np(d):
    max_exact = N_BUCKETS // 2
    d = np.maximum(d, 0)
    large = max_exact + (np.log(np.maximum(d, 1).astype(np.float32) / np.float32(max_exact))
                         / np.float32(math.log(MAX_DISTANCE / max_exact)) * (N_BUCKETS - max_exact)).astype(np.int32)
    return np.where(d < max_exact, d, np.minimum(large, N_BUCKETS - 1)).astype(np.int32)


def _nsa_prompt(yb3, ys3, kcvc, bias_table):
    b, t, _ = yb3.shape
    n_t = t // LANE
    n_slc = t // SLC_BLOCK
    n_cmp = (t - CMP_BLOCK) // CMP_STRIDE + 1
    n_top = min(N_SELECT, n_slc)
    rows_q = NSA_HPG * LANE
    assert t % LANE == 0 and n_cmp <= LANE and n_slc <= LANE

    table = bias_table.astype(jnp.float32)
    dist_c = np.arange(t)[:, None] - (np.arange(LANE) * CMP_STRIDE + CMP_BLOCK - 1)[None, :]
    bias_c = jnp.moveaxis(table[_t5_bucket_np(dist_c)], -1, 0)
    rc = np.arange(LANE)[:, None] - np.arange(LANE)[None, :]
    buckets_t = np.stack([_t5_bucket_np(rc), _t5_bucket_np(rc + LANE), _t5_bucket_np(rc + 2 * LANE)])
    assert (_t5_bucket_np(np.arange(LANE + 1, 4 * LANE)) == N_BUCKETS - 1).all()
    bias_t = jnp.moveaxis(table[buckets_t], -1, 0)
    bias_t = bias_t.reshape(NSA_KV_GROUPS, NSA_HPG, 3, LANE, LANE).transpose(0, 2, 1, 3, 4)
    bias_t = bias_t.reshape(NSA_KV_GROUPS, 3, rows_q, LANE)

    c0 = np.arange(n_cmp) * CMP_STRIDE
    s0 = np.arange(n_slc) * SLC_BLOCK
    ov = np.minimum(c0[:, None] + CMP_BLOCK, s0[None, :] + SLC_BLOCK) - np.maximum(c0[:, None], s0[None, :])
    c2s = np.zeros((LANE, LANE), np.float32)
    c2s[:n_cmp, :n_slc] = np.maximum(ov, 0) / CMP_STRIDE
    expand = np.zeros((n_t, LANE, LANE), np.float32)
    for kt in range(n_t):
        tok_blk = (kt * LANE + np.arange(LANE)) // SLC_BLOCK
        expand[kt, tok_blk, np.arange(LANE)] = 1.0

    kv_spec = lambda cb: pl.BlockSpec((1, t, LANE), lambda bi, g, i: (bi, 0, cb + g))
    g_n = NSA_KV_GROUPS
    return pl.pallas_call(
        functools.partial(_nsa_kernel, n_slc=n_slc, n_top=n_top),
        grid=(b, g_n, n_t),
        in_specs=[
            pl.BlockSpec((1, LANE, rows_q), lambda bi, g, i: (bi, i, g)),
            pl.BlockSpec((1, 1, t // CMP_STRIDE, HEAD_DIM), lambda bi, g, i: (bi, g, 0, 0)),
            pl.BlockSpec((1, 1, t // CMP_STRIDE, HEAD_DIM), lambda bi, g, i: (bi, g_n + g, 0, 0)),
            kv_spec(CB_KVN + 2 * g_n), kv_spec(CB_KVN + 3 * g_n),
            kv_spec(CB_KVN + 4 * g_n), kv_spec(CB_KVN + 5 * g_n),
            pl.BlockSpec((1, LANE, LANE), lambda bi, g, i: (bi, i, g)),
            pl.BlockSpec((NSA_HPG, LANE, LANE), lambda bi, g, i: (g, i, 0)),
            pl.BlockSpec((1, 3, rows_q, LANE), lambda bi, g, i: (g, 0, 0, 0)),
            pl.BlockSpec((LANE, LANE), lambda bi, g, i: (0, 0)),
            pl.BlockSpec((n_t, LANE, LANE), lambda bi, g, i: (0, 0, 0)),
        ],
        out_specs=pl.BlockSpec((1, LANE, rows_q), lambda bi, g, i: (bi, i, g)),
        out_shape=jax.ShapeDtypeStruct((b, t, NSA_Q), jnp.float32),
        scratch_shapes=[pltpu.VMEM((rows_q, HEAD_DIM), jnp.bfloat16),
                        pltpu.VMEM((LANE, LANE), jnp.bfloat16),
                        pltpu.VMEM((rows_q, 1), jnp.float32),
                        pltpu.VMEM((rows_q, 1), jnp.float32),
                        pltpu.VMEM((rows_q, HEAD_DIM), jnp.float32),
                        pltpu.VMEM((3, rows_q, HEAD_DIM), jnp.float32)],
        compiler_params=pltpu.CompilerParams(
            dimension_semantics=("parallel", "parallel", "arbitrary"), vmem_limit_bytes=VMEM_LIMIT),
        name="nsa_prompt",
    )(yb3, kcvc, kcvc, yb3, yb3, yb3, yb3, ys3, bias_c, bias_t,
      jnp.asarray(c2s, jnp.bfloat16), jnp.asarray(expand, jnp.bfloat16))


FOX_TILE = 256


def _fox_kernel(q_ref, k_ref, v_ref, cq_ref, ck_ref, o_ref, m_s, l_s, acc_s):
    i = pl.program_id(2)
    tq = FOX_TILE
    bf16 = jnp.bfloat16
    q = q_ref[0].astype(bf16)
    cq = cq_ref[0, 0]
    m_s[...] = jnp.full((tq, 1), NEG, jnp.float32)
    l_s[...] = jnp.zeros((tq, 1), jnp.float32)
    acc_s[...] = jnp.zeros((tq, HEAD_DIM), jnp.float32)

    def body(kt, carry):
        off = pl.multiple_of(kt * tq, tq)
        k = k_ref[0, pl.ds(off, tq), :].astype(bf16)
        v = v_ref[0, pl.ds(off, tq), :].astype(bf16)
        sc = _dot_nt(q, k) * SCALE + cq - ck_ref[0, 0, kt]
        rr = lax.broadcasted_iota(jnp.int32, (tq, tq), 0)
        cc = lax.broadcasted_iota(jnp.int32, (tq, tq), 1)
        msk = cc + (kt - i) * tq <= rr
        sc = jnp.where(msk, sc, NEG)
        m_old = m_s[...]
        m_new = jnp.maximum(m_old, jnp.max(sc, -1, keepdims=True))
        pe = jnp.where(msk, jnp.exp(sc - m_new), 0.0)
        alpha = jnp.exp(m_old - m_new)
        l_s[...] = alpha * l_s[...] + jnp.sum(pe, -1, keepdims=True)
        acc_s[...] = alpha * acc_s[...] + _dot(pe.astype(bf16), v)
        m_s[...] = m_new
        return carry

    lax.fori_loop(0, i + 1, body, 0)
    o_ref[0] = acc_s[...] / jnp.maximum(l_s[...], 1e-30)


def _fox_prompt(yb3, logf):
    b, t, _ = yb3.shape
    tq = FOX_TILE
    n_t = t // tq
    cum = jnp.moveaxis(jnp.cumsum(logf.astype(jnp.float32), axis=1), 1, 2)
    cum_q = cum[..., None]
    cum_k = cum.reshape(b, FOX_HEADS, n_t, 1, tq)
    return pl.pallas_call(
        _fox_kernel,
        grid=(b, FOX_HEADS, n_t),
        in_specs=[pl.BlockSpec((1, tq, LANE), lambda bi, h, i: (bi, i, CB_QF + h)),
                  pl.BlockSpec((1, t, LANE), lambda bi, h, i: (bi, 0, CB_KF + h)),
                  pl.BlockSpec((1, t, LANE), lambda bi, h, i: (bi, 0, CB_VF + h)),
                  pl.BlockSpec((1, 1, tq, 1), lambda bi, h, i: (bi, h, i, 0)),
                  pl.BlockSpec((1, 1, n_t, 1, tq), lambda bi, h, i: (bi, h, 0, 0, 0))],
        out_specs=pl.BlockSpec((1, tq, LANE), lambda bi, h, i: (bi, i, h)),
        out_shape=jax.ShapeDtypeStruct((b, t, FOX_W), jnp.float32),
        scratch_shapes=[pltpu.VMEM((tq, 1), jnp.float32),
                        pltpu.VMEM((tq, 1), jnp.float32),
                        pltpu.VMEM((tq, HEAD_DIM), jnp.float32)],
        compiler_params=pltpu.CompilerParams(
            dimension_semantics=("parallel", "parallel", "arbitrary"), vmem_limit_bytes=VMEM_LIMIT),
        name="fox_prompt",
    )(yb3, yb3, yb3, cum_q, cum_k)


def _ln(z, g, b):
    mu = jnp.mean(z, -1, keepdims=True)
    zc = z - mu
    var = jnp.mean(zc * zc, -1, keepdims=True)
    return zc * lax.rsqrt(var + LN_EPS) * g + b


def _post_kernel(on_ref, of_ref, x_ref, gn_ref, gf_ref, w_ref, lg_ref, lb_ref, h_ref):
    def rms(o, g):
        return (o * lax.rsqrt(jnp.mean(o * o, -1, keepdims=True) + LN_EPS) * g).astype(jnp.bfloat16)

    mix = (_dot(rms(on_ref[...], gn_ref[...]), w_ref[:NSA_Q, :])
           + _dot(rms(of_ref[...], gf_ref[...]), w_ref[NSA_Q:, :]))
    h_ref[...] = _ln(ALPHA * x_ref[...] + mix, lg_ref[...], lb_ref[...])


def _post_attention(o_n, o_f, x, g_nsa, g_fox, w_out, ln_g, ln_b, tm=256):
    m, d = x.shape
    tm = min(tm, m)
    row = lambda n: pl.BlockSpec((1, n), lambda i: (0, 0))
    return pl.pallas_call(
        _post_kernel,
        grid=(m // tm,),
        in_specs=[pl.BlockSpec((tm, NSA_Q), lambda i: (i, 0)),
                  pl.BlockSpec((tm, FOX_W), lambda i: (i, 0)),
                  pl.BlockSpec((tm, d), lambda i: (i, 0)),
                  row(NSA_Q), row(FOX_W),
                  pl.BlockSpec((NSA_Q + FOX_W, d), lambda i: (0, 0)),
                  row(d), row(d)],
        out_specs=pl.BlockSpec((tm, d), lambda i: (i, 0)),
        out_shape=jax.ShapeDtypeStruct((m, d), jnp.float32),
        compiler_params=pltpu.CompilerParams(
            dimension_semantics=("parallel",), vmem_limit_bytes=VMEM_LIMIT),
        name="post_attention",
    )(o_n, o_f, x, g_nsa[None], g_fox[None], w_out.astype(jnp.bfloat16), ln_g[None], ln_b[None])


def _add_ln_kernel(h_ref, f_ref, g_ref, b_ref, o_ref):
    o_ref[...] = _ln(ALPHA * h_ref[...] + f_ref[...], g_ref[...], b_ref[...])


def _add_ln(h, f, ln_g, ln_b, tm=512):
    m, d = h.shape
    tm = min(tm, m)
    return pl.pallas_call(
        _add_ln_kernel,
        grid=(m // tm,),
        in_specs=[pl.BlockSpec((tm, d), lambda i: (i, 0)),
                  pl.BlockSpec((tm, d), lambda i: (i, 0)),
                  pl.BlockSpec((1, d), lambda i: (0, 0)),
                  pl.BlockSpec((1, d), lambda i: (0, 0))],
        out_specs=pl.BlockSpec((tm, d), lambda i: (i, 0)),
        out_shape=jax.ShapeDtypeStruct((m, d), jnp.float32),
        compiler_params=pltpu.CompilerParams(
            dimension_semantics=("parallel",), vmem_limit_bytes=VMEM_LIMIT),
        name="add_layer_norm",
    )(h, f, ln_g[None], ln_b[None])


PEER_TILE = 128
N_ROUTES = PEER_HEADS * PEER_TOPK


def _top_rows(vals, row_id, n_out, payload=None):
    big = float(vals.shape[0])
    out_v, out_i = [], []
    for _ in range(n_out):
        m = jnp.max(vals, axis=0, keepdims=True)
        win = jnp.min(jnp.where(vals == m, row_id, big), axis=0, keepdims=True)
        hit = row_id == win
        out_v.append(m)
        if payload is None:
            out_i.append(win)
        else:
            out_i.append(jnp.sum(jnp.where(hit, payload, 0.0), axis=0, keepdims=True))
        vals = jnp.where(hit, -jnp.inf, vals)
    return jnp.concatenate(out_v, axis=0), jnp.concatenate(out_i, axis=0)


def _peer_route_kernel(h_ref, wq_ref, keys_ref, g_ref, e_ref, sv_s, si_s):
    bf16 = jnp.bfloat16
    tm = PEER_TILE
    half = PEER_QDIM // 2
    q = _dot(h_ref[...].astype(bf16), wq_ref[...]).astype(bf16)
    key_id = lax.broadcasted_iota(jnp.int32, (N_KEYS, tm), 0).astype(jnp.float32)
    for hp in range(2 * PEER_HEADS):
        s_t = _dot_nt(keys_ref[hp], q[:, hp * half:(hp + 1) * half])
        sv, si = _top_rows(s_t, key_id, PEER_TOPK)
        sv_s[hp] = sv
        si_s[hp] = si
    pair_id = lax.broadcasted_iota(jnp.int32, (PEER_TOPK * PEER_TOPK, tm), 0).astype(jnp.float32)
    for h in range(PEER_HEADS):
        sv0, sv1 = sv_s[2 * h], sv_s[2 * h + 1]
        si0, si1 = si_s[2 * h], si_s[2 * h + 1]
        cand = jnp.concatenate([sv0[a:a + 1, :] + sv1 for a in range(PEER_TOPK)], axis=0)
        expert = jnp.concatenate([si0[a:a + 1, :] * float(N_KEYS) + si1 for a in range(PEER_TOPK)], axis=0)
        best, eid = _top_rows(cand, pair_id, PEER_TOPK, payload=expert)
        ex = jnp.exp(best - best[0:1, :])
        g_ref[0, h * PEER_TOPK:(h + 1) * PEER_TOPK, :] = ex / jnp.sum(ex, axis=0, keepdims=True)
        e_ref[0, h * PEER_TOPK:(h + 1) * PEER_TOPK, :] = eid.astype(jnp.int32)


def _peer_route(h, w_q, sub_keys):
    n, d = h.shape
    tm = PEER_TILE
    nb = n // tm
    n_hp = 2 * PEER_HEADS
    half = PEER_QDIM // 2
    out = jax.ShapeDtypeStruct((nb, N_ROUTES, tm), jnp.float32)
    return pl.pallas_call(
        _peer_route_kernel,
        grid=(nb,),
        in_specs=[pl.BlockSpec((tm, d), lambda i: (i, 0)),
                  pl.BlockSpec((d, PEER_HEADS * PEER_QDIM), lambda i: (0, 0)),
                  pl.BlockSpec((n_hp, N_KEYS, half), lambda i: (0, 0, 0))],
        out_specs=[pl.BlockSpec((1, N_ROUTES, tm), lambda i: (i, 0, 0)),
                   pl.BlockSpec((1, N_ROUTES, tm), lambda i: (i, 0, 0))],
        out_shape=[out, jax.ShapeDtypeStruct((nb, N_ROUTES, tm), jnp.int32)],
        scratch_shapes=[pltpu.VMEM((n_hp, PEER_TOPK, tm), jnp.float32),
                        pltpu.VMEM((n_hp, PEER_TOPK, tm), jnp.float32)],
        compiler_params=pltpu.CompilerParams(
            dimension_semantics=("parallel",), vmem_limit_bytes=VMEM_LIMIT),
        name="peer_route",
    )(h, w_q.astype(jnp.bfloat16), sub_keys.reshape(n_hp, N_KEYS, half).astype(jnp.bfloat16))


PACK_DTYPE = jnp.dtype("bfloat16")
PEER_SLOTS = 4
PEER_AHEAD = PEER_SLOTS - 1


def _pack_expert_tables(u, v):
    def pack(t):
        bits = lax.bitcast_convert_type(t.astype(PACK_DTYPE), jnp.uint16).astype(jnp.uint32)
        half = t.shape[1] // 2
        return bits[:, :half] | (bits[:, half:] << 16)
    return jnp.concatenate([pack(u), pack(v)], axis=1)


def _unpack_words(w):
    lo = lax.bitcast_convert_type(w << 16, jnp.float32)
    hi = lax.bitcast_convert_type(w & jnp.uint32(0xFFFF0000), jnp.float32)
    return lo, hi


def _peer_expert_kernel(e_ref, g_ref, x_ref, uv_hbm, o_ref, e_smem, buf, sem, esem):
    tm = PEER_TILE
    hw = D_MODEL // 2
    n_chunk = hw // LANE
    per_point = N_ROUTES // (2 * n_chunk)
    ids = pltpu.make_async_copy(e_ref.at[0], e_smem, esem)
    ids.start()
    ids.wait()

    def issue(t, slot, ks):
        for k in ks:
            pltpu.make_async_copy(uv_hbm.at[pl.ds(e_smem[k, t], 1), :],
                                  buf.at[slot, pl.ds(k, 1), :], sem.at[slot]).start()

    def wait_rows(slot):
        pltpu.make_async_copy(uv_hbm.at[pl.ds(0, N_ROUTES), :], buf.at[slot], sem.at[slot]).wait()

    def compute(t, slot, t_ahead):
        other = (slot + PEER_AHEAD) % PEER_SLOTS
        point = [0]

        def issue_some():
            issue(t_ahead, other, range(point[0] * per_point, (point[0] + 1) * per_point))
            point[0] += 1

        x_row = x_ref[pl.ds(t, 1), :]
        acc = None
        for c in range(n_chunk):
            lo, hi = _unpack_words(buf[slot, :, c * LANE:(c + 1) * LANE])
            term = lo * x_row[:, c * LANE:(c + 1) * LANE] + hi * x_row[:, hw + c * LANE:hw + (c + 1) * LANE]
            acc = term if acc is None else acc + term
            issue_some()
        s = jnp.sum(acc, axis=1, keepdims=True)
        gate = pltpu.roll(g_ref[0], jnp.where(t == 0, 0, tm - t), 1)[:, 0:1]
        coef = gate * _gelu_tanh(s)
        lo_out, hi_out = [], []
        for c in range(n_chunk):
            lo, hi = _unpack_words(buf[slot, :, hw + c * LANE:hw + (c + 1) * LANE])
            lo_out.append(jnp.sum(lo * coef, axis=0, keepdims=True))
            hi_out.append(jnp.sum(hi * coef, axis=0, keepdims=True))
            issue_some()
        o_ref[pl.ds(t, 1), :] = jnp.concatenate(lo_out + hi_out, axis=1)

    for j in range(PEER_AHEAD):
        issue(j, j, range(N_ROUTES))

    def body(i, carry):
        for j in range(PEER_SLOTS):
            t = PEER_SLOTS * i + j
            wait_rows(j)
            compute(t, j, jnp.minimum(t + PEER_AHEAD, tm - 1))
        return carry

    lax.fori_loop(0, tm // PEER_SLOTS, body, 0)
    for j in range(PEER_AHEAD):
        wait_rows((tm + j) % PEER_SLOTS)


def _peer_experts(h, gates, experts, uv):
    n, d = h.shape
    tm = PEER_TILE
    nb = n // tm
    return pl.pallas_call(
        _peer_expert_kernel,
        grid=(nb,),
        in_specs=[pl.BlockSpec((1, N_ROUTES, tm), lambda i: (i, 0, 0)),
                  pl.BlockSpec((1, N_ROUTES, tm), lambda i: (i, 0, 0)),
                  pl.BlockSpec((tm, d), lambda i: (i, 0)),
                  pl.BlockSpec(memory_space=pl.ANY)],
        out_specs=pl.BlockSpec((tm, d), lambda i: (i, 0)),
        out_shape=jax.ShapeDtypeStruct((n, d), jnp.float32),
        scratch_shapes=[pltpu.SMEM((N_ROUTES, tm), jnp.int32),
                        pltpu.VMEM((PEER_SLOTS, N_ROUTES, d), jnp.uint32),
                        pltpu.SemaphoreType.DMA((PEER_SLOTS,)),
                        pltpu.SemaphoreType.DMA],
        compiler_params=pltpu.CompilerParams(
            dimension_semantics=("arbitrary",), vmem_limit_bytes=VMEM_LIMIT),
        name="peer_experts",
    )(experts, gates, h, uv)


def _peer(h, w_q, sub_keys, uv):
    gates, experts = _peer_route(h, w_q, sub_keys)
    return _peer_experts(h, gates, experts, uv)


QSLOT = 8
DEC_GROUP = 4


def _nsa_decode_kernel(pt_ref, q_ref, ks_new_ref, vs_new_ref, kw_new_ref, vw_new_ref, kwin_ref, vwin_ref,
                       gate_ref, biasc_ref, biast_ref, c2s_ref, expand_ref, pe_ref, w1_ref, w2_ref, pool_hbm,
                       o_ref, kbuf, wbuf, kc_s, q_s, sel_s, m_s, l_s, acc_s, out_s, sem,
                       *, n_pages, page, tq, n_slc, n_top):
    b = pl.program_id(0)
    g = pl.program_id(1)
    bf16 = jnp.bfloat16
    rows_q = NSA_HPG * QSLOT
    p0 = n_pages * page
    n_half = p0 // CMP_STRIDE
    i_slc = p0 // LANE
    i_win = WINDOW // LANE

    def page_copy(p, c, gg):
        col = (2 * c + gg) * LANE
        return pltpu.make_async_copy(pool_hbm.at[pt_ref[b, p], :, pl.ds(col, LANE)],
                                     kbuf.at[c, pl.ds(p * page, page), :], sem.at[c])

    def for_group(fn):
        for gg in range(NSA_KV_GROUPS):
            @pl.when(g == gg)
            def _():
                fn(gg)

    def start_pages(gg):
        for c in range(4):
            for p in range(n_pages):
                page_copy(p, c, gg).start()

    def wait_pages(cs):
        def fn(gg):
            for c in cs:
                for p in range(n_pages):
                    page_copy(p, c, gg).wait()
        return fn

    for_group(start_pages)

    q_s[...] = jnp.zeros((rows_q, HEAD_DIM), jnp.float32)
    qt = q_ref[0]
    for h in range(NSA_HPG):
        q_s[h * QSLOT:h * QSLOT + tq, :] = qt[:, h * LANE:(h + 1) * LANE]
    zeros_tile = jnp.zeros((LANE, HEAD_DIM), jnp.float32)
    wbuf[0, 0:WINDOW, :] = kwin_ref[0]
    wbuf[1, 0:WINDOW, :] = vwin_ref[0]
    wbuf[0, WINDOW:WINDOW + LANE, :] = zeros_tile
    wbuf[1, WINDOW:WINDOW + LANE, :] = zeros_tile
    wbuf[0, WINDOW:WINDOW + tq, :] = kw_new_ref[0]
    wbuf[1, WINDOW:WINDOW + tq, :] = vw_new_ref[0]

    for_group(wait_pages((0, 1)))
    for kv in range(2):
        def half(s0):
            acc = jnp.zeros((n_half, HEAD_DIM), jnp.float32)
            for s in range(CMP_STRIDE):
                rows = kbuf[kv, pl.ds(s, n_half, stride=CMP_STRIDE), :] + pe_ref[kv, s0 + s:s0 + s + 1, :]
                acc = acc + _dot(rows.astype(bf16), w1_ref[kv, s0 + s])
            return acc

        first = half(0)
        second = half(CMP_STRIDE)
        hmid = first + pltpu.roll(second, n_half - 1, 0)
        kc_s[kv] = _dot(_gelu_tanh(hmid).astype(bf16), w2_ref[kv])

    qb = q_s[...].astype(bf16)
    r = lax.broadcasted_iota(jnp.int32, (rows_q, n_half), 0) & (QSLOT - 1)
    c = lax.broadcasted_iota(jnp.int32, (rows_q, n_half), 1)
    s = _dot_nt(qb, kc_s[0].astype(bf16)) * SCALE + biasc_ref[...].reshape(rows_q, n_half)
    mask = c * CMP_STRIDE + (CMP_BLOCK - 1) <= p0 + r
    s = jnp.where(mask, s, NEG)
    p = jnp.where(mask, jnp.exp(s - jnp.max(s, -1, keepdims=True)), 0.0)
    p = p / jnp.maximum(jnp.sum(p, -1, keepdims=True), 1e-30)
    out_s[0] = _dot(p.astype(bf16), kc_s[1].astype(bf16))

    p_sum = p[0:QSLOT]
    for h in range(1, NSA_HPG):
        p_sum = p_sum + p[h * QSLOT:(h + 1) * QSLOT]
    p_hi = p_sum.astype(bf16)
    p_lo = (p_sum - p_hi.astype(jnp.float32)).astype(bf16)
    imp = _dot(p_hi, c2s_ref[...]) + _dot(p_lo, c2s_ref[...])
    wide = 2 * LANE
    jj = lax.broadcasted_iota(jnp.int32, (QSLOT, wide), 1)
    qp = p0 + lax.broadcasted_iota(jnp.int32, (QSLOT, wide), 0)
    cur = qp >> int(math.log2(SLC_BLOCK))
    forced = (jj == 0) | (jj == cur) | (jj == cur - 1)
    val = jnp.where(jj * SLC_BLOCK <= qp, imp + jnp.where(forced, FORCE_BONUS, 0.0), NEG)
    val = jnp.where(jj < n_slc, val, -3e38)
    rank = jnp.zeros((QSLOT, wide), jnp.int32)
    for t in range(n_slc):
        col = val[:, t:t + 1]
        ahead = (col > val) | ((col == val) & (jj > t))
        rank = rank + ahead.astype(jnp.int32)
    sel_s[...] = jnp.where(rank < n_top, 1.0, 0.0)[:, :LANE]

    def attend(buf, kslot, i_tile, use_sel, use_win, slot):
        m_s[...] = jnp.full((rows_q, 1), NEG, jnp.float32)
        l_s[...] = jnp.zeros((rows_q, 1), jnp.float32)
        acc_s[...] = jnp.zeros((rows_q, HEAD_DIM), jnp.float32)

        def tiles(k0, w, with_sel):
            off = pl.multiple_of(k0 * LANE, LANE)
            k = buf[kslot, pl.ds(off, w * LANE), :].astype(bf16)
            v = buf[kslot + 1, pl.ds(off, w * LANE), :].astype(bf16)
            bias = [biast_ref[0, jnp.minimum(i_tile - k0 - j, 2)] for j in range(w)]
            sc = _dot_nt(qb, k) * SCALE + (bias[0] if w == 1 else jnp.concatenate(bias, axis=1))
            rr = lax.broadcasted_iota(jnp.int32, (rows_q, w * LANE), 0) & (QSLOT - 1)
            cc = lax.broadcasted_iota(jnp.int32, (rows_q, w * LANE), 1)
            dist = rr - cc + (i_tile - k0) * LANE
            msk = dist >= 0
            if use_win:
                msk = msk & (dist <= WINDOW)
            if with_sel:
                sb = sel_s[...].astype(bf16)
                se = jnp.concatenate([_dot(sb, expand_ref[k0 + j]) for j in range(w)], axis=1)
                msk = msk & (jnp.concatenate([se] * NSA_HPG, axis=0) > 0.5)
            sc = jnp.where(msk, sc, NEG)
            m_old = m_s[...]
            m_new = jnp.maximum(m_old, jnp.max(sc, -1, keepdims=True))
            pe = jnp.where(msk, jnp.exp(sc - m_new), 0.0)
            alpha = jnp.exp(m_old - m_new)
            l_s[...] = alpha * l_s[...] + jnp.sum(pe, -1, keepdims=True)
            acc_s[...] = alpha * acc_s[...] + _dot(pe.astype(bf16), v)
            m_s[...] = m_new

        def body(grp, carry):
            tiles(grp * DEC_GROUP, DEC_GROUP, use_sel)
            return carry

        lax.fori_loop(0, i_tile // DEC_GROUP, body, 0)
        tiles(i_tile, 1, False)
        out_s[slot] = acc_s[...] / jnp.maximum(l_s[...], 1e-30)

    for_group(wait_pages((2, 3)))
    kbuf[2, p0:p0 + LANE, :] = zeros_tile
    kbuf[3, p0:p0 + LANE, :] = zeros_tile
    kbuf[2, p0:p0 + tq, :] = ks_new_ref[0]
    kbuf[3, p0:p0 + tq, :] = vs_new_ref[0]
    attend(kbuf, 2, i_slc, True, False, 1)
    attend(wbuf, 0, i_win, False, True, 2)

    gt = gate_ref[0]
    for h in range(NSA_HPG):
        sl = slice(h * QSLOT, h * QSLOT + tq)
        o_ref[0, :, h * LANE:(h + 1) * LANE] = (gt[:, 3 * h:3 * h + 1] * out_s[0, sl, :]
                                                 + gt[:, 3 * h + 1:3 * h + 2] * out_s[1, sl, :]
                                                 + gt[:, 3 * h + 2:3 * h + 3] * out_s[2, sl, :])


def _nsa_decode(yb3, ys3, pool, win_cache, page_table, bias_table, cmp_pos, cmp_w1, cmp_w2):
    b, tq, _ = yb3.shape
    n_pages = page_table.shape[1]
    page = pool.shape[1]
    p0 = n_pages * page
    tk = p0 + tq
    n_slc = -(-tk // SLC_BLOCK)
    n_cmp = (tk - CMP_BLOCK) // CMP_STRIDE + 1
    n_half = p0 // CMP_STRIDE
    n_top = min(N_SELECT, n_slc)
    rows_q = NSA_HPG * QSLOT
    g_n = NSA_KV_GROUPS
    wide = 2 * LANE
    assert p0 % LANE == 0 and tq <= QSLOT and tq <= SLC_BLOCK and p0 % SLC_BLOCK == 0
    assert n_cmp + 1 == n_half and n_slc <= wide and (n_slc - 1) * SLC_BLOCK == p0
    assert (p0 // LANE) % DEC_GROUP == 0 and (WINDOW // LANE) % DEC_GROUP == 0
    assert win_cache.shape[1] == WINDOW

    table = bias_table.astype(jnp.float32)
    dist_c = p0 + np.arange(QSLOT)[:, None] - (np.arange(n_half) * CMP_STRIDE + CMP_BLOCK - 1)[None, :]
    bias_c = jnp.moveaxis(table[_t5_bucket_np(dist_c)], -1, 0)
    rc = np.arange(QSLOT)[:, None] - np.arange(LANE)[None, :]
    buckets_t = np.stack([_t5_bucket_np(rc), _t5_bucket_np(rc + LANE), _t5_bucket_np(rc + 2 * LANE)])
    bias_t = jnp.moveaxis(table[buckets_t], -1, 0)
    bias_t = bias_t.reshape(g_n, NSA_HPG, 3, QSLOT, LANE).transpose(0, 2, 1, 3, 4).reshape(g_n, 3, rows_q, LANE)

    c0 = np.arange(n_cmp) * CMP_STRIDE
    s0 = np.arange(n_slc) * SLC_BLOCK
    ov = np.minimum(c0[:, None] + CMP_BLOCK, s0[None, :] + SLC_BLOCK) - np.maximum(c0[:, None], s0[None, :])
    c2s = np.zeros((n_half, wide), np.float32)
    c2s[:n_cmp, :n_slc] = np.maximum(ov, 0) / CMP_STRIDE
    n_t = p0 // LANE
    expand = np.zeros((n_t, LANE, LANE), np.float32)
    for kt in range(n_t):
        expand[kt, (kt * LANE + np.arange(LANE)) // SLC_BLOCK, np.arange(LANE)] = 1.0

    new_spec = lambda cb: pl.BlockSpec((1, tq, LANE), lambda bi, g, pt: (bi, 0, cb + g))
    full = lambda shape: pl.BlockSpec(shape, lambda bi, g, pt: (0,) * len(shape))
    grid_spec = pltpu.PrefetchScalarGridSpec(
        num_scalar_prefetch=1,
        grid=(b, g_n),
        in_specs=[
            pl.BlockSpec((1, tq, NSA_HPG * LANE), lambda bi, g, pt: (bi, 0, g)),
            new_spec(CB_KVN + 2 * g_n), new_spec(CB_KVN + 3 * g_n),
            new_spec(CB_KVN + 4 * g_n), new_spec(CB_KVN + 5 * g_n),
            pl.BlockSpec((1, WINDOW, LANE), lambda bi, g, pt: (bi, 0, g)),
            pl.BlockSpec((1, WINDOW, LANE), lambda bi, g, pt: (bi, 0, g_n + g)),
            pl.BlockSpec((1, tq, LANE), lambda bi, g, pt: (bi, 0, g)),
            pl.BlockSpec((NSA_HPG, QSLOT, n_half), lambda bi, g, pt: (g, 0, 0)),
            pl.BlockSpec((1, 3, rows_q, LANE), lambda bi, g, pt: (g, 0, 0, 0)),
            full((n_half, wide)), full((n_t, LANE, LANE)),
            full((2, CMP_BLOCK, HEAD_DIM)), full((2, CMP_BLOCK, HEAD_DIM, HEAD_DIM)), full((2, HEAD_DIM, HEAD_DIM)),
            pl.BlockSpec(memory_space=pl.ANY),
        ],
        out_specs=pl.BlockSpec((1, tq, NSA_HPG * LANE), lambda bi, g, pt: (bi, 0, g)),
        scratch_shapes=[pltpu.VMEM((4, p0 + LANE, HEAD_DIM), jnp.float32),
                        pltpu.VMEM((2, WINDOW + LANE, HEAD_DIM), jnp.float32),
                        pltpu.VMEM((2, n_half, HEAD_DIM), jnp.float32),
                        pltpu.VMEM((rows_q, HEAD_DIM), jnp.float32),
                        pltpu.VMEM((QSLOT, LANE), jnp.float32),
                        pltpu.VMEM((rows_q, 1), jnp.float32),
                        pltpu.VMEM((rows_q, 1), jnp.float32),
                        pltpu.VMEM((rows_q, HEAD_DIM), jnp.float32),
                        pltpu.VMEM((3, rows_q, HEAD_DIM), jnp.float32),
                        pltpu.SemaphoreType.DMA((4,))],
    )
    return pl.pallas_call(
        functools.partial(_nsa_decode_kernel, n_pages=n_pages, page=page, tq=tq, n_slc=n_slc, n_top=n_top),
        grid_spec=grid_spec,
        out_shape=jax.ShapeDtypeStruct((b, tq, NSA_Q), jnp.float32),
        compiler_params=pltpu.CompilerParams(
            dimension_semantics=("arbitrary", "arbitrary"), vmem_limit_bytes=VMEM_LIMIT),
        name="nsa_decode",
    )(page_table, yb3, yb3, yb3, yb3, yb3, win_cache, win_cache, ys3, bias_c, bias_t,
      jnp.asarray(c2s, jnp.bfloat16), jnp.asarray(expand, jnp.bfloat16),
      cmp_pos, cmp_w1.astype(jnp.bfloat16), cmp_w2.astype(jnp.bfloat16), pool)


def _fox_decode_kernel(pt_ref, qa_ref, qb_ref, ka_ref, kb_ref, va_ref, vb_ref, cq_ref, ck_ref, kv_ref,
                       o_ref, q_s, new_s, m_s, l_s, acc_s, *, n_pages, tq):
    p = pl.program_id(1)
    bf16 = jnp.bfloat16
    half_h = FOX_HEADS // 2

    rows = FOX_HEADS * QSLOT

    @pl.when(p == 0)
    def _():
        m_s[...] = jnp.full(m_s.shape, NEG, jnp.float32)
        l_s[...] = jnp.zeros(l_s.shape, jnp.float32)
        acc_s[...] = jnp.zeros(acc_s.shape, jnp.float32)
        q_s[...] = jnp.zeros(q_s.shape, jnp.float32)
        new_s[...] = jnp.zeros(new_s.shape, jnp.float32)
        for h in range(FOX_HEADS):
            src_q, src_k, src_v = (qa_ref, ka_ref, va_ref) if h < half_h else (qb_ref, kb_ref, vb_ref)
            lo = (h % half_h) * LANE
            q_s[h * QSLOT:h * QSLOT + tq, :] = src_q[0, :, lo:lo + LANE]
            new_s[0:tq, h * LANE:(h + 1) * LANE] = src_k[0, :, lo:lo + LANE]
            new_s[0:tq, FOX_W + h * LANE:FOX_W + (h + 1) * LANE] = src_v[0, :, lo:lo + LANE]

    def step(kv_at, is_new):
        qb = q_s[...].astype(bf16)
        sc = jnp.concatenate(
            [_dot_nt(qb[h * QSLOT:(h + 1) * QSLOT], kv_at(h * LANE).astype(bf16)) for h in range(FOX_HEADS)], axis=0)
        ck = jnp.concatenate([jnp.broadcast_to(ck_ref[0, h, 0], (QSLOT, LANE)) for h in range(FOX_HEADS)], axis=0)
        sc = sc * SCALE + cq_ref[0].reshape(rows, 1) - ck
        if is_new:
            rr = lax.broadcasted_iota(jnp.int32, (rows, LANE), 0) & (QSLOT - 1)
            cc = lax.broadcasted_iota(jnp.int32, (rows, LANE), 1)
            msk = cc <= rr
            sc = jnp.where(msk, sc, NEG)
        m_old = m_s[...]
        m_new = jnp.maximum(m_old, jnp.max(sc, -1, keepdims=True))
        pe = jnp.exp(sc - m_new)
        if is_new:
            pe = jnp.where(msk, pe, 0.0)
        alpha = jnp.exp(m_old - m_new)
        l_s[...] = alpha * l_s[...] + jnp.sum(pe, -1, keepdims=True)
        pb = pe.astype(bf16)
        pv = jnp.concatenate(
            [_dot(pb[h * QSLOT:(h + 1) * QSLOT], kv_at(FOX_W + h * LANE).astype(bf16)) for h in range(FOX_HEADS)],
            axis=0)
        acc_s[...] = alpha * acc_s[...] + pv
        m_s[...] = m_new

    @pl.when(p < n_pages)
    def _():
        step(lambda col: kv_ref[0, pl.ds(col // LANE, LANE, stride=2 * FOX_HEADS), :], False)

    @pl.when(p == n_pages)
    def _():
        step(lambda col: new_s[:, col:col + LANE], True)
        res = acc_s[...] / jnp.maximum(l_s[...], 1e-30)
        for h in range(FOX_HEADS):
            o_ref[0, :, h * LANE:(h + 1) * LANE] = res[h * QSLOT:h * QSLOT + tq, :]


def _fox_decode(yb3, logf_new, pool, logf_past, page_table):
    b, tq, _ = yb3.shape
    n_pages = page_table.shape[1]
    page = pool.shape[1] // (2 * FOX_HEADS)
    p0 = n_pages * page
    assert page == LANE and tq <= QSLOT
    cum = jnp.cumsum(jnp.concatenate([logf_past.astype(jnp.float32), logf_new], axis=1), axis=1)
    cum_q = jnp.pad(jnp.moveaxis(cum[:, p0:], 1, 2), ((0, 0), (0, 0), (0, QSLOT - tq)))[..., None]
    cum_k = jnp.pad(jnp.moveaxis(cum, 1, 2), ((0, 0), (0, 0), (0, LANE - tq)))
    cum_k = cum_k.reshape(b, FOX_HEADS, n_pages + 1, 1, LANE)
    wq = FOX_W // 2
    blk = lambda col: pl.BlockSpec((1, tq, wq), lambda bi, p, pt: (bi, 0, col))
    base_q, base_k, base_v = CB_QF * LANE // wq, CB_KF * LANE // wq, CB_VF * LANE // wq
    grid_spec = pltpu.PrefetchScalarGridSpec(
        num_scalar_prefetch=1,
        grid=(b, n_pages + 1),
        in_specs=[blk(base_q), blk(base_q + 1), blk(base_k), blk(base_k + 1), blk(base_v), blk(base_v + 1),
                  pl.BlockSpec((1, FOX_HEADS, QSLOT, 1), lambda bi, p, pt: (bi, 0, 0, 0)),
                  pl.BlockSpec((1, FOX_HEADS, 1, 1, LANE), lambda bi, p, pt: (bi, 0, p, 0, 0)),
                  pl.BlockSpec((1, page * 2 * FOX_HEADS, HEAD_DIM),
                               lambda bi, p, pt: (pt[bi, jnp.minimum(p, n_pages - 1)], 0, 0))],
        out_specs=pl.BlockSpec((1, tq, FOX_W), lambda bi, p, pt: (bi, 0, 0)),
        scratch_shapes=[pltpu.VMEM((FOX_HEADS * QSLOT, HEAD_DIM), jnp.float32),
                        pltpu.VMEM((LANE, 2 * FOX_W), jnp.float32),
                        pltpu.VMEM((FOX_HEADS * QSLOT, 1), jnp.float32),
                        pltpu.VMEM((FOX_HEADS * QSLOT, 1), jnp.float32),
                        pltpu.VMEM((FOX_HEADS * QSLOT, HEAD_DIM), jnp.float32)],
    )
    return pl.pallas_call(
        functools.partial(_fox_decode_kernel, n_pages=n_pages, tq=tq),
        grid_spec=grid_spec,
        out_shape=jax.ShapeDtypeStruct((b, tq, FOX_W), jnp.float32),
        compiler_params=pltpu.CompilerParams(
            dimension_semantics=("arbitrary", "arbitrary"), vmem_limit_bytes=VMEM_LIMIT),
        name="fox_decode",
    )(page_table, yb3, yb3, yb3, yb3, yb3, yb3, cum_q, cum_k, pool)


def layer_norm(x, g, b):
    xf = x.astype(jnp.float32)
    mu = jnp.mean(xf, -1, keepdims=True)
    var = jnp.mean(jnp.square(xf - mu), -1, keepdims=True)
    return ((xf - mu) * lax.rsqrt(var + LN_EPS) * g + b).astype(x.dtype)


def rms_norm(x, g):
    xf = x.astype(jnp.float32)
    return (xf * lax.rsqrt(jnp.mean(xf * xf, -1, keepdims=True) + LN_EPS) * g).astype(x.dtype)


def masked_softmax(s, mask):
    s = jnp.where(mask, s, NEG)
    m = jnp.max(s, -1, keepdims=True)
    p = jnp.where(mask, jnp.exp(s - m), 0.0)
    return p / jnp.maximum(jnp.sum(p, -1, keepdims=True), 1e-30)


def t5_bucket(dist):
    max_exact = N_BUCKETS // 2
    d = jnp.maximum(dist, 0)
    large = max_exact + (jnp.log(jnp.maximum(d, 1).astype(jnp.float32) / max_exact)
                         / math.log(MAX_DISTANCE / max_exact) * (N_BUCKETS - max_exact)).astype(jnp.int32)
    return jnp.where(d < max_exact, d, jnp.minimum(large, N_BUCKETS - 1))


def head_bias(table, dist):
    b = jnp.moveaxis(table[t5_bucket(dist)].astype(jnp.float32), -1, 0)
    return b.reshape((NSA_KV_GROUPS, NSA_HPG) + dist.shape)


def query_block(t, cap):
    return t if t <= cap else cap


def to_blocks(x, axis, qb):
    n = x.shape[axis] // qb
    return jnp.moveaxis(x.reshape(x.shape[:axis] + (n, qb) + x.shape[axis + 1:]), axis, 0)


def from_blocks(y):
    y = jnp.moveaxis(y, 0, 1)
    return y.reshape((y.shape[0], y.shape[1] * y.shape[2]) + y.shape[3:])


def gather_pages(pool, page_table):
    g = pool[page_table]
    return g.reshape((g.shape[0], g.shape[1] * g.shape[2]) + g.shape[3:])


def cmp_to_slc(n_cmp, n_slc):
    c0 = np.arange(n_cmp) * CMP_STRIDE
    s0 = np.arange(n_slc) * SLC_BLOCK
    ov = np.minimum(c0[:, None] + CMP_BLOCK, s0[None, :] + SLC_BLOCK) - np.maximum(c0[:, None], s0[None, :])
    return jnp.asarray(np.maximum(ov, 0) / CMP_STRIDE, dtype=jnp.float32)


def project(x, w_in, b_forget):
    B, T, D = x.shape
    offs = [int(o) for o in np.cumsum(IN_SIZES)[:-1]]
    y = _matmul(x.reshape(B * T, D), w_in).reshape(B, T, -1)
    q_n, kv_n, gate_n, q_f, kv_f, f_f = jnp.split(y, offs, axis=-1)
    qn = q_n.reshape(B, T, NSA_HEADS, HEAD_DIM)
    kvn = kv_n.reshape(B, T, 3, 2, NSA_KV_GROUPS, HEAD_DIM)
    gates = jax.nn.sigmoid(gate_n).reshape(B, T, NSA_HEADS, 3)
    qf = q_f.reshape(B, T, FOX_HEADS, HEAD_DIM)
    kvf = kv_f.reshape(B, T, 2, FOX_HEADS, HEAD_DIM)
    logf = jax.nn.log_sigmoid((f_f + b_forget).astype(jnp.float32))
    return qn, kvn, gates, qf, kvf, logf


def compress(k, pos, w1, w2):
    B, T, G, dk = k.shape
    n_cmp = (T - CMP_BLOCK) // CMP_STRIDE + 1
    halves = k[:, :(n_cmp + 1) * CMP_STRIDE].reshape(B, n_cmp + 1, CMP_STRIDE, G, dk)
    pe = pos.reshape(2, CMP_STRIDE, 1, dk)
    w1r = w1.reshape(2, CMP_STRIDE, dk, w1.shape[-1])
    h = (jnp.einsum('bnsgd,sdh->bngh', halves[:, :-1] + pe[0], w1r[0])
         + jnp.einsum('bnsgd,sdh->bngh', halves[:, 1:] + pe[1], w1r[1]))
    return jax.nn.gelu(h) @ w2


def nsa(q, nsa_full, win_ext, gates, bias_table, cmp_pos, cmp_w1, cmp_w2):
    B, Tq = q.shape[:2]
    Tk = nsa_full.shape[1]
    p0 = Tk - Tq
    dt = q.dtype
    scale = HEAD_DIM ** -0.5
    G, HPG = NSA_KV_GROUPS, NSA_HPG
    qg = q.reshape(B, Tq, G, HPG, HEAD_DIM)
    q_pos = p0 + jnp.arange(Tq)

    kc = compress(nsa_full[:, :, 0, 0], cmp_pos[0], cmp_w1[0], cmp_w2[0])
    vc = compress(nsa_full[:, :, 0, 1], cmp_pos[1], cmp_w1[1], cmp_w2[1])
    n_cmp = kc.shape[1]
    dist_c = q_pos[:, None] - (jnp.arange(n_cmp) * CMP_STRIDE + CMP_BLOCK - 1)[None, :]
    s_c = jnp.einsum('bqghd,bngd->bghqn', qg, kc).astype(jnp.float32) * scale + head_bias(bias_table, dist_c)
    p_c = masked_softmax(s_c, dist_c >= 0)
    o_c = jnp.einsum('bghqn,bngd->bqghd', p_c.astype(dt), vc)

    n_slc = -(-Tk // SLC_BLOCK)
    imp = jnp.einsum('bghqn,nj->bgqj', p_c, cmp_to_slc(n_cmp, n_slc))
    blk = jnp.arange(n_slc)[None, :]
    cur = (q_pos // SLC_BLOCK)[:, None]
    forced = (blk == 0) | (blk == cur) | (blk == cur - 1)
    imp = jnp.where(blk * SLC_BLOCK <= q_pos[:, None], imp + FORCE_BONUS * forced, NEG)
    _, sel = lax.top_k(imp, min(N_SELECT, n_slc))
    tok = (sel[..., None] * SLC_BLOCK + jnp.arange(SLC_BLOCK)).reshape(B, G, Tq, -1)
    kv_s = jnp.pad(nsa_full[:, :, 1], ((0, 0), (0, n_slc * SLC_BLOCK - Tk), (0, 0), (0, 0), (0, 0)))
    kv_s = kv_s.transpose(0, 3, 1, 2, 4)
    b_ix = jnp.arange(B)[:, None, None, None]
    g_ix = jnp.arange(G)[None, :, None, None]
    table_g = bias_table.reshape(N_BUCKETS, G, HPG)

    def slc_block(args):
        qc, tc, pc = args
        kv = kv_s[b_ix, g_ix, tc]
        dist = pc[None, None, :, None] - tc
        bias = jnp.moveaxis(table_g[t5_bucket(dist), g_ix].astype(jnp.float32), -1, 2)
        s = jnp.einsum('bqghd,bgqnd->bghqn', qc, kv[..., 0, :]).astype(jnp.float32) * scale + bias
        p = masked_softmax(s, (dist >= 0)[:, :, None])
        return jnp.einsum('bghqn,bgqnd->bqghd', p.astype(dt), kv[..., 1, :])

    qs = query_block(Tq, SLC_Q_BLOCK)
    o_s = from_blocks(lax.map(slc_block, (to_blocks(qg, 1, qs), to_blocks(tok, 2, qs), q_pos.reshape(-1, qs))))

    qw = query_block(Tq, Q_BLOCK)

    def win_block(args):
        c, qc = args
        kv = lax.dynamic_slice_in_dim(win_ext, c * qw, WINDOW + qw, axis=1)
        pos = p0 + c * qw + jnp.arange(qw)
        kpos = p0 - WINDOW + c * qw + jnp.arange(WINDOW + qw)
        dist = pos[:, None] - kpos[None, :]
        mask = (dist >= 0) & (dist <= WINDOW) & (kpos >= 0)[None, :]
        s = jnp.einsum('bqghd,bkgd->bghqk', qc, kv[:, :, 0]).astype(jnp.float32) * scale + head_bias(bias_table, dist)
        p = masked_softmax(s, mask)
        return jnp.einsum('bghqk,bkgd->bqghd', p.astype(dt), kv[:, :, 1])

    o_w = from_blocks(lax.map(win_block, (jnp.arange(Tq // qw), to_blocks(qg, 1, qw))))

    g = gates.reshape(B, Tq, G, HPG, 3)
    o = g[..., 0:1] * o_c + g[..., 1:2] * o_s + g[..., 2:3] * o_w
    return o.reshape(B, Tq, NSA_Q)


def fox(q, kv, logf):
    B, Tq = q.shape[:2]
    Tk = kv.shape[1]
    p0 = Tk - Tq
    dt = q.dtype
    scale = HEAD_DIM ** -0.5
    cum = jnp.cumsum(logf.astype(jnp.float32), axis=1)
    cum_k = jnp.moveaxis(cum, 1, 2)
    k_all, v_all = kv[:, :, 0], kv[:, :, 1]
    kpos = jnp.arange(Tk)

    def blk(args):
        qc, cq, pos = args
        s = jnp.einsum('bqhd,bkhd->bhqk', qc, k_all).astype(jnp.float32) * scale
        s = s + jnp.moveaxis(cq, 1, 2)[..., None] - cum_k[:, :, None, :]
        p = masked_softmax(s, kpos[None, :] <= pos[:, None])
        return jnp.einsum('bhqk,bkhd->bqhd', p.astype(dt), v_all)

    qb = query_block(Tq, Q_BLOCK)
    o = from_blocks(lax.map(blk, (to_blocks(q, 1, qb), to_blocks(cum[:, p0:], 1, qb),
                                  (p0 + jnp.arange(Tq)).reshape(-1, qb))))
    return o.reshape(B, Tq, FOX_W)


def layer_forward(x, past, w_in, b_forget, cmp_pos, cmp_w1, cmp_w2, bias_table, g_nsa, g_fox, w_out,
                  ln1_g, ln1_b, peer_w_q, peer_keys, peer_uv, ln2_g, ln2_b):
    B, T, D = x.shape
    qn, kvn, gates, qf, kvf, logf = project(x, w_in, b_forget)
    nsa_rows, win_rows = kvn[:, :, :2], kvn[:, :, 2]
    nsa_past, win_past, fox_past, logf_past = past
    buf_len = win_past.shape[1]
    nsa_full = jnp.concatenate([nsa_past, nsa_rows], axis=1)
    fox_full = jnp.concatenate([fox_past, kvf], axis=1)
    logf_full = jnp.concatenate([logf_past.astype(jnp.float32), logf], axis=1)
    pad = jnp.zeros((B, WINDOW - buf_len) + win_rows.shape[2:], win_rows.dtype)
    win_ext = jnp.concatenate([pad, win_past, win_rows], axis=1)
    o_n = nsa(qn, nsa_full, win_ext, gates, bias_table, cmp_pos, cmp_w1, cmp_w2)
    o_f = fox(qf, fox_full, logf_full)
    h = _post_attention(o_n.reshape(B * T, -1), o_f.reshape(B * T, -1), x.reshape(B * T, D),
                        g_nsa, g_fox, w_out, ln1_g, ln1_b)
    f = _peer(h, peer_w_q, peer_keys, peer_uv)
    y = _add_ln(h, f, ln2_g, ln2_b).reshape(B, T, D)
    return y, nsa_rows, win_ext[:, win_ext.shape[1] - buf_len:], kvf, logf


def sample_forward(x, caches, page_table, w_proj, cmp_pos, cmp_w1, cmp_w2, bias_table, g_nsa, g_fox, w_out,
                   ln1_g, ln1_b, peer_w_q, peer_keys, peer_uv, ln2_g, ln2_b):
    B, T, D = x.shape
    G = NSA_KV_GROUPS
    cache_nsa, cache_win, cache_fox, cache_logf = caches
    n_pool, page = cache_nsa.shape[:2]
    x2 = x.reshape(B * T, D)
    w_big, w_small, b_small = w_proj
    yb3 = _matmul(x2, w_big).reshape(B, T, BIG_WIDTH)
    ys3 = _proj_small(x2, w_small, b_small).reshape(B, T, SMALL_WIDTH)
    logf = ys3[:, :, 2 * LANE:2 * LANE + FOX_HEADS]
    kvn = yb3[:, :, CB_KVN * LANE:CB_QF * LANE].reshape(B, T, 3, 2, G, HEAD_DIM)
    kvf = yb3[:, :, CB_KF * LANE:].reshape(B, T, 2, FOX_HEADS, HEAD_DIM)

    o_n = _nsa_decode(yb3, ys3, cache_nsa.reshape(n_pool, page, 4 * NSA_KV), cache_win.reshape(B, -1, 2 * NSA_KV),
                      page_table, bias_table, cmp_pos, cmp_w1, cmp_w2)
    logf_past = cache_logf[page_table].reshape(B, -1, FOX_HEADS)
    o_f = _fox_decode(yb3, logf, cache_fox.reshape(n_pool, page * 2 * FOX_HEADS, HEAD_DIM), logf_past, page_table)
    h = _post_attention(o_n.reshape(B * T, NSA_Q), o_f.reshape(B * T, FOX_W), x2,
                        g_nsa, g_fox, w_out, ln1_g, ln1_b)
    f = _peer(h, peer_w_q, peer_keys, peer_uv)
    y = _add_ln(h, f, ln2_g, ln2_b).reshape(B, T, D)
    win_buf = jnp.concatenate([cache_win[:, T:], kvn[:, :, 2]], axis=1)
    return y, kvn[:, :, :2], win_buf, kvf, logf


def prompt_forward(x, w_proj, cmp_pos, cmp_w1, cmp_w2, bias_table, g_nsa, g_fox, w_out,
                   ln1_g, ln1_b, peer_w_q, peer_keys, peer_uv, ln2_g, ln2_b):
    B, T, D = x.shape
    G = NSA_KV_GROUPS
    x2 = x.reshape(B * T, D)
    w_big, w_small, b_small = w_proj
    yb = _matmul(x2, w_big)
    ys = _proj_small(x2, w_small, b_small)
    yb3 = yb.reshape(B, T, BIG_WIDTH)
    ys3 = ys.reshape(B, T, SMALL_WIDTH)
    logf = ys3[:, :, 2 * LANE:2 * LANE + FOX_HEADS]
    kvn = yb3[:, :, CB_KVN * LANE:CB_QF * LANE].reshape(B, T, 3, 2, G, HEAD_DIM)
    kvf = yb3[:, :, CB_KF * LANE:].reshape(B, T, 2, FOX_HEADS, HEAD_DIM)

    kcvc = _compress_prompt(yb3, cmp_pos, cmp_w1, cmp_w2)
    o_n = _nsa_prompt(yb3, ys3, kcvc, bias_table)
    o_f = _fox_prompt(yb3, logf)
    h = _post_attention(o_n.reshape(B * T, NSA_Q), o_f.reshape(B * T, FOX_W), x2,
                        g_nsa, g_fox, w_out, ln1_g, ln1_b)
    f = _peer(h, peer_w_q, peer_keys, peer_uv)
    y = _add_ln(h, f, ln2_g, ln2_b).reshape(B, T, D)
    buf_len = min(WINDOW, T)
    return y, kvn[:, :, :2], kvn[:, T - buf_len:, 2], kvf, logf


def kernel(x_prompt, x_sample, cache_nsa_kv, cache_nsa_win, cache_fox_kv, cache_fox_logf, page_table,
           w_in, b_forget, nsa_cmp_pos, nsa_cmp_w1, nsa_cmp_w2, rel_bias_table, g_nsa, g_fox, w_out,
           ln1_g, ln1_b, peer_w_q, peer_sub_keys, peer_u, peer_v, ln2_g, ln2_b):
    layer = 0
    w = (_permute_w_in(w_in[layer], b_forget[layer]), nsa_cmp_pos[layer], nsa_cmp_w1[layer], nsa_cmp_w2[layer],
         rel_bias_table, g_nsa[layer], g_fox[layer], w_out[layer], ln1_g[layer], ln1_b[layer],
         peer_w_q[layer], peer_sub_keys[layer], _pack_expert_tables(peer_u[layer], peer_v[layer]),
         ln2_g[layer], ln2_b[layer])
    yp, a_nsa, a_win, a_fox, a_logf = prompt_forward(x_prompt, *w)
    caches = (cache_nsa_kv[layer], cache_nsa_win[layer], cache_fox_kv[layer], cache_fox_logf[layer])
    ys, b_nsa, b_win, b_fox, b_logf = sample_forward(x_sample, caches, page_table, *w)
    return (yp, ys, a_nsa[None], a_win[None], a_fox[None], a_logf[None],
            b_nsa[None], b_win[None], b_fox[None], b_logf[None])
```

```python
import functools
import math

import jax
import jax.numpy as jnp
import numpy as np
from jax import lax
from jax.experimental import pallas as pl
from jax.experimental.pallas import tpu as pltpu

D_MODEL = 2048
HEAD_DIM = 128
NSA_HEADS = 8
NSA_KV_GROUPS = 2
NSA_HPG = NSA_HEADS // NSA_KV_GROUPS
CMP_BLOCK = 32
CMP_STRIDE = 16
SLC_BLOCK = 64
N_SELECT = 16
WINDOW = 512
FOX_HEADS = 8
NSA_Q = NSA_HEADS * HEAD_DIM
NSA_KV = NSA_KV_GROUPS * HEAD_DIM
FOX_W = FOX_HEADS * HEAD_DIM
IN_SIZES = (NSA_Q, 6 * NSA_KV, 3 * NSA_HEADS, FOX_W, 2 * FOX_W, FOX_HEADS)
N_BUCKETS = 32
MAX_DISTANCE = 128
PEER_HEADS = 8
N_KEYS = 128
PEER_TOPK = 16
PEER_QDIM = 256
DEPTH = 1
ALPHA = (2.0 * DEPTH) ** 0.25
LN_EPS = 1e-5
NEG = -1e30
FORCE_BONUS = 1e4
SCALE = HEAD_DIM ** -0.5

LANE = 128
VMEM_LIMIT = 48 * 1024 * 1024

BIG_WIDTH = NSA_Q + 6 * NSA_KV + FOX_W + 2 * FOX_W
CB_QN = 0
CB_KVN = NSA_Q // LANE
CB_QF = CB_KVN + 6 * NSA_KV // LANE
CB_KF = CB_QF + FOX_W // LANE
CB_VF = CB_KF + FOX_W // LANE
SMALL_WIDTH = 3 * LANE


def _dot_nt(a, b):
    return lax.dot_general(a, b, (((1,), (1,)), ((), ())), preferred_element_type=jnp.float32)


def _dot(a, b):
    return jnp.dot(a, b, preferred_element_type=jnp.float32)


def _mm_kernel(x_ref, w_ref, o_ref):
    o_ref[...] = _dot(x_ref[...].astype(jnp.bfloat16), w_ref[...])


def _matmul(x, w, tm=512, tn=512):
    m, k = x.shape
    n = w.shape[1]
    tm = min(tm, m)
    n_pad = -(-n // tn) * tn
    wb = w.astype(jnp.bfloat16)
    if n_pad != n:
        wb = jnp.pad(wb, ((0, 0), (0, n_pad - n)))
    out = pl.pallas_call(
        _mm_kernel,
        grid=(m // tm, n_pad // tn),
        in_specs=[pl.BlockSpec((tm, k), lambda i, j: (i, 0)),
                  pl.BlockSpec((k, tn), lambda i, j: (0, j))],
        out_specs=pl.BlockSpec((tm, tn), lambda i, j: (i, j)),
        out_shape=jax.ShapeDtypeStruct((m, n_pad), jnp.float32),
        compiler_params=pltpu.CompilerParams(
            dimension_semantics=("parallel", "arbitrary"),
            vmem_limit_bytes=VMEM_LIMIT),
        name="dense_matmul",
    )(x, wb)
    return out[:, :n] if n_pad != n else out


def _proj_small_kernel(x_ref, w_ref, b_ref, o_ref):
    y = _dot(x_ref[...].astype(jnp.bfloat16), w_ref[...]) + b_ref[...]
    gates = y[:, :2 * LANE]
    o_ref[:, :2 * LANE] = 1.0 / (1.0 + jnp.exp(-gates))
    f = y[:, 2 * LANE:]
    o_ref[:, 2 * LANE:] = -(jnp.maximum(-f, 0.0) + jnp.log1p(jnp.exp(-jnp.abs(f))))


def _proj_small(x, w_small, b_small, tm=512):
    m, k = x.shape
    tm = min(tm, m)
    return pl.pallas_call(
        _proj_small_kernel,
        grid=(m // tm,),
        in_specs=[pl.BlockSpec((tm, k), lambda i: (i, 0)),
                  pl.BlockSpec((k, SMALL_WIDTH), lambda i: (0, 0)),
                  pl.BlockSpec((1, SMALL_WIDTH), lambda i: (0, 0))],
        out_specs=pl.BlockSpec((tm, SMALL_WIDTH), lambda i: (i, 0)),
        out_shape=jax.ShapeDtypeStruct((m, SMALL_WIDTH), jnp.float32),
        compiler_params=pltpu.CompilerParams(
            dimension_semantics=("parallel",), vmem_limit_bytes=VMEM_LIMIT),
        name="proj_small",
    )(x, w_small, b_small)


def _permute_w_in(w_in, b_forget):
    offs = [0] + [int(o) for o in np.cumsum(IN_SIZES)]
    q_n, kv_n, gate, q_f, kv_f, f_f = (w_in[:, offs[i]:offs[i + 1]] for i in range(6))
    w_big = jnp.concatenate([q_n, kv_n, q_f, kv_f], axis=1).astype(jnp.bfloat16)
    d = w_in.shape[0]
    n_gate = 3 * NSA_HPG
    zg = jnp.zeros((d, LANE - n_gate), w_in.dtype)
    zf = jnp.zeros((d, LANE - FOX_HEADS), w_in.dtype)
    w_small = jnp.concatenate([gate[:, :n_gate], zg, gate[:, n_gate:], zg, f_f, zf], axis=1).astype(jnp.bfloat16)
    b_small = jnp.concatenate([jnp.zeros((2 * LANE,), jnp.float32), b_forget.astype(jnp.float32),
                               jnp.zeros((LANE - FOX_HEADS,), jnp.float32)])[None]
    return w_big, w_small, b_small


def _gelu_tanh(h):
    return 0.5 * h * (1.0 + jnp.tanh(math.sqrt(2.0 / math.pi) * (h + 0.044715 * (h * h * h))))


def _compress_kernel(k_ref, pe_ref, w1_ref, w2_ref, o_ref, *, nh):
    def half(s0):
        acc = jnp.zeros((nh, HEAD_DIM), jnp.float32)
        for s in range(CMP_STRIDE):
            rows = k_ref[0, pl.ds(s, nh, stride=CMP_STRIDE), :] + pe_ref[0, s0 + s:s0 + s + 1, :]
            acc = acc + _dot(rows.astype(jnp.bfloat16), w1_ref[0, s0 + s])
        return acc

    first = half(0)
    second = half(CMP_STRIDE)
    h = first + pltpu.roll(second, nh - 1, 0)
    o_ref[0, 0] = _dot(_gelu_tanh(h).astype(jnp.bfloat16), w2_ref[0])


def _compress_prompt(yb3, cmp_pos, cmp_w1, cmp_w2):
    b, t, _ = yb3.shape
    nh = t // CMP_STRIDE
    n_kg = 2 * NSA_KV_GROUPS
    return pl.pallas_call(
        functools.partial(_compress_kernel, nh=nh),
        grid=(b, n_kg),
        in_specs=[pl.BlockSpec((1, t, LANE), lambda i, c: (i, 0, CB_KVN + c)),
                  pl.BlockSpec((1, CMP_BLOCK, HEAD_DIM), lambda i, c: (c // NSA_KV_GROUPS, 0, 0)),
                  pl.BlockSpec((1, CMP_BLOCK, HEAD_DIM, HEAD_DIM), lambda i, c: (c // NSA_KV_GROUPS, 0, 0, 0)),
                  pl.BlockSpec((1, HEAD_DIM, HEAD_DIM), lambda i, c: (c // NSA_KV_GROUPS, 0, 0))],
        out_specs=pl.BlockSpec((1, 1, nh, HEAD_DIM), lambda i, c: (i, c, 0, 0)),
        out_shape=jax.ShapeDtypeStruct((b, n_kg, nh, HEAD_DIM), jnp.float32),
        compiler_params=pltpu.CompilerParams(
            dimension_semantics=("parallel", "arbitrary"), vmem_limit_bytes=VMEM_LIMIT),
        name="nsa_compress",
    )(yb3, cmp_pos, cmp_w1.astype(jnp.bfloat16), cmp_w2.astype(jnp.bfloat16))


SLC_PAIR = 2


def _nsa_kernel(q_ref, kc_ref, vc_ref, ks_ref, vs_ref, kw_ref, vw_ref, gate_ref, biasc_ref, biast_ref,
                c2s_ref, expand_ref, o_ref, q_s, sel_s, m_s, l_s, acc_s, out_s, *, n_slc, n_top):
    i = pl.program_id(2)
    rows_q = NSA_HPG * LANE
    bf16 = jnp.bfloat16

    qt = q_ref[0]
    q_s[...] = jnp.concatenate([qt[:, h * LANE:(h + 1) * LANE] for h in range(NSA_HPG)], axis=0).astype(bf16)

    r = lax.broadcasted_iota(jnp.int32, (rows_q, LANE), 0) & (LANE - 1)
    c = lax.broadcasted_iota(jnp.int32, (rows_q, LANE), 1)
    q_pos = i * LANE + r
    s = _dot_nt(q_s[...], kc_ref[0, 0].astype(bf16)) * SCALE + biasc_ref[...].reshape(rows_q, LANE)
    mask = c * CMP_STRIDE + (CMP_BLOCK - 1) <= q_pos
    s = jnp.where(mask, s, NEG)
    p = jnp.where(mask, jnp.exp(s - jnp.max(s, -1, keepdims=True)), 0.0)
    p = p / jnp.maximum(jnp.sum(p, -1, keepdims=True), 1e-30)
    out_s[0] = _dot(p.astype(bf16), vc_ref[0, 0].astype(bf16))

    p_sum = p[0:LANE]
    for h in range(1, NSA_HPG):
        p_sum = p_sum + p[h * LANE:(h + 1) * LANE]
    p_hi = p_sum.astype(bf16)
    p_lo = (p_sum - p_hi.astype(jnp.float32)).astype(bf16)
    imp = _dot(p_hi, c2s_ref[...]) + _dot(p_lo, c2s_ref[...])
    jj = lax.broadcasted_iota(jnp.int32, (LANE, LANE), 1)
    qp = i * LANE + lax.broadcasted_iota(jnp.int32, (LANE, LANE), 0)
    cur = qp >> int(math.log2(SLC_BLOCK))
    forced = (jj == 0) | (jj == cur) | (jj == cur - 1)
    val = jnp.where(jj * SLC_BLOCK <= qp, imp + jnp.where(forced, FORCE_BONUS, 0.0), NEG)
    val = jnp.where(jj < n_slc, val, -3e38)
    rank = jnp.zeros((LANE, LANE), jnp.int32)
    for t in range(n_slc):
        col = val[:, t:t + 1]
        ahead = (col > val) | ((col == val) & (jj > t))
        rank = rank + ahead.astype(jnp.int32)
    sel_s[...] = jnp.where(rank < n_top, 1.0, 0.0).astype(bf16)

    def attend(k_ref, v_ref, lo, w, use_sel, use_win, slot):
        m_s[...] = jnp.full((rows_q, 1), NEG, jnp.float32)
        l_s[...] = jnp.zeros((rows_q, 1), jnp.float32)
        acc_s[...] = jnp.zeros((rows_q, HEAD_DIM), jnp.float32)

        n_tiles = i + 1 - lo

        def body(j, carry):
            k0 = lo + w * j
            off = pl.multiple_of(k0 * LANE, LANE)
            k = k_ref[0, pl.ds(off, w * LANE), :].astype(bf16)
            v = v_ref[0, pl.ds(off, w * LANE), :].astype(bf16)
            dq = i - k0
            bias = [biast_ref[0, jnp.clip(dq - t, 0, 2)] for t in range(w)]
            sc = _dot_nt(q_s[...], k) * SCALE + (bias[0] if w == 1 else jnp.concatenate(bias, axis=1))
            rr = lax.broadcasted_iota(jnp.int32, (rows_q, w * LANE), 0) & (LANE - 1)
            cc = lax.broadcasted_iota(jnp.int32, (rows_q, w * LANE), 1)
            dist = rr - cc + dq * LANE
            msk = dist >= 0
            if use_win:
                msk = msk & (dist <= WINDOW)
            if use_sel:
                se = [_dot(sel_s[...], expand_ref[k0 + t]) for t in range(w)]
                se = se[0] if w == 1 else jnp.concatenate(se, axis=1)
                msk = msk & (jnp.concatenate([se] * NSA_HPG, axis=0) > 0.5)
            sc = jnp.where(msk, sc, NEG)
            m_old = m_s[...]
            m_new = jnp.maximum(m_old, jnp.max(sc, -1, keepdims=True))
            pe = jnp.where(msk, jnp.exp(sc - m_new), 0.0)
            alpha = jnp.exp(m_old - m_new)
            l_s[...] = alpha * l_s[...] + jnp.sum(pe, -1, keepdims=True)
            acc_s[...] = alpha * acc_s[...] + _dot(pe.astype(bf16), v)
            m_s[...] = m_new
            return carry

        assert w in (1, 2)
        lax.fori_loop(0, n_tiles if w == 1 else (n_tiles + 1) >> 1, body, 0)
        out_s[slot] = acc_s[...] / jnp.maximum(l_s[...], 1e-30)

    attend(ks_ref, vs_ref, 0, SLC_PAIR, True, False, 1)
    attend(kw_ref, vw_ref, jnp.maximum(i - WINDOW // LANE, 0), 1, False, True, 2)

    gt = gate_ref[0]
    for h in range(NSA_HPG):
        sl = slice(h * LANE, (h + 1) * LANE)
        o_ref[0, :, sl] = (gt[:, 3 * h:3 * h + 1] * out_s[0, sl, :]
                           + gt[:, 3 * h + 1:3 * h + 2] * out_s[1, sl, :]
                           + gt[:, 3 * h + 2:3 * h + 3] * out_s[2, sl, :])


def _t5_bucket_np(d):
    max_exact = N_BUCKETS // 2
    d = np.maximum(d, 0)
    large = max_exact + (np.log(np.maximum(d, 1).astype(np.float32) / np.float32(max_exact))
                         / np.float32(math.log(MAX_DISTANCE / max_exact)) * (N_BUCKETS - max_exact)).astype(np.int32)
    return np.where(d < max_exact, d, np.minimum(large, N_BUCKETS - 1)).astype(np.int32)


def _nsa_prompt(yb3, ys3, kcvc, bias_table):
    b, t, _ = yb3.shape
    n_t = t // LANE
    n_slc = t // SLC_BLOCK
    n_cmp = (t - CMP_BLOCK) // CMP_STRIDE + 1
    n_top = min(N_SELECT, n_slc)
    rows_q = NSA_HPG * LANE
    assert t % LANE == 0 and n_cmp <= LANE and n_slc <= LANE and n_t % SLC_PAIR == 0

    table = bias_table.astype(jnp.float32)
    dist_c = np.arange(t)[:, None] - (np.arange(LANE) * CMP_STRIDE + CMP_BLOCK - 1)[None, :]
    bias_c = jnp.moveaxis(table[_t5_bucket_np(dist_c)], -1, 0)
    rc = np.arange(LANE)[:, None] - np.arange(LANE)[None, :]
    buckets_t = np.stack([_t5_bucket_np(rc), _t5_bucket_np(rc + LANE), _t5_bucket_np(rc + 2 * LANE)])
    assert (_t5_bucket_np(np.arange(LANE + 1, 4 * LANE)) == N_BUCKETS - 1).all()
    bias_t = jnp.moveaxis(table[buckets_t], -1, 0)
    bias_t = bias_t.reshape(NSA_KV_GROUPS, NSA_HPG, 3, LANE, LANE).transpose(0, 2, 1, 3, 4)
    bias_t = bias_t.reshape(NSA_KV_GROUPS, 3, rows_q, LANE)

    c0 = np.arange(n_cmp) * CMP_STRIDE
    s0 = np.arange(n_slc) * SLC_BLOCK
    ov = np.minimum(c0[:, None] + CMP_BLOCK, s0[None, :] + SLC_BLOCK) - np.maximum(c0[:, None], s0[None, :])
    c2s = np.zeros((LANE, LANE), np.float32)
    c2s[:n_cmp, :n_slc] = np.maximum(ov, 0) / CMP_STRIDE
    expand = np.zeros((n_t, LANE, LANE), np.float32)
    for kt in range(n_t):
        tok_blk = (kt * LANE + np.arange(LANE)) // SLC_BLOCK
        expand[kt, tok_blk, np.arange(LANE)] = 1.0

    kv_spec = lambda cb: pl.BlockSpec((1, t, LANE), lambda bi, g, i: (bi, 0, cb + g))
    g_n = NSA_KV_GROUPS
    return pl.pallas_call(
        functools.partial(_nsa_kernel, n_slc=n_slc, n_top=n_top),
        grid=(b, g_n, n_t),
        in_specs=[
            pl.BlockSpec((1, LANE, rows_q), lambda bi, g, i: (bi, i, g)),
            pl.BlockSpec((1, 1, t // CMP_STRIDE, HEAD_DIM), lambda bi, g, i: (bi, g, 0, 0)),
            pl.BlockSpec((1, 1, t // CMP_STRIDE, HEAD_DIM), lambda bi, g, i: (bi, g_n + g, 0, 0)),
            kv_spec(CB_KVN + 2 * g_n), kv_spec(CB_KVN + 3 * g_n),
            kv_spec(CB_KVN + 4 * g_n), kv_spec(CB_KVN + 5 * g_n),
            pl.BlockSpec((1, LANE, LANE), lambda bi, g, i: (bi, i, g)),
            pl.BlockSpec((NSA_HPG, LANE, LANE), lambda bi, g, i: (g, i, 0)),
            pl.BlockSpec((1, 3, rows_q, LANE), lambda bi, g, i: (g, 0, 0, 0)),
            pl.BlockSpec((LANE, LANE), lambda bi, g, i: (0, 0)),
            pl.BlockSpec((n_t, LANE, LANE), lambda bi, g, i: (0, 0, 0)),
        ],
        out_specs=pl.BlockSpec((1, LANE, rows_q), lambda bi, g, i: (bi, i, g)),
        out_shape=jax.ShapeDtypeStruct((b, t, NSA_Q), jnp.float32),
        scratch_shapes=[pltpu.VMEM((rows_q, HEAD_DIM), jnp.bfloat16),
                        pltpu.VMEM((LANE, LANE), jnp.bfloat16),
                        pltpu.VMEM((rows_q, 1), jnp.float32),
                        pltpu.VMEM((rows_q, 1), jnp.float32),
                        pltpu.VMEM((rows_q, HEAD_DIM), jnp.float32),
                        pltpu.VMEM((3, rows_q, HEAD_DIM), jnp.float32)],
        compiler_params=pltpu.CompilerParams(
            dimension_semantics=("parallel", "parallel", "arbitrary"), vmem_limit_bytes=VMEM_LIMIT),
        name="nsa_prompt",
    )(yb3, kcvc, kcvc, yb3, yb3, yb3, yb3, ys3, bias_c, bias_t,
      jnp.asarray(c2s, jnp.bfloat16), jnp.asarray(expand, jnp.bfloat16))


FOX_TILE = 256
FOX_KTILE = 512


def _fox_kernel(q_ref, k_ref, v_ref, cq_ref, ck_ref, o_ref, m_s, l_s, acc_s):
    i = pl.program_id(2)
    tq = FOX_TILE
    bf16 = jnp.bfloat16
    q = q_ref[0].astype(bf16)
    cq = cq_ref[0, 0]
    m_s[...] = jnp.full((tq, 1), NEG, jnp.float32)
    l_s[...] = jnp.zeros((tq, 1), jnp.float32)
    acc_s[...] = jnp.zeros((tq, HEAD_DIM), jnp.float32)

    tk = FOX_KTILE

    def body(kt, carry):
        off = pl.multiple_of(kt * tk, tk)
        k = k_ref[0, pl.ds(off, tk), :].astype(bf16)
        v = v_ref[0, pl.ds(off, tk), :].astype(bf16)
        sc = _dot_nt(q, k) * SCALE + cq - ck_ref[0, 0, kt]
        rr = lax.broadcasted_iota(jnp.int32, (tq, tk), 0)
        cc = lax.broadcasted_iota(jnp.int32, (tq, tk), 1)
        msk = cc + kt * tk <= rr + i * tq
        sc = jnp.where(msk, sc, NEG)
        m_old = m_s[...]
        m_new = jnp.maximum(m_old, jnp.max(sc, -1, keepdims=True))
        pe = jnp.where(msk, jnp.exp(sc - m_new), 0.0)
        alpha = jnp.exp(m_old - m_new)
        l_s[...] = alpha * l_s[...] + jnp.sum(pe, -1, keepdims=True)
        acc_s[...] = alpha * acc_s[...] + _dot(pe.astype(bf16), v)
        m_s[...] = m_new
        return carry

    lax.fori_loop(0, ((i * tq) >> int(math.log2(tk))) + 1, body, 0)
    o_ref[0] = acc_s[...] / jnp.maximum(l_s[...], 1e-30)


def _fox_prompt(yb3, logf):
    b, t, _ = yb3.shape
    tq = FOX_TILE
    n_t = t // tq
    cum = jnp.moveaxis(jnp.cumsum(logf.astype(jnp.float32), axis=1), 1, 2)
    cum_q = cum[..., None]
    tk = FOX_KTILE
    assert t % tk == 0 and tk % tq == 0
    cum_k = cum.reshape(b, FOX_HEADS, t // tk, 1, tk)
    return pl.pallas_call(
        _fox_kernel,
        grid=(b, FOX_HEADS, n_t),
        in_specs=[pl.BlockSpec((1, tq, LANE), lambda bi, h, i: (bi, i, CB_QF + h)),
                  pl.BlockSpec((1, t, LANE), lambda bi, h, i: (bi, 0, CB_KF + h)),
                  pl.BlockSpec((1, t, LANE), lambda bi, h, i: (bi, 0, CB_VF + h)),
                  pl.BlockSpec((1, 1, tq, 1), lambda bi, h, i: (bi, h, i, 0)),
                  pl.BlockSpec((1, 1, t // tk, 1, tk), lambda bi, h, i: (bi, h, 0, 0, 0))],
        out_specs=pl.BlockSpec((1, tq, LANE), lambda bi, h, i: (bi, i, h)),
        out_shape=jax.ShapeDtypeStruct((b, t, FOX_W), jnp.float32),
        scratch_shapes=[pltpu.VMEM((tq, 1), jnp.float32),
                        pltpu.VMEM((tq, 1), jnp.float32),
                        pltpu.VMEM((tq, HEAD_DIM), jnp.float32)],
        compiler_params=pltpu.CompilerParams(
            dimension_semantics=("parallel", "parallel", "arbitrary"), vmem_limit_bytes=VMEM_LIMIT),
        name="fox_prompt",
    )(yb3, yb3, yb3, cum_q, cum_k)


def _ln(z, g, b):
    mu = jnp.mean(z, -1, keepdims=True)
    zc = z - mu
    var = jnp.mean(zc * zc, -1, keepdims=True)
    return zc * lax.rsqrt(var + LN_EPS) * g + b


def _post_kernel(on_ref, of_ref, x_ref, gn_ref, gf_ref, w_ref, lg_ref, lb_ref, h_ref):
    def rms(o, g):
        return (o * lax.rsqrt(jnp.mean(o * o, -1, keepdims=True) + LN_EPS) * g).astype(jnp.bfloat16)

    mix = (_dot(rms(on_ref[...], gn_ref[...]), w_ref[:NSA_Q, :])
           + _dot(rms(of_ref[...], gf_ref[...]), w_ref[NSA_Q:, :]))
    h_ref[...] = _ln(ALPHA * x_ref[...] + mix, lg_ref[...], lb_ref[...])


def _post_attention(o_n, o_f, x, g_nsa, g_fox, w_out, ln_g, ln_b, tm=256):
    m, d = x.shape
    tm = min(tm, m)
    row = lambda n: pl.BlockSpec((1, n), lambda i: (0, 0))
    return pl.pallas_call(
        _post_kernel,
        grid=(m // tm,),
        in_specs=[pl.BlockSpec((tm, NSA_Q), lambda i: (i, 0)),
                  pl.BlockSpec((tm, FOX_W), lambda i: (i, 0)),
                  pl.BlockSpec((tm, d), lambda i: (i, 0)),
                  row(NSA_Q), row(FOX_W),
                  pl.BlockSpec((NSA_Q + FOX_W, d), lambda i: (0, 0)),
                  row(d), row(d)],
        out_specs=pl.BlockSpec((tm, d), lambda i: (i, 0)),
        out_shape=jax.ShapeDtypeStruct((m, d), jnp.float32),
        compiler_params=pltpu.CompilerParams(
            dimension_semantics=("parallel",), vmem_limit_bytes=VMEM_LIMIT),
        name="post_attention",
    )(o_n, o_f, x, g_nsa[None], g_fox[None], w_out.astype(jnp.bfloat16), ln_g[None], ln_b[None])


def _add_ln_kernel(h_ref, f_ref, g_ref, b_ref, o_ref):
    o_ref[...] = _ln(ALPHA * h_ref[...] + f_ref[...], g_ref[...], b_ref[...])


def _add_ln(h, f, ln_g, ln_b, tm=512):
    m, d = h.shape
    tm = min(tm, m)
    return pl.pallas_call(
        _add_ln_kernel,
        grid=(m // tm,),
        in_specs=[pl.BlockSpec((tm, d), lambda i: (i, 0)),
                  pl.BlockSpec((tm, d), lambda i: (i, 0)),
                  pl.BlockSpec((1, d), lambda i: (0, 0)),
                  pl.BlockSpec((1, d), lambda i: (0, 0))],
        out_specs=pl.BlockSpec((tm, d), lambda i: (i, 0)),
        out_shape=jax.ShapeDtypeStruct((m, d), jnp.float32),
        compiler_params=pltpu.CompilerParams(
            dimension_semantics=("parallel",), vmem_limit_bytes=VMEM_LIMIT),
        name="add_layer_norm",
    )(h, f, ln_g[None], ln_b[None])


PEER_TILE = 128
N_ROUTES = PEER_HEADS * PEER_TOPK


def _top_rows(vals, row_id, n_out, payload=None):
    big = float(vals.shape[0])
    out_v, out_i = [], []
    for _ in range(n_out):
        m = jnp.max(vals, axis=0, keepdims=True)
        win = jnp.min(jnp.where(vals == m, row_id, big), axis=0, keepdims=True)
        hit = row_id == win
        out_v.append(m)
        if payload is None:
            out_i.append(win)
        else:
            out_i.append(jnp.sum(jnp.where(hit, payload, 0.0), axis=0, keepdims=True))
        vals = jnp.where(hit, -jnp.inf, vals)
    return jnp.concatenate(out_v, axis=0), jnp.concatenate(out_i, axis=0)


def _peer_route_kernel(h_ref, wq_ref, keys_ref, g_ref, e_ref, sv_s, si_s):
    bf16 = jnp.bfloat16
    tm = PEER_TILE
    half = PEER_QDIM // 2
    q = _dot(h_ref[...].astype(bf16), wq_ref[...]).astype(bf16)
    key_id = lax.broadcasted_iota(jnp.int32, (N_KEYS, tm), 0).astype(jnp.float32)
    for hp in range(2 * PEER_HEADS):
        s_t = _dot_nt(keys_ref[hp], q[:, hp * half:(hp + 1) * half])
        sv, si = _top_rows(s_t, key_id, PEER_TOPK)
        sv_s[hp] = sv
        si_s[hp] = si
    pair_id = lax.broadcasted_iota(jnp.int32, (PEER_TOPK * PEER_TOPK, tm), 0).astype(jnp.float32)
    for h in range(PEER_HEADS):
        sv0, sv1 = sv_s[2 * h], sv_s[2 * h + 1]
        si0, si1 = si_s[2 * h], si_s[2 * h + 1]
        cand = jnp.concatenate([sv0[a:a + 1, :] + sv1 for a in range(PEER_TOPK)], axis=0)
        expert = jnp.concatenate([si0[a:a + 1, :] * float(N_KEYS) + si1 for a in range(PEER_TOPK)], axis=0)
        best, eid = _top_rows(cand, pair_id, PEER_TOPK, payload=expert)
        ex = jnp.exp(best - best[0:1, :])
        g_ref[0, h * PEER_TOPK:(h + 1) * PEER_TOPK, :] = ex / jnp.sum(ex, axis=0, keepdims=True)
        e_ref[0, h * PEER_TOPK:(h + 1) * PEER_TOPK, :] = eid.astype(jnp.int32)


def _peer_route(h, w_q, sub_keys):
    n, d = h.shape
    tm = PEER_TILE
    nb = n // tm
    n_hp = 2 * PEER_HEADS
    half = PEER_QDIM // 2
    out = jax.ShapeDtypeStruct((nb, N_ROUTES, tm), jnp.float32)
    return pl.pallas_call(
        _peer_route_kernel,
        grid=(nb,),
        in_specs=[pl.BlockSpec((tm, d), lambda i: (i, 0)),
                  pl.BlockSpec((d, PEER_HEADS * PEER_QDIM), lambda i: (0, 0)),
                  pl.BlockSpec((n_hp, N_KEYS, half), lambda i: (0, 0, 0))],
        out_specs=[pl.BlockSpec((1, N_ROUTES, tm), lambda i: (i, 0, 0)),
                   pl.BlockSpec((1, N_ROUTES, tm), lambda i: (i, 0, 0))],
        out_shape=[out, jax.ShapeDtypeStruct((nb, N_ROUTES, tm), jnp.int32)],
        scratch_shapes=[pltpu.VMEM((n_hp, PEER_TOPK, tm), jnp.float32),
                        pltpu.VMEM((n_hp, PEER_TOPK, tm), jnp.float32)],
        compiler_params=pltpu.CompilerParams(
            dimension_semantics=("parallel",), vmem_limit_bytes=VMEM_LIMIT),
        name="peer_route",
    )(h, w_q.astype(jnp.bfloat16), sub_keys.reshape(n_hp, N_KEYS, half).astype(jnp.bfloat16))


PACK_DTYPE = jnp.dtype("bfloat16")
PEER_SLOTS = 4
PEER_AHEAD = PEER_SLOTS - 1


def _pack_expert_tables(u, v):
    def pack(t):
        bits = lax.bitcast_convert_type(t.astype(PACK_DTYPE), jnp.uint16).astype(jnp.uint32)
        half = t.shape[1] // 2
        return bits[:, :half] | (bits[:, half:] << 16)
    return jnp.concatenate([pack(u), pack(v)], axis=1)


def _unpack_words(w):
    lo = lax.bitcast_convert_type(w << 16, jnp.float32)
    hi = lax.bitcast_convert_type(w & jnp.uint32(0xFFFF0000), jnp.float32)
    return lo, hi


def _peer_expert_kernel(e_ref, g_ref, x_ref, uv_hbm, o_ref, e_smem, buf, sem, esem):
    tm = PEER_TILE
    hw = D_MODEL // 2
    n_chunk = hw // LANE
    per_point = N_ROUTES // (2 * n_chunk)
    ids = pltpu.make_async_copy(e_ref.at[0], e_smem, esem)
    ids.start()
    ids.wait()

    def issue(t, slot, ks):
        for k in ks:
            pltpu.make_async_copy(uv_hbm.at[pl.ds(e_smem[k, t], 1), :],
                                  buf.at[slot, pl.ds(k, 1), :], sem.at[slot]).start()

    def wait_rows(slot):
        pltpu.make_async_copy(uv_hbm.at[pl.ds(0, N_ROUTES), :], buf.at[slot], sem.at[slot]).wait()

    def compute(t, slot, t_ahead):
        other = (slot + PEER_AHEAD) % PEER_SLOTS
        point = [0]

        def issue_some():
            issue(t_ahead, other, range(point[0] * per_point, (point[0] + 1) * per_point))
            point[0] += 1

        x_row = x_ref[pl.ds(t, 1), :]
        acc = None
        for c in range(n_chunk):
            lo, hi = _unpack_words(buf[slot, :, c * LANE:(c + 1) * LANE])
            term = lo * x_row[:, c * LANE:(c + 1) * LANE] + hi * x_row[:, hw + c * LANE:hw + (c + 1) * LANE]
            acc = term if acc is None else acc + term
            issue_some()
        s = jnp.sum(acc, axis=1, keepdims=True)
        gate = pltpu.roll(g_ref[0], jnp.where(t == 0, 0, tm - t), 1)[:, 0:1]
        coef = gate * _gelu_tanh(s)
        lo_out, hi_out = [], []
        for c in range(n_chunk):
            lo, hi = _unpack_words(buf[slot, :, hw + c * LANE:hw + (c + 1) * LANE])
            lo_out.append(jnp.sum(lo * coef, axis=0, keepdims=True))
            hi_out.append(jnp.sum(hi * coef, axis=0, keepdims=True))
            issue_some()
        o_ref[pl.ds(t, 1), :] = jnp.concatenate(lo_out + hi_out, axis=1)

    for j in range(PEER_AHEAD):
        issue(j, j, range(N_ROUTES))

    def body(i, carry):
        for j in range(PEER_SLOTS):
            t = PEER_SLOTS * i + j
            wait_rows(j)
            compute(t, j, jnp.minimum(t + PEER_AHEAD, tm - 1))
        return carry

    lax.fori_loop(0, tm // PEER_SLOTS, body, 0)
    for j in range(PEER_AHEAD):
        wait_rows((tm + j) % PEER_SLOTS)


def _peer_experts(h, gates, experts, uv):
    n, d = h.shape
    tm = PEER_TILE
    nb = n // tm
    return pl.pallas_call(
        _peer_expert_kernel,
        grid=(nb,),
        in_specs=[pl.BlockSpec((1, N_ROUTES, tm), lambda i: (i, 0, 0)),
                  pl.BlockSpec((1, N_ROUTES, tm), lambda i: (i, 0, 0)),
                  pl.BlockSpec((tm, d), lambda i: (i, 0)),
                  pl.BlockSpec(memory_space=pl.ANY)],
        out_specs=pl.BlockSpec((tm, d), lambda i: (i, 0)),
        out_shape=jax.ShapeDtypeStruct((n, d), jnp.float32),
        scratch_shapes=[pltpu.SMEM((N_ROUTES, tm), jnp.int32),
                        pltpu.VMEM((PEER_SLOTS, N_ROUTES, d), jnp.uint32),
                        pltpu.SemaphoreType.DMA((PEER_SLOTS,)),
                        pltpu.SemaphoreType.DMA],
        compiler_params=pltpu.CompilerParams(
            dimension_semantics=("arbitrary",), vmem_limit_bytes=VMEM_LIMIT),
        name="peer_experts",
    )(experts, gates, h, uv)


def _peer(h, w_q, sub_keys, uv):
    gates, experts = _peer_route(h, w_q, sub_keys)
    return _peer_experts(h, gates, experts, uv)


QSLOT = 8
DEC_GROUP = 4
NSA_STREAMS = 2 * 2 * NSA_KV_GROUPS


def _nsa_decode_kernel(pt_ref, q_ref, ks_new_ref, vs_new_ref, kw_new_ref, vw_new_ref, kwin_ref, vwin_ref,
                       gate_ref, biasc_ref, biast_ref, c2s_ref, expand_ref, pe_ref, w1_ref, w2_ref, pool_hbm,
                       o_ref, kbuf, nbuf, wbuf, kc_s, q_s, sel_s, m_s, l_s, acc_s, out_s, sem,
                       *, n_pages, page, tq, n_slc, n_top):
    b = pl.program_id(0)
    g = pl.program_id(1)
    bf16 = jnp.bfloat16
    rows_q = NSA_HPG * QSLOT
    p0 = n_pages * page
    n_half = p0 // CMP_STRIDE
    i_slc = p0 // LANE
    i_win = WINDOW // LANE

    def page_copy(p):
        return pltpu.make_async_copy(pool_hbm.at[pt_ref[b, p]],
                                     kbuf.at[pl.ds(p * page * NSA_STREAMS, page * NSA_STREAMS), :], sem.at[0])

    @pl.when(g == 0)
    def _():
        for p in range(n_pages):
            page_copy(p).start()
        for p in range(n_pages):
            page_copy(p).wait()

    def cached_rows(c, first, count, step=1):
        return kbuf[pl.ds(first * NSA_STREAMS + 2 * c + g, count, stride=step * NSA_STREAMS), :]

    q_s[...] = jnp.zeros((rows_q, HEAD_DIM), jnp.float32)
    qt = q_ref[0]
    for h in range(NSA_HPG):
        q_s[h * QSLOT:h * QSLOT + tq, :] = qt[:, h * LANE:(h + 1) * LANE]
    zeros_tile = jnp.zeros((LANE, HEAD_DIM), jnp.float32)
    wbuf[0, 0:WINDOW, :] = kwin_ref[0]
    wbuf[1, 0:WINDOW, :] = vwin_ref[0]
    wbuf[0, WINDOW:WINDOW + LANE, :] = zeros_tile
    wbuf[1, WINDOW:WINDOW + LANE, :] = zeros_tile
    wbuf[0, WINDOW:WINDOW + tq, :] = kw_new_ref[0]
    wbuf[1, WINDOW:WINDOW + tq, :] = vw_new_ref[0]

    for kv in range(2):
        def half(s0):
            acc = jnp.zeros((n_half, HEAD_DIM), jnp.float32)
            for s in range(CMP_STRIDE):
                rows = cached_rows(kv, s, n_half, CMP_STRIDE) + pe_ref[kv, s0 + s:s0 + s + 1, :]
                acc = acc + _dot(rows.astype(bf16), w1_ref[kv, s0 + s])
            return acc

        first = half(0)
        second = half(CMP_STRIDE)
        hmid = first + pltpu.roll(second, n_half - 1, 0)
        kc_s[kv] = _dot(_gelu_tanh(hmid).astype(bf16), w2_ref[kv])

    qb = q_s[...].astype(bf16)
    r = lax.broadcasted_iota(jnp.int32, (rows_q, n_half), 0) & (QSLOT - 1)
    c = lax.broadcasted_iota(jnp.int32, (rows_q, n_half), 1)
    s = _dot_nt(qb, kc_s[0].astype(bf16)) * SCALE + biasc_ref[...].reshape(rows_q, n_half)
    mask = c * CMP_STRIDE + (CMP_BLOCK - 1) <= p0 + r
    s = jnp.where(mask, s, NEG)
    p = jnp.where(mask, jnp.exp(s - jnp.max(s, -1, keepdims=True)), 0.0)
    p = p / jnp.maximum(jnp.sum(p, -1, keepdims=True), 1e-30)
    out_s[0] = _dot(p.astype(bf16), kc_s[1].astype(bf16))

    p_sum = p[0:QSLOT]
    for h in range(1, NSA_HPG):
        p_sum = p_sum + p[h * QSLOT:(h + 1) * QSLOT]
    p_hi = p_sum.astype(bf16)
    p_lo = (p_sum - p_hi.astype(jnp.float32)).astype(bf16)
    imp = _dot(p_hi, c2s_ref[...]) + _dot(p_lo, c2s_ref[...])
    wide = 2 * LANE
    jj = lax.broadcasted_iota(jnp.int32, (QSLOT, wide), 1)
    qp = p0 + lax.broadcasted_iota(jnp.int32, (QSLOT, wide), 0)
    cur = qp >> int(math.log2(SLC_BLOCK))
    forced = (jj == 0) | (jj == cur) | (jj == cur - 1)
    val = jnp.where(jj * SLC_BLOCK <= qp, imp + jnp.where(forced, FORCE_BONUS, 0.0), NEG)
    val = jnp.where(jj < n_slc, val, -3e38)
    rank = jnp.zeros((QSLOT, wide), jnp.int32)
    for t in range(n_slc):
        col = val[:, t:t + 1]
        ahead = (col > val) | ((col == val) & (jj > t))
        rank = rank + ahead.astype(jnp.int32)
    sel_s[...] = jnp.where(rank < n_top, 1.0, 0.0)[:, :LANE]

    def attend(kv_at, i_tile, use_sel, use_win, slot):
        m_s[...] = jnp.full((rows_q, 1), NEG, jnp.float32)
        l_s[...] = jnp.zeros((rows_q, 1), jnp.float32)
        acc_s[...] = jnp.zeros((rows_q, HEAD_DIM), jnp.float32)

        def tiles(k0, w, with_sel):
            k, v = kv_at(k0, w)
            k = k.astype(bf16)
            v = v.astype(bf16)
            bias = [biast_ref[0, jnp.minimum(i_tile - k0 - j, 2)] for j in range(w)]
            sc = _dot_nt(qb, k) * SCALE + (bias[0] if w == 1 else jnp.concatenate(bias, axis=1))
            rr = lax.broadcasted_iota(jnp.int32, (rows_q, w * LANE), 0) & (QSLOT - 1)
            cc = lax.broadcasted_iota(jnp.int32, (rows_q, w * LANE), 1)
            dist = rr - cc + (i_tile - k0) * LANE
            msk = dist >= 0
            if use_win:
                msk = msk & (dist <= WINDOW)
            if with_sel:
                sb = sel_s[...].astype(bf16)
                se = jnp.concatenate([_dot(sb, expand_ref[k0 + j]) for j in range(w)], axis=1)
                msk = msk & (jnp.concatenate([se] * NSA_HPG, axis=0) > 0.5)
            sc = jnp.where(msk, sc, NEG)
            m_old = m_s[...]
            m_new = jnp.maximum(m_old, jnp.max(sc, -1, keepdims=True))
            pe = jnp.where(msk, jnp.exp(sc - m_new), 0.0)
            alpha = jnp.exp(m_old - m_new)
            l_s[...] = alpha * l_s[...] + jnp.sum(pe, -1, keepdims=True)
            acc_s[...] = alpha * acc_s[...] + _dot(pe.astype(bf16), v)
            m_s[...] = m_new

        def body(grp, carry):
            tiles(grp * DEC_GROUP, DEC_GROUP, use_sel)
            return carry

        lax.fori_loop(0, i_tile // DEC_GROUP, body, 0)
        tiles(i_tile, 1, False)
        out_s[slot] = acc_s[...] / jnp.maximum(l_s[...], 1e-30)

    nbuf[0] = zeros_tile
    nbuf[1] = zeros_tile
    nbuf[0, 0:tq, :] = ks_new_ref[0]
    nbuf[1, 0:tq, :] = vs_new_ref[0]

    def slc_at(k0, w):
        if w == 1:
            return nbuf[0], nbuf[1]
        return cached_rows(2, k0 * LANE, w * LANE), cached_rows(3, k0 * LANE, w * LANE)

    def win_at(k0, w):
        off = pl.multiple_of(k0 * LANE, LANE)
        return wbuf[0, pl.ds(off, w * LANE), :], wbuf[1, pl.ds(off, w * LANE), :]

    attend(slc_at, i_slc, True, False, 1)
    attend(win_at, i_win, False, True, 2)

    gt = gate_ref[0]
    for h in range(NSA_HPG):
        sl = slice(h * QSLOT, h * QSLOT + tq)
        o_ref[0, :, h * LANE:(h + 1) * LANE] = (gt[:, 3 * h:3 * h + 1] * out_s[0, sl, :]
                                                 + gt[:, 3 * h + 1:3 * h + 2] * out_s[1, sl, :]
                                                 + gt[:, 3 * h + 2:3 * h + 3] * out_s[2, sl, :])


def _nsa_decode(yb3, ys3, pool, win_cache, page_table, bias_table, cmp_pos, cmp_w1, cmp_w2):
    b, tq, _ = yb3.shape
    n_pages = page_table.shape[1]
    page = pool.shape[1] // NSA_STREAMS
    p0 = n_pages * page
    tk = p0 + tq
    n_slc = -(-tk // SLC_BLOCK)
    n_cmp = (tk - CMP_BLOCK) // CMP_STRIDE + 1
    n_half = p0 // CMP_STRIDE
    n_top = min(N_SELECT, n_slc)
    rows_q = NSA_HPG * QSLOT
    g_n = NSA_KV_GROUPS
    wide = 2 * LANE
    assert p0 % LANE == 0 and tq <= QSLOT and tq <= SLC_BLOCK and p0 % SLC_BLOCK == 0
    assert n_cmp + 1 == n_half and n_slc <= wide and (n_slc - 1) * SLC_BLOCK == p0
    assert (p0 // LANE) % DEC_GROUP == 0 and (WINDOW // LANE) % DEC_GROUP == 0
    assert win_cache.shape[1] == WINDOW

    table = bias_table.astype(jnp.float32)
    dist_c = p0 + np.arange(QSLOT)[:, None] - (np.arange(n_half) * CMP_STRIDE + CMP_BLOCK - 1)[None, :]
    bias_c = jnp.moveaxis(table[_t5_bucket_np(dist_c)], -1, 0)
    rc = np.arange(QSLOT)[:, None] - np.arange(LANE)[None, :]
    buckets_t = np.stack([_t5_bucket_np(rc), _t5_bucket_np(rc + LANE), _t5_bucket_np(rc + 2 * LANE)])
    bias_t = jnp.moveaxis(table[buckets_t], -1, 0)
    bias_t = bias_t.reshape(g_n, NSA_HPG, 3, QSLOT, LANE).transpose(0, 2, 1, 3, 4).reshape(g_n, 3, rows_q, LANE)

    c0 = np.arange(n_cmp) * CMP_STRIDE
    s0 = np.arange(n_slc) * SLC_BLOCK
    ov = np.minimum(c0[:, None] + CMP_BLOCK, s0[None, :] + SLC_BLOCK) - np.maximum(c0[:, None], s0[None, :])
    c2s = np.zeros((n_half, wide), np.float32)
    c2s[:n_cmp, :n_slc] = np.maximum(ov, 0) / CMP_STRIDE
    n_t = p0 // LANE
    expand = np.zeros((n_t, LANE, LANE), np.float32)
    for kt in range(n_t):
        expand[kt, (kt * LANE + np.arange(LANE)) // SLC_BLOCK, np.arange(LANE)] = 1.0

    new_spec = lambda cb: pl.BlockSpec((1, tq, LANE), lambda bi, g, pt: (bi, 0, cb + g))
    full = lambda shape: pl.BlockSpec(shape, lambda bi, g, pt: (0,) * len(shape))
    grid_spec = pltpu.PrefetchScalarGridSpec(
        num_scalar_prefetch=1,
        grid=(b, g_n),
        in_specs=[
            pl.BlockSpec((1, tq, NSA_HPG * LANE), lambda bi, g, pt: (bi, 0, g)),
            new_spec(CB_KVN + 2 * g_n), new_spec(CB_KVN + 3 * g_n),
            new_spec(CB_KVN + 4 * g_n), new_spec(CB_KVN + 5 * g_n),
            pl.BlockSpec((1, WINDOW, LANE), lambda bi, g, pt: (bi, 0, g)),
            pl.BlockSpec((1, WINDOW, LANE), lambda bi, g, pt: (bi, 0, g_n + g)),
            pl.BlockSpec((1, tq, LANE), lambda bi, g, pt: (bi, 0, g)),
            pl.BlockSpec((NSA_HPG, QSLOT, n_half), lambda bi, g, pt: (g, 0, 0)),
            pl.BlockSpec((1, 3, rows_q, LANE), lambda bi, g, pt: (g, 0, 0, 0)),
            full((n_half, wide)), full((n_t, LANE, LANE)),
            full((2, CMP_BLOCK, HEAD_DIM)), full((2, CMP_BLOCK, HEAD_DIM, HEAD_DIM)), full((2, HEAD_DIM, HEAD_DIM)),
            pl.BlockSpec(memory_space=pl.ANY),
        ],
        out_specs=pl.BlockSpec((1, tq, NSA_HPG * LANE), lambda bi, g, pt: (bi, 0, g)),
        scratch_shapes=[pltpu.VMEM((p0 * NSA_STREAMS, HEAD_DIM), jnp.float32),
                        pltpu.VMEM((2, LANE, HEAD_DIM), jnp.float32),
                        pltpu.VMEM((2, WINDOW + LANE, HEAD_DIM), jnp.float32),
                        pltpu.VMEM((2, n_half, HEAD_DIM), jnp.float32),
                        pltpu.VMEM((rows_q, HEAD_DIM), jnp.float32),
                        pltpu.VMEM((QSLOT, LANE), jnp.float32),
                        pltpu.VMEM((rows_q, 1), jnp.float32),
                        pltpu.VMEM((rows_q, 1), jnp.float32),
                        pltpu.VMEM((rows_q, HEAD_DIM), jnp.float32),
                        pltpu.VMEM((3, rows_q, HEAD_DIM), jnp.float32),
                        pltpu.SemaphoreType.DMA((1,))],
    )
    return pl.pallas_call(
        functools.partial(_nsa_decode_kernel, n_pages=n_pages, page=page, tq=tq, n_slc=n_slc, n_top=n_top),
        grid_spec=grid_spec,
        out_shape=jax.ShapeDtypeStruct((b, tq, NSA_Q), jnp.float32),
        compiler_params=pltpu.CompilerParams(
            dimension_semantics=("arbitrary", "arbitrary"), vmem_limit_bytes=VMEM_LIMIT),
        name="nsa_decode",
    )(page_table, yb3, yb3, yb3, yb3, yb3, win_cache, win_cache, ys3, bias_c, bias_t,
      jnp.asarray(c2s, jnp.bfloat16), jnp.asarray(expand, jnp.bfloat16),
      cmp_pos, cmp_w1.astype(jnp.bfloat16), cmp_w2.astype(jnp.bfloat16), pool)


def _fox_decode_kernel(pt_ref, qa_ref, qb_ref, ka_ref, kb_ref, va_ref, vb_ref, cq_ref, ck_ref, kv_ref,
                       o_ref, q_s, new_s, m_s, l_s, acc_s, *, n_pages, tq):
    p = pl.program_id(1)
    bf16 = jnp.bfloat16
    half_h = FOX_HEADS // 2

    rows = FOX_HEADS * QSLOT

    @pl.when(p == 0)
    def _():
        m_s[...] = jnp.full(m_s.shape, NEG, jnp.float32)
        l_s[...] = jnp.zeros(l_s.shape, jnp.float32)
        acc_s[...] = jnp.zeros(acc_s.shape, jnp.float32)
        q_s[...] = jnp.zeros(q_s.shape, jnp.float32)
        new_s[...] = jnp.zeros(new_s.shape, jnp.float32)
        for h in range(FOX_HEADS):
            src_q, src_k, src_v = (qa_ref, ka_ref, va_ref) if h < half_h else (qb_ref, kb_ref, vb_ref)
            lo = (h % half_h) * LANE
            q_s[h * QSLOT:h * QSLOT + tq, :] = src_q[0, :, lo:lo + LANE]
            new_s[0:tq, h * LANE:(h + 1) * LANE] = src_k[0, :, lo:lo + LANE]
            new_s[0:tq, FOX_W + h * LANE:FOX_W + (h + 1) * LANE] = src_v[0, :, lo:lo + LANE]

    def step(kv_at, is_new):
        qb = q_s[...].astype(bf16)
        sc = jnp.concatenate(
            [_dot_nt(qb[h * QSLOT:(h + 1) * QSLOT], kv_at(h * LANE).astype(bf16)) for h in range(FOX_HEADS)], axis=0)
        ck = jnp.concatenate([jnp.broadcast_to(ck_ref[0, h, 0], (QSLOT, LANE)) for h in range(FOX_HEADS)], axis=0)
        sc = sc * SCALE + cq_ref[0].reshape(rows, 1) - ck
        if is_new:
            rr = lax.broadcasted_iota(jnp.int32, (rows, LANE), 0) & (QSLOT - 1)
            cc = lax.broadcasted_iota(jnp.int32, (rows, LANE), 1)
            msk = cc <= rr
            sc = jnp.where(msk, sc, NEG)
        m_old = m_s[...]
        m_new = jnp.maximum(m_old, jnp.max(sc, -1, keepdims=True))
        pe = jnp.exp(sc - m_new)
        if is_new:
            pe = jnp.where(msk, pe, 0.0)
        alpha = jnp.exp(m_old - m_new)
        l_s[...] = alpha * l_s[...] + jnp.sum(pe, -1, keepdims=True)
        pb = pe.astype(bf16)
        pv = jnp.concatenate(
            [_dot(pb[h * QSLOT:(h + 1) * QSLOT], kv_at(FOX_W + h * LANE).astype(bf16)) for h in range(FOX_HEADS)],
            axis=0)
        acc_s[...] = alpha * acc_s[...] + pv
        m_s[...] = m_new

    @pl.when(p < n_pages)
    def _():
        step(lambda col: kv_ref[0, pl.ds(col // LANE, LANE, stride=2 * FOX_HEADS), :], False)

    @pl.when(p == n_pages)
    def _():
        step(lambda col: new_s[:, col:col + LANE], True)
        res = acc_s[...] / jnp.maximum(l_s[...], 1e-30)
        for h in range(FOX_HEADS):
            o_ref[0, :, h * LANE:(h + 1) * LANE] = res[h * QSLOT:h * QSLOT + tq, :]


def _fox_decode(yb3, logf_new, pool, logf_past, page_table):
    b, tq, _ = yb3.shape
    n_pages = page_table.shape[1]
    page = pool.shape[1] // (2 * FOX_HEADS)
    p0 = n_pages * page
    assert page == LANE and tq <= QSLOT
    cum = jnp.cumsum(jnp.concatenate([logf_past.astype(jnp.float32), logf_new], axis=1), axis=1)
    cum_q = jnp.pad(jnp.moveaxis(cum[:, p0:], 1, 2), ((0, 0), (0, 0), (0, QSLOT - tq)))[..., None]
    cum_k = jnp.pad(jnp.moveaxis(cum, 1, 2), ((0, 0), (0, 0), (0, LANE - tq)))
    cum_k = cum_k.reshape(b, FOX_HEADS, n_pages + 1, 1, LANE)
    wq = FOX_W // 2
    blk = lambda col: pl.BlockSpec((1, tq, wq), lambda bi, p, pt: (bi, 0, col))
    base_q, base_k, base_v = CB_QF * LANE // wq, CB_KF * LANE // wq, CB_VF * LANE // wq
    grid_spec = pltpu.PrefetchScalarGridSpec(
        num_scalar_prefetch=1,
        grid=(b, n_pages + 1),
        in_specs=[blk(base_q), blk(base_q + 1), blk(base_k), blk(base_k + 1), blk(base_v), blk(base_v + 1),
                  pl.BlockSpec((1, FOX_HEADS, QSLOT, 1), lambda bi, p, pt: (bi, 0, 0, 0)),
                  pl.BlockSpec((1, FOX_HEADS, 1, 1, LANE), lambda bi, p, pt: (bi, 0, p, 0, 0)),
                  pl.BlockSpec((1, page * 2 * FOX_HEADS, HEAD_DIM),
                               lambda bi, p, pt: (pt[bi, jnp.minimum(p, n_pages - 1)], 0, 0))],
        out_specs=pl.BlockSpec((1, tq, FOX_W), lambda bi, p, pt: (bi, 0, 0)),
        scratch_shapes=[pltpu.VMEM((FOX_HEADS * QSLOT, HEAD_DIM), jnp.float32),
                        pltpu.VMEM((LANE, 2 * FOX_W), jnp.float32),
                        pltpu.VMEM((FOX_HEADS * QSLOT, 1), jnp.float32),
                        pltpu.VMEM((FOX_HEADS * QSLOT, 1), jnp.float32),
                        pltpu.VMEM((FOX_HEADS * QSLOT, HEAD_DIM), jnp.float32)],
    )
    return pl.pallas_call(
        functools.partial(_fox_decode_kernel, n_pages=n_pages, tq=tq),
        grid_spec=grid_spec,
        out_shape=jax.ShapeDtypeStruct((b, tq, FOX_W), jnp.float32),
        compiler_params=pltpu.CompilerParams(
            dimension_semantics=("arbitrary", "arbitrary"), vmem_limit_bytes=VMEM_LIMIT),
        name="fox_decode",
    )(page_table, yb3, yb3, yb3, yb3, yb3, yb3, cum_q, cum_k, pool)


def sample_forward(x, caches, page_table, w_proj, cmp_pos, cmp_w1, cmp_w2, bias_table, g_nsa, g_fox, w_out,
                   ln1_g, ln1_b, peer_w_q, peer_keys, peer_uv, ln2_g, ln2_b):
    B, T, D = x.shape
    G = NSA_KV_GROUPS
    cache_nsa, cache_win, cache_fox, cache_logf = caches
    n_pool, page = cache_nsa.shape[:2]
    x2 = x.reshape(B * T, D)
    w_big, w_small, b_small = w_proj
    yb3 = _matmul(x2, w_big).reshape(B, T, BIG_WIDTH)
    ys3 = _proj_small(x2, w_small, b_small).reshape(B, T, SMALL_WIDTH)
    logf = ys3[:, :, 2 * LANE:2 * LANE + FOX_HEADS]
    kvn = yb3[:, :, CB_KVN * LANE:CB_QF * LANE].reshape(B, T, 3, 2, G, HEAD_DIM)
    kvf = yb3[:, :, CB_KF * LANE:].reshape(B, T, 2, FOX_HEADS, HEAD_DIM)

    o_n = _nsa_decode(yb3, ys3, cache_nsa.reshape(n_pool, page * NSA_STREAMS, HEAD_DIM),
                      cache_win.reshape(B, -1, 2 * NSA_KV),
                      page_table, bias_table, cmp_pos, cmp_w1, cmp_w2)
    logf_past = cache_logf[page_table].reshape(B, -1, FOX_HEADS)
    o_f = _fox_decode(yb3, logf, cache_fox.reshape(n_pool, page * 2 * FOX_HEADS, HEAD_DIM), logf_past, page_table)
    h = _post_attention(o_n.reshape(B * T, NSA_Q), o_f.reshape(B * T, FOX_W), x2,
                        g_nsa, g_fox, w_out, ln1_g, ln1_b)
    f = _peer(h, peer_w_q, peer_keys, peer_uv)
    y = _add_ln(h, f, ln2_g, ln2_b).reshape(B, T, D)
    win_buf = jnp.concatenate([cache_win[:, T:], kvn[:, :, 2]], axis=1)
    return y, kvn[:, :, :2], win_buf, kvf, logf


def prompt_forward(x, w_proj, cmp_pos, cmp_w1, cmp_w2, bias_table, g_nsa, g_fox, w_out,
                   ln1_g, ln1_b, peer_w_q, peer_keys, peer_uv, ln2_g, ln2_b):
    B, T, D = x.shape
    G = NSA_KV_GROUPS
    x2 = x.reshape(B * T, D)
    w_big, w_small, b_small = w_proj
    yb = _matmul(x2, w_big)
    ys = _proj_small(x2, w_small, b_small)
    yb3 = yb.reshape(B, T, BIG_WIDTH)
    ys3 = ys.reshape(B, T, SMALL_WIDTH)
    logf = ys3[:, :, 2 * LANE:2 * LANE + FOX_HEADS]
    kvn = yb3[:, :, CB_KVN * LANE:CB_QF * LANE].reshape(B, T, 3, 2, G, HEAD_DIM)
    kvf = yb3[:, :, CB_KF * LANE:].reshape(B, T, 2, FOX_HEADS, HEAD_DIM)

    kcvc = _compress_prompt(yb3, cmp_pos, cmp_w1, cmp_w2)
    o_n = _nsa_prompt(yb3, ys3, kcvc, bias_table)
    o_f = _fox_prompt(yb3, logf)
    h = _post_attention(o_n.reshape(B * T, NSA_Q), o_f.reshape(B * T, FOX_W), x2,
                        g_nsa, g_fox, w_out, ln1_g, ln1_b)
    f = _peer(h, peer_w_q, peer_keys, peer_uv)
    y = _add_ln(h, f, ln2_g, ln2_b).reshape(B, T, D)
    buf_len = min(WINDOW, T)
    return y, kvn[:, :, :2], kvn[:, T - buf_len:, 2], kvf, logf


def kernel(x_prompt, x_sample, cache_nsa_kv, cache_nsa_win, cache_fox_kv, cache_fox_logf, page_table,
           w_in, b_forget, nsa_cmp_pos, nsa_cmp_w1, nsa_cmp_w2, rel_bias_table, g_nsa, g_fox, w_out,
           ln1_g, ln1_b, peer_w_q, peer_sub_keys, peer_u, peer_v, ln2_g, ln2_b):
    layer = 0
    w = (_permute_w_in(w_in[layer], b_forget[layer]), nsa_cmp_pos[layer], nsa_cmp_w1[layer], nsa_cmp_w2[layer],
         rel_bias_table, g_nsa[layer], g_fox[layer], w_out[layer], ln1_g[layer], ln1_b[layer],
         peer_w_q[layer], peer_sub_keys[layer], _pack_expert_tables(peer_u[layer], peer_v[layer]),
         ln2_g[layer], ln2_b[layer])
    yp, a_nsa, a_win, a_fox, a_logf = prompt_forward(x_prompt, *w)
    caches = (cache_nsa_kv[layer], cache_nsa_win[layer], cache_fox_kv[layer], cache_fox_logf[layer])
    ys, b_nsa, b_win, b_fox, b_logf = sample_forward(x_sample, caches, page_table, *w)
    return (yp, ys, a_nsa[None], a_win[None], a_fox[None], a_logf[None],
            b_nsa[None], b_win[None], b_fox[None], b_logf[None])
```

```python
import functools
import math

import jax
import jax.numpy as jnp
import numpy as np
from jax import lax
from jax.experimental import pallas as pl
from jax.experimental.pallas import tpu as pltpu

D_MODEL = 2048
HEAD_DIM = 128
NSA_HEADS = 8
NSA_KV_GROUPS = 2
NSA_HPG = NSA_HEADS // NSA_KV_GROUPS
CMP_BLOCK = 32
CMP_STRIDE = 16
SLC_BLOCK = 64
N_SELECT = 16
WINDOW = 512
FOX_HEADS = 8
NSA_Q = NSA_HEADS * HEAD_DIM
NSA_KV = NSA_KV_GROUPS * HEAD_DIM
FOX_W = FOX_HEADS * HEAD_DIM
IN_SIZES = (NSA_Q, 6 * NSA_KV, 3 * NSA_HEADS, FOX_W, 2 * FOX_W, FOX_HEADS)
N_BUCKETS = 32
MAX_DISTANCE = 128
PEER_HEADS = 8
N_KEYS = 128
PEER_TOPK = 16
PEER_QDIM = 256
DEPTH = 1
ALPHA = (2.0 * DEPTH) ** 0.25
LN_EPS = 1e-5
NEG = -1e30
FORCE_BONUS = 1e4
SCALE = HEAD_DIM ** -0.5

LANE = 128
VMEM_LIMIT = 48 * 1024 * 1024

BIG_WIDTH = NSA_Q + 6 * NSA_KV + FOX_W + 2 * FOX_W
CB_QN = 0
CB_KVN = NSA_Q // LANE
CB_QF = CB_KVN + 6 * NSA_KV // LANE
CB_KF = CB_QF + FOX_W // LANE
CB_VF = CB_KF + FOX_W // LANE
SMALL_WIDTH = 3 * LANE


def _dot_nt(a, b):
    return lax.dot_general(a, b, (((1,), (1,)), ((), ())), preferred_element_type=jnp.float32)


def _dot(a, b):
    return jnp.dot(a, b, preferred_element_type=jnp.float32)


def _mm_kernel(x_ref, w_ref, o_ref):
    o_ref[...] = _dot(x_ref[...].astype(jnp.bfloat16), w_ref[...])


def _matmul(x, w, tm=512, tn=512):
    m, k = x.shape
    n = w.shape[1]
    tm = min(tm, m)
    n_pad = -(-n // tn) * tn
    wb = w.astype(jnp.bfloat16)
    if n_pad != n:
        wb = jnp.pad(wb, ((0, 0), (0, n_pad - n)))
    out = pl.pallas_call(
        _mm_kernel,
        grid=(m // tm, n_pad // tn),
        in_specs=[pl.BlockSpec((tm, k), lambda i, j: (i, 0)),
                  pl.BlockSpec((k, tn), lambda i, j: (0, j))],
        out_specs=pl.BlockSpec((tm, tn), lambda i, j: (i, j)),
        out_shape=jax.ShapeDtypeStruct((m, n_pad), jnp.float32),
        compiler_params=pltpu.CompilerParams(
            dimension_semantics=("parallel", "arbitrary"),
            vmem_limit_bytes=VMEM_LIMIT),
        name="dense_matmul",
    )(x, wb)
    return out[:, :n] if n_pad != n else out


def _proj_small_kernel(x_ref, w_ref, b_ref, o_ref):
    y = _dot(x_ref[...].astype(jnp.bfloat16), w_ref[...]) + b_ref[...]
    gates = y[:, :2 * LANE]
    o_ref[:, :2 * LANE] = 1.0 / (1.0 + jnp.exp(-gates))
    f = y[:, 2 * LANE:]
    o_ref[:, 2 * LANE:] = -(jnp.maximum(-f, 0.0) + jnp.log1p(jnp.exp(-jnp.abs(f))))


def _proj_small(x, w_small, b_small, tm=512):
    m, k = x.shape
    tm = min(tm, m)
    return pl.pallas_call(
        _proj_small_kernel,
        grid=(m // tm,),
        in_specs=[pl.BlockSpec((tm, k), lambda i: (i, 0)),
                  pl.BlockSpec((k, SMALL_WIDTH), lambda i: (0, 0)),
                  pl.BlockSpec((1, SMALL_WIDTH), lambda i: (0, 0))],
        out_specs=pl.BlockSpec((tm, SMALL_WIDTH), lambda i: (i, 0)),
        out_shape=jax.ShapeDtypeStruct((m, SMALL_WIDTH), jnp.float32),
        compiler_params=pltpu.CompilerParams(
            dimension_semantics=("parallel",), vmem_limit_bytes=VMEM_LIMIT),
        name="proj_small",
    )(x, w_small, b_small)


def _permute_w_in(w_in, b_forget):
    offs = [0] + [int(o) for o in np.cumsum(IN_SIZES)]
    q_n, kv_n, gate, q_f, kv_f, f_f = (w_in[:, offs[i]:offs[i + 1]] for i in range(6))
    w_big = jnp.concatenate([q_n, kv_n, q_f, kv_f], axis=1).astype(jnp.bfloat16)
    d = w_in.shape[0]
    n_gate = 3 * NSA_HPG
    zg = jnp.zeros((d, LANE - n_gate), w_in.dtype)
    zf = jnp.zeros((d, LANE - FOX_HEADS), w_in.dtype)
    w_small = jnp.concatenate([gate[:, :n_gate], zg, gate[:, n_gate:], zg, f_f, zf], axis=1).astype(jnp.bfloat16)
    b_small = jnp.concatenate([jnp.zeros((2 * LANE,), jnp.float32), b_forget.astype(jnp.float32),
                               jnp.zeros((LANE - FOX_HEADS,), jnp.float32)])[None]
    return w_big, w_small, b_small


def _gelu_tanh(h):
    return 0.5 * h * (1.0 + jnp.tanh(math.sqrt(2.0 / math.pi) * (h + 0.044715 * (h * h * h))))


def _compress_kernel(k_ref, pe_ref, w1_ref, w2_ref, o_ref, *, nh):
    def half(s0):
        acc = jnp.zeros((nh, HEAD_DIM), jnp.float32)
        for s in range(CMP_STRIDE):
            rows = k_ref[0, pl.ds(s, nh, stride=CMP_STRIDE), :] + pe_ref[0, s0 + s:s0 + s + 1, :]
            acc = acc + _dot(rows.astype(jnp.bfloat16), w1_ref[0, s0 + s])
        return acc

    first = half(0)
    second = half(CMP_STRIDE)
    h = first + pltpu.roll(second, nh - 1, 0)
    o_ref[0, 0] = _dot(_gelu_tanh(h).astype(jnp.bfloat16), w2_ref[0])


def _compress_prompt(yb3, cmp_pos, cmp_w1, cmp_w2):
    b, t, _ = yb3.shape
    nh = t // CMP_STRIDE
    n_kg = 2 * NSA_KV_GROUPS
    return pl.pallas_call(
        functools.partial(_compress_kernel, nh=nh),
        grid=(b, n_kg),
        in_specs=[pl.BlockSpec((1, t, LANE), lambda i, c: (i, 0, CB_KVN + c)),
                  pl.BlockSpec((1, CMP_BLOCK, HEAD_DIM), lambda i, c: (c // NSA_KV_GROUPS, 0, 0)),
                  pl.BlockSpec((1, CMP_BLOCK, HEAD_DIM, HEAD_DIM), lambda i, c: (c // NSA_KV_GROUPS, 0, 0, 0)),
                  pl.BlockSpec((1, HEAD_DIM, HEAD_DIM), lambda i, c: (c // NSA_KV_GROUPS, 0, 0))],
        out_specs=pl.BlockSpec((1, 1, nh, HEAD_DIM), lambda i, c: (i, c, 0, 0)),
        out_shape=jax.ShapeDtypeStruct((b, n_kg, nh, HEAD_DIM), jnp.float32),
        compiler_params=pltpu.CompilerParams(
            dimension_semantics=("parallel", "arbitrary"), vmem_limit_bytes=VMEM_LIMIT),
        name="nsa_compress",
    )(yb3, cmp_pos, cmp_w1.astype(jnp.bfloat16), cmp_w2.astype(jnp.bfloat16))


SLC_PAIR = 2


def _nsa_kernel(q_ref, kc_ref, vc_ref, ks_ref, vs_ref, kw_ref, vw_ref, gate_ref, biasc_ref, biast_ref,
                c2s_ref, expand_ref, o_ref, q_s, sel_s, m_s, l_s, acc_s, out_s, *, n_slc, n_top):
    i = pl.program_id(2)
    rows_q = NSA_HPG * LANE
    bf16 = jnp.bfloat16

    qt = q_ref[0]
    q_s[...] = jnp.concatenate([qt[:, h * LANE:(h + 1) * LANE] for h in range(NSA_HPG)], axis=0).astype(bf16)

    r = lax.broadcasted_iota(jnp.int32, (rows_q, LANE), 0) & (LANE - 1)
    c = lax.broadcasted_iota(jnp.int32, (rows_q, LANE), 1)
    q_pos = i * LANE + r
    s = _dot_nt(q_s[...], kc_ref[0, 0].astype(bf16)) * SCALE + biasc_ref[...].reshape(rows_q, LANE)
    mask = c * CMP_STRIDE + (CMP_BLOCK - 1) <= q_pos
    s = jnp.where(mask, s, NEG)
    p = jnp.where(mask, jnp.exp(s - jnp.max(s, -1, keepdims=True)), 0.0)
    p = p / jnp.maximum(jnp.sum(p, -1, keepdims=True), 1e-30)
    out_s[0] = _dot(p.astype(bf16), vc_ref[0, 0].astype(bf16))

    p_sum = p[0:LANE]
    for h in range(1, NSA_HPG):
        p_sum = p_sum + p[h * LANE:(h + 1) * LANE]
    p_hi = p_sum.astype(bf16)
    p_lo = (p_sum - p_hi.astype(jnp.float32)).astype(bf16)
    imp = _dot(p_hi, c2s_ref[...]) + _dot(p_lo, c2s_ref[...])
    jj = lax.broadcasted_iota(jnp.int32, (LANE, LANE), 1)
    qp = i * LANE + lax.broadcasted_iota(jnp.int32, (LANE, LANE), 0)
    cur = qp >> int(math.log2(SLC_BLOCK))
    forced = (jj == 0) | (jj == cur) | (jj == cur - 1)
    val = jnp.where(jj * SLC_BLOCK <= qp, imp + jnp.where(forced, FORCE_BONUS, 0.0), NEG)
    val = jnp.where(jj < n_slc, val, -3e38)
    rank = jnp.zeros((LANE, LANE), jnp.int32)
    for t in range(n_slc):
        col = val[:, t:t + 1]
        ahead = (col > val) | ((col == val) & (jj > t))
        rank = rank + ahead.astype(jnp.int32)
    sel_s[...] = jnp.where(rank < n_top, 1.0, 0.0).astype(bf16)

    def attend(k_ref, v_ref, lo, w, use_sel, use_win, slot):
        m_s[...] = jnp.full((rows_q, 1), NEG, jnp.float32)
        l_s[...] = jnp.zeros((rows_q, 1), jnp.float32)
        acc_s[...] = jnp.zeros((rows_q, HEAD_DIM), jnp.float32)

        n_tiles = i + 1 - lo

        def body(j, carry):
            k0 = lo + w * j
            off = pl.multiple_of(k0 * LANE, LANE)
            k = k_ref[0, pl.ds(off, w * LANE), :].astype(bf16)
            v = v_ref[0, pl.ds(off, w * LANE), :].astype(bf16)
            dq = i - k0
            bias = [biast_ref[0, jnp.clip(dq - t, 0, 2)] for t in range(w)]
            sc = _dot_nt(q_s[...], k) * SCALE + (bias[0] if w == 1 else jnp.concatenate(bias, axis=1))
            rr = lax.broadcasted_iota(jnp.int32, (rows_q, w * LANE), 0) & (LANE - 1)
            cc = lax.broadcasted_iota(jnp.int32, (rows_q, w * LANE), 1)
            dist = rr - cc + dq * LANE
            msk = dist >= 0
            if use_win:
                msk = msk & (dist <= WINDOW)
            if use_sel:
                se = [_dot(sel_s[...], expand_ref[k0 + t]) for t in range(w)]
                se = se[0] if w == 1 else jnp.concatenate(se, axis=1)
                msk = msk & (jnp.concatenate([se] * NSA_HPG, axis=0) > 0.5)
            sc = jnp.where(msk, sc, NEG)
            m_old = m_s[...]
            m_new = jnp.maximum(m_old, jnp.max(sc, -1, keepdims=True))
            pe = jnp.where(msk, jnp.exp(sc - m_new), 0.0)
            alpha = jnp.exp(m_old - m_new)
            l_s[...] = alpha * l_s[...] + jnp.sum(pe, -1, keepdims=True)
            acc_s[...] = alpha * acc_s[...] + _dot(pe.astype(bf16), v)
            m_s[...] = m_new
            return carry

        assert w in (1, 2)
        lax.fori_loop(0, n_tiles if w == 1 else (n_tiles + 1) >> 1, body, 0)
        out_s[slot] = acc_s[...] / jnp.maximum(l_s[...], 1e-30)

    attend(ks_ref, vs_ref, 0, SLC_PAIR, True, False, 1)
    attend(kw_ref, vw_ref, jnp.maximum(i - WINDOW // LANE, 0), 1, False, True, 2)

    gt = gate_ref[0]
    for h in range(NSA_HPG):
        sl = slice(h * LANE, (h + 1) * LANE)
        o_ref[0, :, sl] = (gt[:, 3 * h:3 * h + 1] * out_s[0, sl, :]
                           + gt[:, 3 * h + 1:3 * h + 2] * out_s[1, sl, :]
                           + gt[:, 3 * h + 2:3 * h + 3] * out_s[2, sl, :])


def _t5_bucket_np(d):
    max_exact = N_BUCKETS // 2
    d = np.maximum(d, 0)
    large = max_exact + (np.log(np.maximum(d, 1).astype(np.float32) / np.float32(max_exact))
                         / np.float32(math.log(MAX_DISTANCE / max_exact)) * (N_BUCKETS - max_exact)).astype(np.int32)
    return np.where(d < max_exact, d, np.minimum(large, N_BUCKETS - 1)).astype(np.int32)


def _nsa_prompt(yb3, ys3, kcvc, bias_table):
    b, t, _ = yb3.shape
    n_t = t // LANE
    n_slc = t // SLC_BLOCK
    n_cmp = (t - CMP_BLOCK) // CMP_STRIDE + 1
    n_top = min(N_SELECT, n_slc)
    rows_q = NSA_HPG * LANE
    assert t % LANE == 0 and n_cmp <= LANE and n_slc <= LANE and n_t % SLC_PAIR == 0

    table = bias_table.astype(jnp.float32)
    dist_c = np.arange(t)[:, None] - (np.arange(LANE) * CMP_STRIDE + CMP_BLOCK - 1)[None, :]
    bias_c = jnp.moveaxis(table[_t5_bucket_np(dist_c)], -1, 0)
    rc = np.arange(LANE)[:, None] - np.arange(LANE)[None, :]
    buckets_t = np.stack([_t5_bucket_np(rc), _t5_bucket_np(rc + LANE), _t5_bucket_np(rc + 2 * LANE)])
    assert (_t5_bucket_np(np.arange(LANE + 1, 4 * LANE)) == N_BUCKETS - 1).all()
    bias_t = jnp.moveaxis(table[buckets_t], -1, 0)
    bias_t = bias_t.reshape(NSA_KV_GROUPS, NSA_HPG, 3, LANE, LANE).transpose(0, 2, 1, 3, 4)
    bias_t = bias_t.reshape(NSA_KV_GROUPS, 3, rows_q, LANE)

    c0 = np.arange(n_cmp) * CMP_STRIDE
    s0 = np.arange(n_slc) * SLC_BLOCK
    ov = np.minimum(c0[:, None] + CMP_BLOCK, s0[None, :] + SLC_BLOCK) - np.maximum(c0[:, None], s0[None, :])
    c2s = np.zeros((LANE, LANE), np.float32)
    c2s[:n_cmp, :n_slc] = np.maximum(ov, 0) / CMP_STRIDE
    expand = np.zeros((n_t, LANE, LANE), np.float32)
    for kt in range(n_t):
        tok_blk = (kt * LANE + np.arange(LANE)) // SLC_BLOCK
        expand[kt, tok_blk, np.arange(LANE)] = 1.0

    kv_spec = lambda cb: pl.BlockSpec((1, t, LANE), lambda bi, g, i: (bi, 0, cb + g))
    g_n = NSA_KV_GROUPS
    return pl.pallas_call(
        functools.partial(_nsa_kernel, n_slc=n_slc, n_top=n_top),
        grid=(b, g_n, n_t),
        in_specs=[
            pl.BlockSpec((1, LANE, rows_q), lambda bi, g, i: (bi, i, g)),
            pl.BlockSpec((1, 1, t // CMP_STRIDE, HEAD_DIM), lambda bi, g, i: (bi, g, 0, 0)),
            pl.BlockSpec((1, 1, t // CMP_STRIDE, HEAD_DIM), lambda bi, g, i: (bi, g_n + g, 0, 0)),
            kv_spec(CB_KVN + 2 * g_n), kv_spec(CB_KVN + 3 * g_n),
            kv_spec(CB_KVN + 4 * g_n), kv_spec(CB_KVN + 5 * g_n),
            pl.BlockSpec((1, LANE, LANE), lambda bi, g, i: (bi, i, g)),
            pl.BlockSpec((NSA_HPG, LANE, LANE), lambda bi, g, i: (g, i, 0)),
            pl.BlockSpec((1, 3, rows_q, LANE), lambda bi, g, i: (g, 0, 0, 0)),
            pl.BlockSpec((LANE, LANE), lambda bi, g, i: (0, 0)),
            pl.BlockSpec((n_t, LANE, LANE), lambda bi, g, i: (0, 0, 0)),
        ],
        out_specs=pl.BlockSpec((1, LANE, rows_q), lambda bi, g, i: (bi, i, g)),
        out_shape=jax.ShapeDtypeStruct((b, t, NSA_Q), jnp.float32),
        scratch_shapes=[pltpu.VMEM((rows_q, HEAD_DIM), jnp.bfloat16),
                        pltpu.VMEM((LANE, LANE), jnp.bfloat16),
                        pltpu.VMEM((rows_q, 1), jnp.float32),
                        pltpu.VMEM((rows_q, 1), jnp.float32),
                        pltpu.VMEM((rows_q, HEAD_DIM), jnp.float32),
                        pltpu.VMEM((3, rows_q, HEAD_DIM), jnp.float32)],
        compiler_params=pltpu.CompilerParams(
            dimension_semantics=("parallel", "parallel", "arbitrary"), vmem_limit_bytes=VMEM_LIMIT),
        name="nsa_prompt",
    )(yb3, kcvc, kcvc, yb3, yb3, yb3, yb3, ys3, bias_c, bias_t,
      jnp.asarray(c2s, jnp.bfloat16), jnp.asarray(expand, jnp.bfloat16))


FOX_TILE = 256
FOX_KTILE = 512


def _fox_kernel(q_ref, k_ref, v_ref, cq_ref, ck_ref, o_ref, m_s, l_s, acc_s):
    i = pl.program_id(2)
    tq = FOX_TILE
    bf16 = jnp.bfloat16
    q = q_ref[0].astype(bf16)
    cq = cq_ref[0, 0]
    m_s[...] = jnp.full((tq, 1), NEG, jnp.float32)
    l_s[...] = jnp.zeros((tq, 1), jnp.float32)
    acc_s[...] = jnp.zeros((tq, HEAD_DIM), jnp.float32)

    tk = FOX_KTILE

    def body(kt, carry):
        off = pl.multiple_of(kt * tk, tk)
        k = k_ref[0, pl.ds(off, tk), :].astype(bf16)
        v = v_ref[0, pl.ds(off, tk), :].astype(bf16)
        sc = _dot_nt(q, k) * SCALE + cq - ck_ref[0, 0, kt]
        rr = lax.broadcasted_iota(jnp.int32, (tq, tk), 0)
        cc = lax.broadcasted_iota(jnp.int32, (tq, tk), 1)
        msk = cc + kt * tk <= rr + i * tq
        sc = jnp.where(msk, sc, NEG)
        m_old = m_s[...]
        m_new = jnp.maximum(m_old, jnp.max(sc, -1, keepdims=True))
        pe = jnp.where(msk, jnp.exp(sc - m_new), 0.0)
        alpha = jnp.exp(m_old - m_new)
        l_s[...] = alpha * l_s[...] + jnp.sum(pe, -1, keepdims=True)
        acc_s[...] = alpha * acc_s[...] + _dot(pe.astype(bf16), v)
        m_s[...] = m_new
        return carry

    lax.fori_loop(0, ((i * tq) >> int(math.log2(tk))) + 1, body, 0)
    o_ref[0] = acc_s[...] / jnp.maximum(l_s[...], 1e-30)


def _fox_prompt(yb3, logf):
    b, t, _ = yb3.shape
    tq = FOX_TILE
    n_t = t // tq
    cum = jnp.moveaxis(jnp.cumsum(logf.astype(jnp.float32), axis=1), 1, 2)
    cum_q = cum[..., None]
    tk = FOX_KTILE
    assert t % tk == 0 and tk % tq == 0
    cum_k = cum.reshape(b, FOX_HEADS, t // tk, 1, tk)
    return pl.pallas_call(
        _fox_kernel,
        grid=(b, FOX_HEADS, n_t),
        in_specs=[pl.BlockSpec((1, tq, LANE), lambda bi, h, i: (bi, i, CB_QF + h)),
                  pl.BlockSpec((1, t, LANE), lambda bi, h, i: (bi, 0, CB_KF + h)),
                  pl.BlockSpec((1, t, LANE), lambda bi, h, i: (bi, 0, CB_VF + h)),
                  pl.BlockSpec((1, 1, tq, 1), lambda bi, h, i: (bi, h, i, 0)),
                  pl.BlockSpec((1, 1, t // tk, 1, tk), lambda bi, h, i: (bi, h, 0, 0, 0))],
        out_specs=pl.BlockSpec((1, tq, LANE), lambda bi, h, i: (bi, i, h)),
        out_shape=jax.ShapeDtypeStruct((b, t, FOX_W), jnp.float32),
        scratch_shapes=[pltpu.VMEM((tq, 1), jnp.float32),
                        pltpu.VMEM((tq, 1), jnp.float32),
                        pltpu.VMEM((tq, HEAD_DIM), jnp.float32)],
        compiler_params=pltpu.CompilerParams(
            dimension_semantics=("parallel", "parallel", "arbitrary"), vmem_limit_bytes=VMEM_LIMIT),
        name="fox_prompt",
    )(yb3, yb3, yb3, cum_q, cum_k)


def _ln(z, g, b):
    mu = jnp.mean(z, -1, keepdims=True)
    zc = z - mu
    var = jnp.mean(zc * zc, -1, keepdims=True)
    return zc * lax.rsqrt(var + LN_EPS) * g + b


def _post_kernel(on_ref, of_ref, x_ref, gn_ref, gf_ref, w_ref, lg_ref, lb_ref, h_ref):
    def rms(o, g):
        return (o * lax.rsqrt(jnp.mean(o * o, -1, keepdims=True) + LN_EPS) * g).astype(jnp.bfloat16)

    mix = (_dot(rms(on_ref[...], gn_ref[...]), w_ref[:NSA_Q, :])
           + _dot(rms(of_ref[...], gf_ref[...]), w_ref[NSA_Q:, :]))
    h_ref[...] = _ln(ALPHA * x_ref[...] + mix, lg_ref[...], lb_ref[...])


def _post_attention(o_n, o_f, x, g_nsa, g_fox, w_out, ln_g, ln_b, tm=256):
    m, d = x.shape
    tm = min(tm, m)
    row = lambda n: pl.BlockSpec((1, n), lambda i: (0, 0))
    return pl.pallas_call(
        _post_kernel,
        grid=(m // tm,),
        in_specs=[pl.BlockSpec((tm, NSA_Q), lambda i: (i, 0)),
                  pl.BlockSpec((tm, FOX_W), lambda i: (i, 0)),
                  pl.BlockSpec((tm, d), lambda i: (i, 0)),
                  row(NSA_Q), row(FOX_W),
                  pl.BlockSpec((NSA_Q + FOX_W, d), lambda i: (0, 0)),
                  row(d), row(d)],
        out_specs=pl.BlockSpec((tm, d), lambda i: (i, 0)),
        out_shape=jax.ShapeDtypeStruct((m, d), jnp.float32),
        compiler_params=pltpu.CompilerParams(
            dimension_semantics=("parallel",), vmem_limit_bytes=VMEM_LIMIT),
        name="post_attention",
    )(o_n, o_f, x, g_nsa[None], g_fox[None], w_out.astype(jnp.bfloat16), ln_g[None], ln_b[None])


def _add_ln_kernel(h_ref, f_ref, g_ref, b_ref, o_ref):
    o_ref[...] = _ln(ALPHA * h_ref[...] + f_ref[...], g_ref[...], b_ref[...])


def _add_ln(h, f, ln_g, ln_b, tm=512):
    m, d = h.shape
    tm = min(tm, m)
    return pl.pallas_call(
        _add_ln_kernel,
        grid=(m // tm,),
        in_specs=[pl.BlockSpec((tm, d), lambda i: (i, 0)),
                  pl.BlockSpec((tm, d), lambda i: (i, 0)),
                  pl.BlockSpec((1, d), lambda i: (0, 0)),
                  pl.BlockSpec((1, d), lambda i: (0, 0))],
        out_specs=pl.BlockSpec((tm, d), lambda i: (i, 0)),
        out_shape=jax.ShapeDtypeStruct((m, d), jnp.float32),
        compiler_params=pltpu.CompilerParams(
            dimension_semantics=("parallel",), vmem_limit_bytes=VMEM_LIMIT),
        name="add_layer_norm",
    )(h, f, ln_g[None], ln_b[None])


PEER_TILE = 128
N_ROUTES = PEER_HEADS * PEER_TOPK


def _top_rows(vals, row_id, n_out, payload=None):
    big = float(vals.shape[0])
    out_v, out_i = [], []
    for _ in range(n_out):
        m = jnp.max(vals, axis=0, keepdims=True)
        win = jnp.min(jnp.where(vals == m, row_id, big), axis=0, keepdims=True)
        hit = row_id == win
        out_v.append(m)
        if payload is None:
            out_i.append(win)
        else:
            out_i.append(jnp.sum(jnp.where(hit, payload, 0.0), axis=0, keepdims=True))
        vals = jnp.where(hit, -jnp.inf, vals)
    return jnp.concatenate(out_v, axis=0), jnp.concatenate(out_i, axis=0)


def _peer_route_kernel(h_ref, wq_ref, keys_ref, g_ref, e_ref, sv_s, si_s):
    bf16 = jnp.bfloat16
    tm = PEER_TILE
    half = PEER_QDIM // 2
    q = _dot(h_ref[...].astype(bf16), wq_ref[...]).astype(bf16)
    key_id = lax.broadcasted_iota(jnp.int32, (N_KEYS, tm), 0).astype(jnp.float32)
    for hp in range(2 * PEER_HEADS):
        s_t = _dot_nt(keys_ref[hp], q[:, hp * half:(hp + 1) * half])
        sv, si = _top_rows(s_t, key_id, PEER_TOPK)
        sv_s[hp] = sv
        si_s[hp] = si
    pair_id = lax.broadcasted_iota(jnp.int32, (PEER_TOPK * PEER_TOPK, tm), 0).astype(jnp.float32)
    for h in range(PEER_HEADS):
        sv0, sv1 = sv_s[2 * h], sv_s[2 * h + 1]
        si0, si1 = si_s[2 * h], si_s[2 * h + 1]
        cand = jnp.concatenate([sv0[a:a + 1, :] + sv1 for a in range(PEER_TOPK)], axis=0)
        expert = jnp.concatenate([si0[a:a + 1, :] * float(N_KEYS) + si1 for a in range(PEER_TOPK)], axis=0)
        best, eid = _top_rows(cand, pair_id, PEER_TOPK, payload=expert)
        ex = jnp.exp(best - best[0:1, :])
        g_ref[0, h * PEER_TOPK:(h + 1) * PEER_TOPK, :] = ex / jnp.sum(ex, axis=0, keepdims=True)
        e_ref[0, h * PEER_TOPK:(h + 1) * PEER_TOPK, :] = eid.astype(jnp.int32)


def _peer_route(h, w_q, sub_keys):
    n, d = h.shape
    tm = PEER_TILE
    nb = n // tm
    n_hp = 2 * PEER_HEADS
    half = PEER_QDIM // 2
    out = jax.ShapeDtypeStruct((nb, N_ROUTES, tm), jnp.float32)
    return pl.pallas_call(
        _peer_route_kernel,
        grid=(nb,),
        in_specs=[pl.BlockSpec((tm, d), lambda i: (i, 0)),
                  pl.BlockSpec((d, PEER_HEADS * PEER_QDIM), lambda i: (0, 0)),
                  pl.BlockSpec((n_hp, N_KEYS, half), lambda i: (0, 0, 0))],
        out_specs=[pl.BlockSpec((1, N_ROUTES, tm), lambda i: (i, 0, 0)),
                   pl.BlockSpec((1, N_ROUTES, tm), lambda i: (i, 0, 0))],
        out_shape=[out, jax.ShapeDtypeStruct((nb, N_ROUTES, tm), jnp.int32)],
        scratch_shapes=[pltpu.VMEM((n_hp, PEER_TOPK, tm), jnp.float32),
                        pltpu.VMEM((n_hp, PEER_TOPK, tm), jnp.float32)],
        compiler_params=pltpu.CompilerParams(
            dimension_semantics=("parallel",), vmem_limit_bytes=VMEM_LIMIT),
        name="peer_route",
    )(h, w_q.astype(jnp.bfloat16), sub_keys.reshape(n_hp, N_KEYS, half).astype(jnp.bfloat16))


PACK_DTYPE = jnp.dtype("bfloat16")
PEER_SLOTS = 4
PEER_AHEAD = PEER_SLOTS - 1


def _pack_expert_tables(u, v):
    def pack(t):
        bits = lax.bitcast_convert_type(t.astype(PACK_DTYPE), jnp.uint16).astype(jnp.uint32)
        half = t.shape[1] // 2
        return bits[:, :half] | (bits[:, half:] << 16)
    return jnp.concatenate([pack(u), pack(v)], axis=1)


def _unpack_words(w):
    lo = lax.bitcast_convert_type(w << 16, jnp.float32)
    hi = lax.bitcast_convert_type(w & jnp.uint32(0xFFFF0000), jnp.float32)
    return lo, hi


def _peer_expert_kernel(e_ref, g_ref, x_ref, uv_hbm, o_ref, e_smem, buf, sem, esem):
    tm = PEER_TILE
    hw = D_MODEL // 2
    n_chunk = hw // LANE
    per_point = N_ROUTES // (2 * n_chunk)
    ids = pltpu.make_async_copy(e_ref.at[0], e_smem, esem)
    ids.start()
    ids.wait()

    def issue(t, slot, ks):
        for k in ks:
            pltpu.make_async_copy(uv_hbm.at[pl.ds(e_smem[k, t], 1), :],
                                  buf.at[slot, pl.ds(k, 1), :], sem.at[slot]).start()

    def wait_rows(slot):
        pltpu.make_async_copy(uv_hbm.at[pl.ds(0, N_ROUTES), :], buf.at[slot], sem.at[slot]).wait()

    def compute(t, slot, t_ahead):
        other = (slot + PEER_AHEAD) % PEER_SLOTS
        point = [0]

        def issue_some():
            issue(t_ahead, other, range(point[0] * per_point, (point[0] + 1) * per_point))
            point[0] += 1

        x_row = x_ref[pl.ds(t, 1), :]
        acc = None
        for c in range(n_chunk):
            lo, hi = _unpack_words(buf[slot, :, c * LANE:(c + 1) * LANE])
            term = lo * x_row[:, c * LANE:(c + 1) * LANE] + hi * x_row[:, hw + c * LANE:hw + (c + 1) * LANE]
            acc = term if acc is None else acc + term
            issue_some()
        s = jnp.sum(acc, axis=1, keepdims=True)
        gate = pltpu.roll(g_ref[0], jnp.where(t == 0, 0, tm - t), 1)[:, 0:1]
        coef = gate * _gelu_tanh(s)
        lo_out, hi_out = [], []
        for c in range(n_chunk):
            lo, hi = _unpack_words(buf[slot, :, hw + c * LANE:hw + (c + 1) * LANE])
            lo_out.append(jnp.sum(lo * coef, axis=0, keepdims=True))
            hi_out.append(jnp.sum(hi * coef, axis=0, keepdims=True))
            issue_some()
        o_ref[pl.ds(t, 1), :] = jnp.concatenate(lo_out + hi_out, axis=1)

    for j in range(PEER_AHEAD):
        issue(j, j, range(N_ROUTES))

    def body(i, carry):
        for j in range(PEER_SLOTS):
            t = PEER_SLOTS * i + j
            wait_rows(j)
            compute(t, j, jnp.minimum(t + PEER_AHEAD, tm - 1))
        return carry

    lax.fori_loop(0, tm // PEER_SLOTS, body, 0)
    for j in range(PEER_AHEAD):
        wait_rows((tm + j) % PEER_SLOTS)


def _peer_experts(h, gates, experts, uv):
    n, d = h.shape
    tm = PEER_TILE
    nb = n // tm
    return pl.pallas_call(
        _peer_expert_kernel,
        grid=(nb,),
        in_specs=[pl.BlockSpec((1, N_ROUTES, tm), lambda i: (i, 0, 0)),
                  pl.BlockSpec((1, N_ROUTES, tm), lambda i: (i, 0, 0)),
                  pl.BlockSpec((tm, d), lambda i: (i, 0)),
                  pl.BlockSpec(memory_space=pl.ANY)],
        out_specs=pl.BlockSpec((tm, d), lambda i: (i, 0)),
        out_shape=jax.ShapeDtypeStruct((n, d), jnp.float32),
        scratch_shapes=[pltpu.SMEM((N_ROUTES, tm), jnp.int32),
                        pltpu.VMEM((PEER_SLOTS, N_ROUTES, d), jnp.uint32),
                        pltpu.SemaphoreType.DMA((PEER_SLOTS,)),
                        pltpu.SemaphoreType.DMA],
        compiler_params=pltpu.CompilerParams(
            dimension_semantics=("arbitrary",), vmem_limit_bytes=VMEM_LIMIT),
        name="peer_experts",
    )(experts, gates, h, uv)


def _peer(h, w_q, sub_keys, uv):
    gates, experts = _peer_route(h, w_q, sub_keys)
    return _peer_experts(h, gates, experts, uv)


QSLOT = 8
DEC_GROUP = 4
NSA_STREAMS = 2 * 2 * NSA_KV_GROUPS


def _nsa_decode_kernel(pt_ref, q_ref, ks_new_ref, vs_new_ref, kw_new_ref, vw_new_ref, kwin_ref, vwin_ref,
                       gate_ref, biasc_ref, biast_ref, c2s_ref, expand_ref, pe_ref, w1_ref, w2_ref, pool_hbm,
                       o_ref, kbuf, nbuf, wbuf, kc_s, q_s, sel_s, m_s, l_s, acc_s, out_s, sem,
                       *, n_pages, page, tq, n_slc, n_top):
    b = pl.program_id(0)
    g = pl.program_id(1)
    bf16 = jnp.bfloat16
    rows_q = NSA_HPG * QSLOT
    p0 = n_pages * page
    n_half = p0 // CMP_STRIDE
    i_slc = p0 // LANE
    i_win = WINDOW // LANE

    def page_copy(p):
        return pltpu.make_async_copy(pool_hbm.at[pt_ref[b, p]],
                                     kbuf.at[pl.ds(p * page * NSA_STREAMS, page * NSA_STREAMS), :], sem.at[0])

    @pl.when(g == 0)
    def _():
        for p in range(n_pages):
            page_copy(p).start()
        for p in range(n_pages):
            page_copy(p).wait()

    def cached_rows(c, first, count, step=1):
        return kbuf[pl.ds(first * NSA_STREAMS + 2 * c + g, count, stride=step * NSA_STREAMS), :]

    q_s[...] = jnp.zeros((rows_q, HEAD_DIM), jnp.float32)
    qt = q_ref[0]
    for h in range(NSA_HPG):
        q_s[h * QSLOT:h * QSLOT + tq, :] = qt[:, h * LANE:(h + 1) * LANE]
    zeros_tile = jnp.zeros((LANE, HEAD_DIM), jnp.float32)
    wbuf[0, 0:WINDOW, :] = kwin_ref[0]
    wbuf[1, 0:WINDOW, :] = vwin_ref[0]
    wbuf[0, WINDOW:WINDOW + LANE, :] = zeros_tile
    wbuf[1, WINDOW:WINDOW + LANE, :] = zeros_tile
    wbuf[0, WINDOW:WINDOW + tq, :] = kw_new_ref[0]
    wbuf[1, WINDOW:WINDOW + tq, :] = vw_new_ref[0]

    for kv in range(2):
        def half(s0):
            acc = jnp.zeros((n_half, HEAD_DIM), jnp.float32)
            for s in range(CMP_STRIDE):
                rows = cached_rows(kv, s, n_half, CMP_STRIDE) + pe_ref[kv, s0 + s:s0 + s + 1, :]
                acc = acc + _dot(rows.astype(bf16), w1_ref[kv, s0 + s])
            return acc

        first = half(0)
        second = half(CMP_STRIDE)
        hmid = first + pltpu.roll(second, n_half - 1, 0)
        kc_s[kv] = _dot(_gelu_tanh(hmid).astype(bf16), w2_ref[kv])

    qb = q_s[...].astype(bf16)
    r = lax.broadcasted_iota(jnp.int32, (rows_q, n_half), 0) & (QSLOT - 1)
    c = lax.broadcasted_iota(jnp.int32, (rows_q, n_half), 1)
    s = _dot_nt(qb, kc_s[0].astype(bf16)) * SCALE + biasc_ref[...].reshape(rows_q, n_half)
    mask = c * CMP_STRIDE + (CMP_BLOCK - 1) <= p0 + r
    s = jnp.where(mask, s, NEG)
    p = jnp.where(mask, jnp.exp(s - jnp.max(s, -1, keepdims=True)), 0.0)
    p = p / jnp.maximum(jnp.sum(p, -1, keepdims=True), 1e-30)
    out_s[0] = _dot(p.astype(bf16), kc_s[1].astype(bf16))

    p_sum = p[0:QSLOT]
    for h in range(1, NSA_HPG):
        p_sum = p_sum + p[h * QSLOT:(h + 1) * QSLOT]
    p_hi = p_sum.astype(bf16)
    p_lo = (p_sum - p_hi.astype(jnp.float32)).astype(bf16)
    imp = _dot(p_hi, c2s_ref[...]) + _dot(p_lo, c2s_ref[...])
    wide = 2 * LANE
    jj = lax.broadcasted_iota(jnp.int32, (QSLOT, wide), 1)
    qp = p0 + lax.broadcasted_iota(jnp.int32, (QSLOT, wide), 0)
    cur = qp >> int(math.log2(SLC_BLOCK))
    forced = (jj == 0) | (jj == cur) | (jj == cur - 1)
    val = jnp.where(jj * SLC_BLOCK <= qp, imp + jnp.where(forced, FORCE_BONUS, 0.0), NEG)
    val = jnp.where(jj < n_slc, val, -3e38)
    rank = jnp.zeros((QSLOT, wide), jnp.int32)
    for t in range(n_slc):
        col = val[:, t:t + 1]
        ahead = (col > val) | ((col == val) & (jj > t))
        rank = rank + ahead.astype(jnp.int32)
    sel_s[...] = jnp.where(rank < n_top, 1.0, 0.0)[:, :LANE]

    def attend(kv_at, i_tile, use_sel, use_win, slot):
        m_s[...] = jnp.full((rows_q, 1), NEG, jnp.float32)
        l_s[...] = jnp.zeros((rows_q, 1), jnp.float32)
        acc_s[...] = jnp.zeros((rows_q, HEAD_DIM), jnp.float32)

        def tiles(k0, w, with_sel):
            k, v = kv_at(k0, w)
            k = k.astype(bf16)
            v = v.astype(bf16)
            bias = [biast_ref[0, jnp.minimum(i_tile - k0 - j, 2)] for j in range(w)]
            sc = _dot_nt(qb, k) * SCALE + (bias[0] if w == 1 else jnp.concatenate(bias, axis=1))
            rr = lax.broadcasted_iota(jnp.int32, (rows_q, w * LANE), 0) & (QSLOT - 1)
            cc = lax.broadcasted_iota(jnp.int32, (rows_q, w * LANE), 1)
            dist = rr - cc + (i_tile - k0) * LANE
            msk = dist >= 0
            if use_win:
                msk = msk & (dist <= WINDOW)
            if with_sel:
                sb = sel_s[...].astype(bf16)
                se = jnp.concatenate([_dot(sb, expand_ref[k0 + j]) for j in range(w)], axis=1)
                msk = msk & (jnp.concatenate([se] * NSA_HPG, axis=0) > 0.5)
            sc = jnp.where(msk, sc, NEG)
            m_old = m_s[...]
            m_new = jnp.maximum(m_old, jnp.max(sc, -1, keepdims=True))
            pe = jnp.where(msk, jnp.exp(sc - m_new), 0.0)
            alpha = jnp.exp(m_old - m_new)
            l_s[...] = alpha * l_s[...] + jnp.sum(pe, -1, keepdims=True)
            acc_s[...] = alpha * acc_s[...] + _dot(pe.astype(bf16), v)
            m_s[...] = m_new

        def body(grp, carry):
            tiles(grp * DEC_GROUP, DEC_GROUP, use_sel)
            return carry

        lax.fori_loop(0, i_tile // DEC_GROUP, body, 0)
        tiles(i_tile, 1, False)
        out_s[slot] = acc_s[...] / jnp.maximum(l_s[...], 1e-30)

    nbuf[0] = zeros_tile
    nbuf[1] = zeros_tile
    nbuf[0, 0:tq, :] = ks_new_ref[0]
    nbuf[1, 0:tq, :] = vs_new_ref[0]

    def slc_at(k0, w):
        if w == 1:
            return nbuf[0], nbuf[1]
        return cached_rows(2, k0 * LANE, w * LANE), cached_rows(3, k0 * LANE, w * LANE)

    def win_at(k0, w):
        off = pl.multiple_of(k0 * LANE, LANE)
        return wbuf[0, pl.ds(off, w * LANE), :], wbuf[1, pl.ds(off, w * LANE), :]

    attend(slc_at, i_slc, True, False, 1)
    attend(win_at, i_win, False, True, 2)

    gt = gate_ref[0]
    for h in range(NSA_HPG):
        sl = slice(h * QSLOT, h * QSLOT + tq)
        o_ref[0, :, h * LANE:(h + 1) * LANE] = (gt[:, 3 * h:3 * h + 1] * out_s[0, sl, :]
                                                 + gt[:, 3 * h + 1:3 * h + 2] * out_s[1, sl, :]
                                                 + gt[:, 3 * h + 2:3 * h + 3] * out_s[2, sl, :])


def _nsa_decode(yb3, ys3, pool, win_cache, page_table, bias_table, cmp_pos, cmp_w1, cmp_w2):
    b, tq, _ = yb3.shape
    n_pages = page_table.shape[1]
    page = pool.shape[1] // NSA_STREAMS
    p0 = n_pages * page
    tk = p0 + tq
    n_slc = -(-tk // SLC_BLOCK)
    n_cmp = (tk - CMP_BLOCK) // CMP_STRIDE + 1
    n_half = p0 // CMP_STRIDE
    n_top = min(N_SELECT, n_slc)
    rows_q = NSA_HPG * QSLOT
    g_n = NSA_KV_GROUPS
    wide = 2 * LANE
    assert p0 % LANE == 0 and tq <= QSLOT and tq <= SLC_BLOCK and p0 % SLC_BLOCK == 0
    assert n_cmp + 1 == n_half and n_slc <= wide and (n_slc - 1) * SLC_BLOCK == p0
    assert (p0 // LANE) % DEC_GROUP == 0 and (WINDOW // LANE) % DEC_GROUP == 0
    assert win_cache.shape[1] == WINDOW

    table = bias_table.astype(jnp.float32)
    dist_c = p0 + np.arange(QSLOT)[:, None] - (np.arange(n_half) * CMP_STRIDE + CMP_BLOCK - 1)[None, :]
    bias_c = jnp.moveaxis(table[_t5_bucket_np(dist_c)], -1, 0)
    rc = np.arange(QSLOT)[:, None] - np.arange(LANE)[None, :]
    buckets_t = np.stack([_t5_bucket_np(rc), _t5_bucket_np(rc + LANE), _t5_bucket_np(rc + 2 * LANE)])
    bias_t = jnp.moveaxis(table[buckets_t], -1, 0)
    bias_t = bias_t.reshape(g_n, NSA_HPG, 3, QSLOT, LANE).transpose(0, 2, 1, 3, 4).reshape(g_n, 3, rows_q, LANE)

    c0 = np.arange(n_cmp) * CMP_STRIDE
    s0 = np.arange(n_slc) * SLC_BLOCK
    ov = np.minimum(c0[:, None] + CMP_BLOCK, s0[None, :] + SLC_BLOCK) - np.maximum(c0[:, None], s0[None, :])
    c2s = np.zeros((n_half, wide), np.float32)
    c2s[:n_cmp, :n_slc] = np.maximum(ov, 0) / CMP_STRIDE
    n_t = p0 // LANE
    expand = np.zeros((n_t, LANE, LANE), np.float32)
    for kt in range(n_t):
        expand[kt, (kt * LANE + np.arange(LANE)) // SLC_BLOCK, np.arange(LANE)] = 1.0

    new_spec = lambda cb: pl.BlockSpec((1, tq, LANE), lambda bi, g, pt: (bi, 0, cb + g))
    full = lambda shape: pl.BlockSpec(shape, lambda bi, g, pt: (0,) * len(shape))
    grid_spec = pltpu.PrefetchScalarGridSpec(
        num_scalar_prefetch=1,
        grid=(b, g_n),
        in_specs=[
            pl.BlockSpec((1, tq, NSA_HPG * LANE), lambda bi, g, pt: (bi, 0, g)),
            new_spec(CB_KVN + 2 * g_n), new_spec(CB_KVN + 3 * g_n),
            new_spec(CB_KVN + 4 * g_n), new_spec(CB_KVN + 5 * g_n),
            pl.BlockSpec((1, WINDOW, LANE), lambda bi, g, pt: (bi, 0, g)),
            pl.BlockSpec((1, WINDOW, LANE), lambda bi, g, pt: (bi, 0, g_n + g)),
            pl.BlockSpec((1, tq, LANE), lambda bi, g, pt: (bi, 0, g)),
            pl.BlockSpec((NSA_HPG, QSLOT, n_half), lambda bi, g, pt: (g, 0, 0)),
            pl.BlockSpec((1, 3, rows_q, LANE), lambda bi, g, pt: (g, 0, 0, 0)),
            full((n_half, wide)), full((n_t, LANE, LANE)),
            full((2, CMP_BLOCK, HEAD_DIM)), full((2, CMP_BLOCK, HEAD_DIM, HEAD_DIM)), full((2, HEAD_DIM, HEAD_DIM)),
            pl.BlockSpec(memory_space=pl.ANY),
        ],
        out_specs=pl.BlockSpec((1, tq, NSA_HPG * LANE), lambda bi, g, pt: (bi, 0, g)),
        scratch_shapes=[pltpu.VMEM((p0 * NSA_STREAMS, HEAD_DIM), jnp.float32),
                        pltpu.VMEM((2, LANE, HEAD_DIM), jnp.float32),
                        pltpu.VMEM((2, WINDOW + LANE, HEAD_DIM), jnp.float32),
                        pltpu.VMEM((2, n_half, HEAD_DIM), jnp.float32),
                        pltpu.VMEM((rows_q, HEAD_DIM), jnp.float32),
                        pltpu.VMEM((QSLOT, LANE), jnp.float32),
                        pltpu.VMEM((rows_q, 1), jnp.float32),
                        pltpu.VMEM((rows_q, 1), jnp.float32),
                        pltpu.VMEM((rows_q, HEAD_DIM), jnp.float32),
                        pltpu.VMEM((3, rows_q, HEAD_DIM), jnp.float32),
                        pltpu.SemaphoreType.DMA((1,))],
    )
    return pl.pallas_call(
        functools.partial(_nsa_decode_kernel, n_pages=n_pages, page=page, tq=tq, n_slc=n_slc, n_top=n_top),
        grid_spec=grid_spec,
        out_shape=jax.ShapeDtypeStruct((b, tq, NSA_Q), jnp.float32),
        compiler_params=pltpu.CompilerParams(
            dimension_semantics=("arbitrary", "arbitrary"), vmem_limit_bytes=VMEM_LIMIT),
        name="nsa_decode",
    )(page_table, yb3, yb3, yb3, yb3, yb3, win_cache, win_cache, ys3, bias_c, bias_t,
      jnp.asarray(c2s, jnp.bfloat16), jnp.asarray(expand, jnp.bfloat16),
      cmp_pos, cmp_w1.astype(jnp.bfloat16), cmp_w2.astype(jnp.bfloat16), pool)


FOX_PAGES = 4


def _fox_decode_kernel(pt_ref, qa_ref, qb_ref, ka_ref, kb_ref, va_ref, vb_ref, cq_ref, ck_ref, *rest, n_pages, tq):
    kv_refs = rest[:FOX_PAGES]
    o_ref, q_s, new_s, m_s, l_s, acc_s = rest[FOX_PAGES:]
    p = pl.program_id(1)
    bf16 = jnp.bfloat16
    half_h = FOX_HEADS // 2

    rows = FOX_HEADS * QSLOT

    @pl.when(p == 0)
    def _():
        m_s[...] = jnp.full(m_s.shape, NEG, jnp.float32)
        l_s[...] = jnp.zeros(l_s.shape, jnp.float32)
        acc_s[...] = jnp.zeros(acc_s.shape, jnp.float32)
        q_s[...] = jnp.zeros(q_s.shape, jnp.float32)
        new_s[...] = jnp.zeros(new_s.shape, jnp.float32)
        for h in range(FOX_HEADS):
            src_q, src_k, src_v = (qa_ref, ka_ref, va_ref) if h < half_h else (qb_ref, kb_ref, vb_ref)
            lo = (h % half_h) * LANE
            q_s[h * QSLOT:h * QSLOT + tq, :] = src_q[0, :, lo:lo + LANE]
            new_s[0:tq, h * LANE:(h + 1) * LANE] = src_k[0, :, lo:lo + LANE]
            new_s[0:tq, FOX_W + h * LANE:FOX_W + (h + 1) * LANE] = src_v[0, :, lo:lo + LANE]

    def step(kv_at, ck_tile, is_new):
        qb = q_s[...].astype(bf16)
        sc = jnp.concatenate(
            [_dot_nt(qb[h * QSLOT:(h + 1) * QSLOT], kv_at(h * LANE).astype(bf16)) for h in range(FOX_HEADS)], axis=0)
        ck = jnp.concatenate([jnp.broadcast_to(ck_ref[0, h, ck_tile], (QSLOT, LANE)) for h in range(FOX_HEADS)],
                             axis=0)
        sc = sc * SCALE + cq_ref[0].reshape(rows, 1) - ck
        if is_new:
            rr = lax.broadcasted_iota(jnp.int32, (rows, LANE), 0) & (QSLOT - 1)
            cc = lax.broadcasted_iota(jnp.int32, (rows, LANE), 1)
            msk = cc <= rr
            sc = jnp.where(msk, sc, NEG)
        m_old = m_s[...]
        m_new = jnp.maximum(m_old, jnp.max(sc, -1, keepdims=True))
        pe = jnp.exp(sc - m_new)
        if is_new:
            pe = jnp.where(msk, pe, 0.0)
        alpha = jnp.exp(m_old - m_new)
        l_s[...] = alpha * l_s[...] + jnp.sum(pe, -1, keepdims=True)
        pb = pe.astype(bf16)
        pv = jnp.concatenate(
            [_dot(pb[h * QSLOT:(h + 1) * QSLOT], kv_at(FOX_W + h * LANE).astype(bf16)) for h in range(FOX_HEADS)],
            axis=0)
        acc_s[...] = alpha * acc_s[...] + pv
        m_s[...] = m_new

    @pl.when(p < n_pages // FOX_PAGES)
    def _():
        for j, kv_ref in enumerate(kv_refs):
            step(lambda col: kv_ref[0, pl.ds(col // LANE, LANE, stride=2 * FOX_HEADS), :], j, False)

    @pl.when(p == n_pages // FOX_PAGES)
    def _():
        step(lambda col: new_s[:, col:col + LANE], 0, True)
        res = acc_s[...] / jnp.maximum(l_s[...], 1e-30)
        for h in range(FOX_HEADS):
            o_ref[0, :, h * LANE:(h + 1) * LANE] = res[h * QSLOT:h * QSLOT + tq, :]


def _fox_decode(yb3, logf_new, pool, logf_past, page_table):
    b, tq, _ = yb3.shape
    n_pages = page_table.shape[1]
    page = pool.shape[1] // (2 * FOX_HEADS)
    p0 = n_pages * page
    assert page == LANE and tq <= QSLOT and n_pages % FOX_PAGES == 0
    n_steps = n_pages // FOX_PAGES
    cum = jnp.cumsum(jnp.concatenate([logf_past.astype(jnp.float32), logf_new], axis=1), axis=1)
    cum_q = jnp.pad(jnp.moveaxis(cum[:, p0:], 1, 2), ((0, 0), (0, 0), (0, QSLOT - tq)))[..., None]
    cum_k = jnp.pad(jnp.moveaxis(cum, 1, 2), ((0, 0), (0, 0), (0, FOX_PAGES * LANE - tq)))
    cum_k = cum_k.reshape(b, FOX_HEADS, (n_steps + 1) * FOX_PAGES, 1, LANE)
    page_spec = lambda j: pl.BlockSpec(
        (1, page * 2 * FOX_HEADS, HEAD_DIM),
        lambda bi, p, pt: (pt[bi, jnp.minimum(p * FOX_PAGES + j, n_pages - 1)], 0, 0))
    wq = FOX_W // 2
    blk = lambda col: pl.BlockSpec((1, tq, wq), lambda bi, p, pt: (bi, 0, col))
    base_q, base_k, base_v = CB_QF * LANE // wq, CB_KF * LANE // wq, CB_VF * LANE // wq
    grid_spec = pltpu.PrefetchScalarGridSpec(
        num_scalar_prefetch=1,
        grid=(b, n_steps + 1),
        in_specs=[blk(base_q), blk(base_q + 1), blk(base_k), blk(base_k + 1), blk(base_v), blk(base_v + 1),
                  pl.BlockSpec((1, FOX_HEADS, QSLOT, 1), lambda bi, p, pt: (bi, 0, 0, 0)),
                  pl.BlockSpec((1, FOX_HEADS, FOX_PAGES, 1, LANE), lambda bi, p, pt: (bi, 0, p, 0, 0))]
                 + [page_spec(j) for j in range(FOX_PAGES)],
        out_specs=pl.BlockSpec((1, tq, FOX_W), lambda bi, p, pt: (bi, 0, 0)),
        scratch_shapes=[pltpu.VMEM((FOX_HEADS * QSLOT, HEAD_DIM), jnp.float32),
                        pltpu.VMEM((LANE, 2 * FOX_W), jnp.float32),
                        pltpu.VMEM((FOX_HEADS * QSLOT, 1), jnp.float32),
                        pltpu.VMEM((FOX_HEADS * QSLOT, 1), jnp.float32),
                        pltpu.VMEM((FOX_HEADS * QSLOT, HEAD_DIM), jnp.float32)],
    )
    return pl.pallas_call(
        functools.partial(_fox_decode_kernel, n_pages=n_pages, tq=tq),
        grid_spec=grid_spec,
        out_shape=jax.ShapeDtypeStruct((b, tq, FOX_W), jnp.float32),
        compiler_params=pltpu.CompilerParams(
            dimension_semantics=("arbitrary", "arbitrary"), vmem_limit_bytes=VMEM_LIMIT),
        name="fox_decode",
    )(page_table, yb3, yb3, yb3, yb3, yb3, yb3, cum_q, cum_k, *([pool] * FOX_PAGES))


def sample_forward(x, caches, page_table, w_proj, cmp_pos, cmp_w1, cmp_w2, bias_table, g_nsa, g_fox, w_out,
                   ln1_g, ln1_b, peer_w_q, peer_keys, peer_uv, ln2_g, ln2_b):
    B, T, D = x.shape
    G = NSA_KV_GROUPS
    cache_nsa, cache_win, cache_fox, cache_logf = caches
    n_pool, page = cache_nsa.shape[:2]
    x2 = x.reshape(B * T, D)
    w_big, w_small, b_small = w_proj
    yb3 = _matmul(x2, w_big).reshape(B, T, BIG_WIDTH)
    ys3 = _proj_small(x2, w_small, b_small).reshape(B, T, SMALL_WIDTH)
    logf = ys3[:, :, 2 * LANE:2 * LANE + FOX_HEADS]
    kvn = yb3[:, :, CB_KVN * LANE:CB_QF * LANE].reshape(B, T, 3, 2, G, HEAD_DIM)
    kvf = yb3[:, :, CB_KF * LANE:].reshape(B, T, 2, FOX_HEADS, HEAD_DIM)

    o_n = _nsa_decode(yb3, ys3, cache_nsa.reshape(n_pool, page * NSA_STREAMS, HEAD_DIM),
                      cache_win.reshape(B, -1, 2 * NSA_KV),
                      page_table, bias_table, cmp_pos, cmp_w1, cmp_w2)
    logf_past = cache_logf[page_table].reshape(B, -1, FOX_HEADS)
    o_f = _fox_decode(yb3, logf, cache_fox.reshape(n_pool, page * 2 * FOX_HEADS, HEAD_DIM), logf_past, page_table)
    h = _post_attention(o_n.reshape(B * T, NSA_Q), o_f.reshape(B * T, FOX_W), x2,
                        g_nsa, g_fox, w_out, ln1_g, ln1_b)
    f = _peer(h, peer_w_q, peer_keys, peer_uv)
    y = _add_ln(h, f, ln2_g, ln2_b).reshape(B, T, D)
    win_buf = jnp.concatenate([cache_win[:, T:], kvn[:, :, 2]], axis=1)
    return y, kvn[:, :, :2], win_buf, kvf, logf


def prompt_forward(x, w_proj, cmp_pos, cmp_w1, cmp_w2, bias_table, g_nsa, g_fox, w_out,
                   ln1_g, ln1_b, peer_w_q, peer_keys, peer_uv, ln2_g, ln2_b):
    B, T, D = x.shape
    G = NSA_KV_GROUPS
    x2 = x.reshape(B * T, D)
    w_big, w_small, b_small = w_proj
    yb = _matmul(x2, w_big)
    ys = _proj_small(x2, w_small, b_small)
    yb3 = yb.reshape(B, T, BIG_WIDTH)
    ys3 = ys.reshape(B, T, SMALL_WIDTH)
    logf = ys3[:, :, 2 * LANE:2 * LANE + FOX_HEADS]
    kvn = yb3[:, :, CB_KVN * LANE:CB_QF * LANE].reshape(B, T, 3, 2, G, HEAD_DIM)
    kvf = yb3[:, :, CB_KF * LANE:].reshape(B, T, 2, FOX_HEADS, HEAD_DIM)

    kcvc = _compress_prompt(yb3, cmp_pos, cmp_w1, cmp_w2)
    o_n = _nsa_prompt(yb3, ys3, kcvc, bias_table)
    o_f = _fox_prompt(yb3, logf)
    h = _post_attention(o_n.reshape(B * T, NSA_Q), o_f.reshape(B * T, FOX_W), x2,
                        g_nsa, g_fox, w_out, ln1_g, ln1_b)
    f = _peer(h, peer_w_q, peer_keys, peer_uv)
    y = _add_ln(h, f, ln2_g, ln2_b).reshape(B, T, D)
    buf_len = min(WINDOW, T)
    return y, kvn[:, :, :2], kvn[:, T - buf_len:, 2], kvf, logf


def kernel(x_prompt, x_sample, cache_nsa_kv, cache_nsa_win, cache_fox_kv, cache_fox_logf, page_table,
           w_in, b_forget, nsa_cmp_pos, nsa_cmp_w1, nsa_cmp_w2, rel_bias_table, g_nsa, g_fox, w_out,
           ln1_g, ln1_b, peer_w_q, peer_sub_keys, peer_u, peer_v, ln2_g, ln2_b):
    layer = 0
    w = (_permute_w_in(w_in[layer], b_forget[layer]), nsa_cmp_pos[layer], nsa_cmp_w1[layer], nsa_cmp_w2[layer],
         rel_bias_table, g_nsa[layer], g_fox[layer], w_out[layer], ln1_g[layer], ln1_b[layer],
         peer_w_q[layer], peer_sub_keys[layer], _pack_expert_tables(peer_u[layer], peer_v[layer]),
         ln2_g[layer], ln2_b[layer])
    yp, a_nsa, a_win, a_fox, a_logf = prompt_forward(x_prompt, *w)
    caches = (cache_nsa_kv[layer], cache_nsa_win[layer], cache_fox_kv[layer], cache_fox_logf[layer])
    ys, b_nsa, b_win, b_fox, b_logf = sample_forward(x_sample, caches, page_table, *w)
    return (yp, ys, a_nsa[None], a_win[None], a_fox[None], a_logf[None],
            b_nsa[None], b_win[None], b_fox[None], b_logf[None])
```

```python
import functools
import math

import jax
import jax.numpy as jnp
import numpy as np
from jax import lax
from jax.experimental import pallas as pl
from jax.experimental.pallas import tpu as pltpu

D_MODEL = 2048
HEAD_DIM = 128
NSA_HEADS = 8
NSA_KV_GROUPS = 2
NSA_HPG = NSA_HEADS // NSA_KV_GROUPS
CMP_BLOCK = 32
CMP_STRIDE = 16
SLC_BLOCK = 64
N_SELECT = 16
WINDOW = 512
FOX_HEADS = 8
NSA_Q = NSA_HEADS * HEAD_DIM
NSA_KV = NSA_KV_GROUPS * HEAD_DIM
FOX_W = FOX_HEADS * HEAD_DIM
IN_SIZES = (NSA_Q, 6 * NSA_KV, 3 * NSA_HEADS, FOX_W, 2 * FOX_W, FOX_HEADS)
N_BUCKETS = 32
MAX_DISTANCE = 128
PEER_HEADS = 8
N_KEYS = 128
PEER_TOPK = 16
PEER_QDIM = 256
DEPTH = 1
ALPHA = (2.0 * DEPTH) ** 0.25
LN_EPS = 1e-5
NEG = -1e30
FORCE_BONUS = 1e4
SCALE = HEAD_DIM ** -0.5

LANE = 128
VMEM_LIMIT = 48 * 1024 * 1024

BIG_WIDTH = NSA_Q + 6 * NSA_KV + FOX_W + 2 * FOX_W
CB_QN = 0
CB_KVN = NSA_Q // LANE
CB_QF = CB_KVN + 6 * NSA_KV // LANE
CB_KF = CB_QF + FOX_W // LANE
CB_VF = CB_KF + FOX_W // LANE
SMALL_WIDTH = 3 * LANE


def _dot_nt(a, b):
    return lax.dot_general(a, b, (((1,), (1,)), ((), ())), preferred_element_type=jnp.float32)


def _dot(a, b):
    return jnp.dot(a, b, preferred_element_type=jnp.float32)


def _mm_kernel(x_ref, w_ref, o_ref):
    o_ref[...] = _dot(x_ref[...].astype(jnp.bfloat16), w_ref[...])


def _matmul(x, w, tm=512, tn=512):
    m, k = x.shape
    n = w.shape[1]
    tm = min(tm, m)
    n_pad = -(-n // tn) * tn
    wb = w.astype(jnp.bfloat16)
    if n_pad != n:
        wb = jnp.pad(wb, ((0, 0), (0, n_pad - n)))
    out = pl.pallas_call(
        _mm_kernel,
        grid=(m // tm, n_pad // tn),
        in_specs=[pl.BlockSpec((tm, k), lambda i, j: (i, 0)),
                  pl.BlockSpec((k, tn), lambda i, j: (0, j))],
        out_specs=pl.BlockSpec((tm, tn), lambda i, j: (i, j)),
        out_shape=jax.ShapeDtypeStruct((m, n_pad), jnp.float32),
        compiler_params=pltpu.CompilerParams(
            dimension_semantics=("parallel", "arbitrary"),
            vmem_limit_bytes=VMEM_LIMIT),
        name="dense_matmul",
    )(x, wb)
    return out[:, :n] if n_pad != n else out


def _proj_small_kernel(x_ref, w_ref, b_ref, o_ref):
    y = _dot(x_ref[...].astype(jnp.bfloat16), w_ref[...]) + b_ref[...]
    gates = y[:, :2 * LANE]
    o_ref[:, :2 * LANE] = 1.0 / (1.0 + jnp.exp(-gates))
    f = y[:, 2 * LANE:]
    o_ref[:, 2 * LANE:] = -(jnp.maximum(-f, 0.0) + jnp.log1p(jnp.exp(-jnp.abs(f))))


def _proj_small(x, w_small, b_small, tm=512):
    m, k = x.shape
    tm = min(tm, m)
    return pl.pallas_call(
        _proj_small_kernel,
        grid=(m // tm,),
        in_specs=[pl.BlockSpec((tm, k), lambda i: (i, 0)),
                  pl.BlockSpec((k, SMALL_WIDTH), lambda i: (0, 0)),
                  pl.BlockSpec((1, SMALL_WIDTH), lambda i: (0, 0))],
        out_specs=pl.BlockSpec((tm, SMALL_WIDTH), lambda i: (i, 0)),
        out_shape=jax.ShapeDtypeStruct((m, SMALL_WIDTH), jnp.float32),
        compiler_params=pltpu.CompilerParams(
            dimension_semantics=("parallel",), vmem_limit_bytes=VMEM_LIMIT),
        name="proj_small",
    )(x, w_small, b_small)


def _permute_w_in(w_in, b_forget):
    offs = [0] + [int(o) for o in np.cumsum(IN_SIZES)]
    q_n, kv_n, gate, q_f, kv_f, f_f = (w_in[:, offs[i]:offs[i + 1]] for i in range(6))
    w_big = jnp.concatenate([q_n, kv_n, q_f, kv_f], axis=1).astype(jnp.bfloat16)
    d = w_in.shape[0]
    n_gate = 3 * NSA_HPG
    zg = jnp.zeros((d, LANE - n_gate), w_in.dtype)
    zf = jnp.zeros((d, LANE - FOX_HEADS), w_in.dtype)
    w_small = jnp.concatenate([gate[:, :n_gate], zg, gate[:, n_gate:], zg, f_f, zf], axis=1).astype(jnp.bfloat16)
    b_small = jnp.concatenate([jnp.zeros((2 * LANE,), jnp.float32), b_forget.astype(jnp.float32),
                               jnp.zeros((LANE - FOX_HEADS,), jnp.float32)])[None]
    return w_big, w_small, b_small


def _gelu_tanh(h):
    return 0.5 * h * (1.0 + jnp.tanh(math.sqrt(2.0 / math.pi) * (h + 0.044715 * (h * h * h))))


def _compress_kernel(k_ref, pe_ref, w1_ref, w2_ref, o_ref, *, nh):
    def half(s0):
        acc = jnp.zeros((nh, HEAD_DIM), jnp.float32)
        for s in range(CMP_STRIDE):
            rows = k_ref[0, pl.ds(s, nh, stride=CMP_STRIDE), :] + pe_ref[0, s0 + s:s0 + s + 1, :]
            acc = acc + _dot(rows.astype(jnp.bfloat16), w1_ref[0, s0 + s])
        return acc

    first = half(0)
    second = half(CMP_STRIDE)
    h = first + pltpu.roll(second, nh - 1, 0)
    o_ref[0, 0] = _dot(_gelu_tanh(h).astype(jnp.bfloat16), w2_ref[0])


def _compress_prompt(yb3, cmp_pos, cmp_w1, cmp_w2):
    b, t, _ = yb3.shape
    nh = t // CMP_STRIDE
    n_kg = 2 * NSA_KV_GROUPS
    return pl.pallas_call(
        functools.partial(_compress_kernel, nh=nh),
        grid=(b, n_kg),
        in_specs=[pl.BlockSpec((1, t, LANE), lambda i, c: (i, 0, CB_KVN + c)),
                  pl.BlockSpec((1, CMP_BLOCK, HEAD_DIM), lambda i, c: (c // NSA_KV_GROUPS, 0, 0)),
                  pl.BlockSpec((1, CMP_BLOCK, HEAD_DIM, HEAD_DIM), lambda i, c: (c // NSA_KV_GROUPS, 0, 0, 0)),
                  pl.BlockSpec((1, HEAD_DIM, HEAD_DIM), lambda i, c: (c // NSA_KV_GROUPS, 0, 0))],
        out_specs=pl.BlockSpec((1, 1, nh, HEAD_DIM), lambda i, c: (i, c, 0, 0)),
        out_shape=jax.ShapeDtypeStruct((b, n_kg, nh, HEAD_DIM), jnp.float32),
        compiler_params=pltpu.CompilerParams(
            dimension_semantics=("parallel", "arbitrary"), vmem_limit_bytes=VMEM_LIMIT),
        name="nsa_compress",
    )(yb3, cmp_pos, cmp_w1.astype(jnp.bfloat16), cmp_w2.astype(jnp.bfloat16))


SLC_PAIR = 2


def _nsa_kernel(q_ref, kc_ref, vc_ref, ks_ref, vs_ref, kw_ref, vw_ref, gate_ref, biasc_ref, biast_ref,
                c2s_ref, expand_ref, o_ref, q_s, sel_s, m_s, l_s, acc_s, out_s, *, n_slc, n_top):
    i = pl.program_id(2)
    rows_q = NSA_HPG * LANE
    bf16 = jnp.bfloat16

    qt = q_ref[0]
    q_s[...] = jnp.concatenate([qt[:, h * LANE:(h + 1) * LANE] for h in range(NSA_HPG)], axis=0).astype(bf16)

    r = lax.broadcasted_iota(jnp.int32, (rows_q, LANE), 0) & (LANE - 1)
    c = lax.broadcasted_iota(jnp.int32, (rows_q, LANE), 1)
    q_pos = i * LANE + r
    s = _dot_nt(q_s[...], kc_ref[0, 0].astype(bf16)) * SCALE + biasc_ref[...].reshape(rows_q, LANE)
    mask = c * CMP_STRIDE + (CMP_BLOCK - 1) <= q_pos
    s = jnp.where(mask, s, NEG)
    p = jnp.where(mask, jnp.exp(s - jnp.max(s, -1, keepdims=True)), 0.0)
    p = p / jnp.maximum(jnp.sum(p, -1, keepdims=True), 1e-30)
    out_s[0] = _dot(p.astype(bf16), vc_ref[0, 0].astype(bf16))

    p_sum = p[0:LANE]
    for h in range(1, NSA_HPG):
        p_sum = p_sum + p[h * LANE:(h + 1) * LANE]
    p_hi = p_sum.astype(bf16)
    p_lo = (p_sum - p_hi.astype(jnp.float32)).astype(bf16)
    imp = _dot(p_hi, c2s_ref[...]) + _dot(p_lo, c2s_ref[...])
    jj = lax.broadcasted_iota(jnp.int32, (LANE, LANE), 1)
    qp = i * LANE + lax.broadcasted_iota(jnp.int32, (LANE, LANE), 0)
    cur = qp >> int(math.log2(SLC_BLOCK))
    forced = (jj == 0) | (jj == cur) | (jj == cur - 1)
    val = jnp.where(jj * SLC_BLOCK <= qp, imp + jnp.where(forced, FORCE_BONUS, 0.0), NEG)
    val = jnp.where(jj < n_slc, val, -3e38)
    rank = jnp.zeros((LANE, LANE), jnp.int32)
    for t in range(n_slc):
        col = val[:, t:t + 1]
        ahead = (col > val) | ((col == val) & (jj > t))
        rank = rank + ahead.astype(jnp.int32)
    sel_s[...] = jnp.where(rank < n_top, 1.0, 0.0).astype(bf16)

    def attend(k_ref, v_ref, lo, w, use_sel, use_win, slot):
        m_s[...] = jnp.full((rows_q, 1), NEG, jnp.float32)
        l_s[...] = jnp.zeros((rows_q, 1), jnp.float32)
        acc_s[...] = jnp.zeros((rows_q, HEAD_DIM), jnp.float32)

        n_tiles = i + 1 - lo

        def body(j, carry):
            k0 = lo + w * j
            off = pl.multiple_of(k0 * LANE, LANE)
            k = k_ref[0, pl.ds(off, w * LANE), :].astype(bf16)
            v = v_ref[0, pl.ds(off, w * LANE), :].astype(bf16)
            dq = i - k0
            bias = [biast_ref[0, jnp.clip(dq - t, 0, 2)] for t in range(w)]
            sc = _dot_nt(q_s[...], k) * SCALE + (bias[0] if w == 1 else jnp.concatenate(bias, axis=1))
            rr = lax.broadcasted_iota(jnp.int32, (rows_q, w * LANE), 0) & (LANE - 1)
            cc = lax.broadcasted_iota(jnp.int32, (rows_q, w * LANE), 1)
            dist = rr - cc + dq * LANE
            msk = dist >= 0
            if use_win:
                msk = msk & (dist <= WINDOW)
            if use_sel:
                se = [_dot(sel_s[...], expand_ref[k0 + t]) for t in range(w)]
                se = se[0] if w == 1 else jnp.concatenate(se, axis=1)
                msk = msk & (jnp.concatenate([se] * NSA_HPG, axis=0) > 0.5)
            sc = jnp.where(msk, sc, NEG)
            m_old = m_s[...]
            m_new = jnp.maximum(m_old, jnp.max(sc, -1, keepdims=True))
            pe = jnp.where(msk, jnp.exp(sc - m_new), 0.0)
            alpha = jnp.exp(m_old - m_new)
            l_s[...] = alpha * l_s[...] + jnp.sum(pe, -1, keepdims=True)
            acc_s[...] = alpha * acc_s[...] + _dot(pe.astype(bf16), v)
            m_s[...] = m_new
            return carry

        assert w in (1, 2)
        lax.fori_loop(0, n_tiles if w == 1 else (n_tiles + 1) >> 1, body, 0)
        out_s[slot] = acc_s[...] / jnp.maximum(l_s[...], 1e-30)

    attend(ks_ref, vs_ref, 0, SLC_PAIR, True, False, 1)
    attend(kw_ref, vw_ref, jnp.maximum(i - WINDOW // LANE, 0), 1, False, True, 2)

    gt = gate_ref[0]
    for h in range(NSA_HPG):
        sl = slice(h * LANE, (h + 1) * LANE)
        o_ref[0, :, sl] = (gt[:, 3 * h:3 * h + 1] * out_s[0, sl, :]
                           + gt[:, 3 * h + 1:3 * h + 2] * out_s[1, sl, :]
                           + gt[:, 3 * h + 2:3 * h + 3] * out_s[2, sl, :])


def _t5_bucket_np(d):
    max_exact = N_BUCKETS // 2
    d = np.maximum(d, 0)
    large = max_exact + (np.log(np.maximum(d, 1).astype(np.float32) / np.float32(max_exact))
                         / np.float32(math.log(MAX_DISTANCE / max_exact)) * (N_BUCKETS - max_exact)).astype(np.int32)
    return np.where(d < max_exact, d, np.minimum(large, N_BUCKETS - 1)).astype(np.int32)


def _nsa_prompt(yb3, ys3, kcvc, bias_table):
    b, t, _ = yb3.shape
    n_t = t // LANE
    n_slc = t // SLC_BLOCK
    n_cmp = (t - CMP_BLOCK) // CMP_STRIDE + 1
    n_top = min(N_SELECT, n_slc)
    rows_q = NSA_HPG * LANE
    assert t % LANE == 0 and n_cmp <= LANE and n_slc <= LANE and n_t % SLC_PAIR == 0

    table = bias_table.astype(jnp.float32)
    dist_c = np.arange(t)[:, None] - (np.arange(LANE) * CMP_STRIDE + CMP_BLOCK - 1)[None, :]
    bias_c = jnp.moveaxis(table[_t5_bucket_np(dist_c)], -1, 0)
    rc = np.arange(LANE)[:, None] - np.arange(LANE)[None, :]
    buckets_t = np.stack([_t5_bucket_np(rc), _t5_bucket_np(rc + LANE), _t5_bucket_np(rc + 2 * LANE)])
    assert (_t5_bucket_np(np.arange(LANE + 1, 4 * LANE)) == N_BUCKETS - 1).all()
    bias_t = jnp.moveaxis(table[buckets_t], -1, 0)
    bias_t = bias_t.reshape(NSA_KV_GROUPS, NSA_HPG, 3, LANE, LANE).transpose(0, 2, 1, 3, 4)
    bias_t = bias_t.reshape(NSA_KV_GROUPS, 3, rows_q, LANE)

    c0 = np.arange(n_cmp) * CMP_STRIDE
    s0 = np.arange(n_slc) * SLC_BLOCK
    ov = np.minimum(c0[:, None] + CMP_BLOCK, s0[None, :] + SLC_BLOCK) - np.maximum(c0[:, None], s0[None, :])
    c2s = np.zeros((LANE, LANE), np.float32)
    c2s[:n_cmp, :n_slc] = np.maximum(ov, 0) / CMP_STRIDE
    expand = np.zeros((n_t, LANE, LANE), np.float32)
    for kt in range(n_t):
        tok_blk = (kt * LANE + np.arange(LANE)) // SLC_BLOCK
        expand[kt, tok_blk, np.arange(LANE)] = 1.0

    kv_spec = lambda cb: pl.BlockSpec((1, t, LANE), lambda bi, g, i: (bi, 0, cb + g))
    g_n = NSA_KV_GROUPS
    return pl.pallas_call(
        functools.partial(_nsa_kernel, n_slc=n_slc, n_top=n_top),
        grid=(b, g_n, n_t),
        in_specs=[
            pl.BlockSpec((1, LANE, rows_q), lambda bi, g, i: (bi, i, g)),
            pl.BlockSpec((1, 1, t // CMP_STRIDE, HEAD_DIM), lambda bi, g, i: (bi, g, 0, 0)),
            pl.BlockSpec((1, 1, t // CMP_STRIDE, HEAD_DIM), lambda bi, g, i: (bi, g_n + g, 0, 0)),
            kv_spec(CB_KVN + 2 * g_n), kv_spec(CB_KVN + 3 * g_n),
            kv_spec(CB_KVN + 4 * g_n), kv_spec(CB_KVN + 5 * g_n),
            pl.BlockSpec((1, LANE, LANE), lambda bi, g, i: (bi, i, g)),
            pl.BlockSpec((NSA_HPG, LANE, LANE), lambda bi, g, i: (g, i, 0)),
            pl.BlockSpec((1, 3, rows_q, LANE), lambda bi, g, i: (g, 0, 0, 0)),
            pl.BlockSpec((LANE, LANE), lambda bi, g, i: (0, 0)),
            pl.BlockSpec((n_t, LANE, LANE), lambda bi, g, i: (0, 0, 0)),
        ],
        out_specs=pl.BlockSpec((1, LANE, rows_q), lambda bi, g, i: (bi, i, g)),
        out_shape=jax.ShapeDtypeStruct((b, t, NSA_Q), jnp.float32),
        scratch_shapes=[pltpu.VMEM((rows_q, HEAD_DIM), jnp.bfloat16),
                        pltpu.VMEM((LANE, LANE), jnp.bfloat16),
                        pltpu.VMEM((rows_q, 1), jnp.float32),
                        pltpu.VMEM((rows_q, 1), jnp.float32),
                        pltpu.VMEM((rows_q, HEAD_DIM), jnp.float32),
                        pltpu.VMEM((3, rows_q, HEAD_DIM), jnp.float32)],
        compiler_params=pltpu.CompilerParams(
            dimension_semantics=("parallel", "parallel", "arbitrary"), vmem_limit_bytes=VMEM_LIMIT),
        name="nsa_prompt",
    )(yb3, kcvc, kcvc, yb3, yb3, yb3, yb3, ys3, bias_c, bias_t,
      jnp.asarray(c2s, jnp.bfloat16), jnp.asarray(expand, jnp.bfloat16))


FOX_TILE = 256
FOX_KTILE = 512


def _fox_kernel(q_ref, k_ref, v_ref, cq_ref, ck_ref, o_ref, m_s, l_s, acc_s):
    i = pl.program_id(2)
    tq = FOX_TILE
    bf16 = jnp.bfloat16
    q = q_ref[0].astype(bf16)
    cq = cq_ref[0, 0]
    m_s[...] = jnp.full((tq, 1), NEG, jnp.float32)
    l_s[...] = jnp.zeros((tq, 1), jnp.float32)
    acc_s[...] = jnp.zeros((tq, HEAD_DIM), jnp.float32)

    tk = FOX_KTILE

    def body(kt, carry):
        off = pl.multiple_of(kt * tk, tk)
        k = k_ref[0, pl.ds(off, tk), :].astype(bf16)
        v = v_ref[0, pl.ds(off, tk), :].astype(bf16)
        sc = _dot_nt(q, k) * SCALE + cq - ck_ref[0, 0, kt]
        rr = lax.broadcasted_iota(jnp.int32, (tq, tk), 0)
        cc = lax.broadcasted_iota(jnp.int32, (tq, tk), 1)
        msk = cc + kt * tk <= rr + i * tq
        sc = jnp.where(msk, sc, NEG)
        m_old = m_s[...]
        m_new = jnp.maximum(m_old, jnp.max(sc, -1, keepdims=True))
        pe = jnp.where(msk, jnp.exp(sc - m_new), 0.0)
        alpha = jnp.exp(m_old - m_new)
        l_s[...] = alpha * l_s[...] + jnp.sum(pe, -1, keepdims=True)
        acc_s[...] = alpha * acc_s[...] + _dot(pe.astype(bf16), v)
        m_s[...] = m_new
        return carry

    lax.fori_loop(0, ((i * tq) >> int(math.log2(tk))) + 1, body, 0)
    o_ref[0] = acc_s[...] / jnp.maximum(l_s[...], 1e-30)


def _fox_prompt(yb3, logf):
    b, t, _ = yb3.shape
    tq = FOX_TILE
    n_t = t // tq
    cum = jnp.moveaxis(jnp.cumsum(logf.astype(jnp.float32), axis=1), 1, 2)
    cum_q = cum[..., None]
    tk = FOX_KTILE
    assert t % tk == 0 and tk % tq == 0
    cum_k = cum.reshape(b, FOX_HEADS, t // tk, 1, tk)
    return pl.pallas_call(
        _fox_kernel,
        grid=(b, FOX_HEADS, n_t),
        in_specs=[pl.BlockSpec((1, tq, LANE), lambda bi, h, i: (bi, i, CB_QF + h)),
                  pl.BlockSpec((1, t, LANE), lambda bi, h, i: (bi, 0, CB_KF + h)),
                  pl.BlockSpec((1, t, LANE), lambda bi, h, i: (bi, 0, CB_VF + h)),
                  pl.BlockSpec((1, 1, tq, 1), lambda bi, h, i: (bi, h, i, 0)),
                  pl.BlockSpec((1, 1, t // tk, 1, tk), lambda bi, h, i: (bi, h, 0, 0, 0))],
        out_specs=pl.BlockSpec((1, tq, LANE), lambda bi, h, i: (bi, i, h)),
        out_shape=jax.ShapeDtypeStruct((b, t, FOX_W), jnp.float32),
        scratch_shapes=[pltpu.VMEM((tq, 1), jnp.float32),
                        pltpu.VMEM((tq, 1), jnp.float32),
                        pltpu.VMEM((tq, HEAD_DIM), jnp.float32)],
        compiler_params=pltpu.CompilerParams(
            dimension_semantics=("parallel", "parallel", "arbitrary"), vmem_limit_bytes=VMEM_LIMIT),
        name="fox_prompt",
    )(yb3, yb3, yb3, cum_q, cum_k)


def _ln(z, g, b):
    mu = jnp.mean(z, -1, keepdims=True)
    zc = z - mu
    var = jnp.mean(zc * zc, -1, keepdims=True)
    return zc * lax.rsqrt(var + LN_EPS) * g + b


def _post_kernel(on_ref, of_ref, x_ref, gn_ref, gf_ref, w_ref, lg_ref, lb_ref, h_ref):
    def rms(o, g):
        return (o * lax.rsqrt(jnp.mean(o * o, -1, keepdims=True) + LN_EPS) * g).astype(jnp.bfloat16)

    mix = (_dot(rms(on_ref[...], gn_ref[...]), w_ref[:NSA_Q, :])
           + _dot(rms(of_ref[...], gf_ref[...]), w_ref[NSA_Q:, :]))
    h_ref[...] = _ln(ALPHA * x_ref[...] + mix, lg_ref[...], lb_ref[...])


def _post_attention(o_n, o_f, x, g_nsa, g_fox, w_out, ln_g, ln_b, tm=256):
    m, d = x.shape
    tm = min(tm, m)
    row = lambda n: pl.BlockSpec((1, n), lambda i: (0, 0))
    return pl.pallas_call(
        _post_kernel,
        grid=(m // tm,),
        in_specs=[pl.BlockSpec((tm, NSA_Q), lambda i: (i, 0)),
                  pl.BlockSpec((tm, FOX_W), lambda i: (i, 0)),
                  pl.BlockSpec((tm, d), lambda i: (i, 0)),
                  row(NSA_Q), row(FOX_W),
                  pl.BlockSpec((NSA_Q + FOX_W, d), lambda i: (0, 0)),
                  row(d), row(d)],
        out_specs=pl.BlockSpec((tm, d), lambda i: (i, 0)),
        out_shape=jax.ShapeDtypeStruct((m, d), jnp.float32),
        compiler_params=pltpu.CompilerParams(
            dimension_semantics=("parallel",), vmem_limit_bytes=VMEM_LIMIT),
        name="post_attention",
    )(o_n, o_f, x, g_nsa[None], g_fox[None], w_out.astype(jnp.bfloat16), ln_g[None], ln_b[None])


def _add_ln_kernel(h_ref, f_ref, g_ref, b_ref, o_ref):
    o_ref[...] = _ln(ALPHA * h_ref[...] + f_ref[...], g_ref[...], b_ref[...])


def _add_ln(h, f, ln_g, ln_b, tm=512):
    m, d = h.shape
    tm = min(tm, m)
    return pl.pallas_call(
        _add_ln_kernel,
        grid=(m // tm,),
        in_specs=[pl.BlockSpec((tm, d), lambda i: (i, 0)),
                  pl.BlockSpec((tm, d), lambda i: (i, 0)),
                  pl.BlockSpec((1, d), lambda i: (0, 0)),
                  pl.BlockSpec((1, d), lambda i: (0, 0))],
        out_specs=pl.BlockSpec((tm, d), lambda i: (i, 0)),
        out_shape=jax.ShapeDtypeStruct((m, d), jnp.float32),
        compiler_params=pltpu.CompilerParams(
            dimension_semantics=("parallel",), vmem_limit_bytes=VMEM_LIMIT),
        name="add_layer_norm",
    )(h, f, ln_g[None], ln_b[None])


PEER_TILE = 128
N_ROUTES = PEER_HEADS * PEER_TOPK


def _top_rows(vals, row_id, n_out, payload=None):
    big = float(vals.shape[0])
    out_v, out_i = [], []
    for _ in range(n_out):
        m = jnp.max(vals, axis=0, keepdims=True)
        win = jnp.min(jnp.where(vals == m, row_id, big), axis=0, keepdims=True)
        hit = row_id == win
        out_v.append(m)
        if payload is None:
            out_i.append(win)
        else:
            out_i.append(jnp.sum(jnp.where(hit, payload, 0.0), axis=0, keepdims=True))
        vals = jnp.where(hit, -jnp.inf, vals)
    return jnp.concatenate(out_v, axis=0), jnp.concatenate(out_i, axis=0)


def _peer_route_kernel(h_ref, wq_ref, keys_ref, g_ref, e_ref, sv_s, si_s):
    bf16 = jnp.bfloat16
    tm = PEER_TILE
    half = PEER_QDIM // 2
    q = _dot(h_ref[...].astype(bf16), wq_ref[...]).astype(bf16)
    key_id = lax.broadcasted_iota(jnp.int32, (N_KEYS, tm), 0).astype(jnp.float32)
    for hp in range(2 * PEER_HEADS):
        s_t = _dot_nt(keys_ref[hp], q[:, hp * half:(hp + 1) * half])
        sv, si = _top_rows(s_t, key_id, PEER_TOPK)
        sv_s[hp] = sv
        si_s[hp] = si
    pair_id = lax.broadcasted_iota(jnp.int32, (PEER_TOPK * PEER_TOPK, tm), 0).astype(jnp.float32)
    for h in range(PEER_HEADS):
        sv0, sv1 = sv_s[2 * h], sv_s[2 * h + 1]
        si0, si1 = si_s[2 * h], si_s[2 * h + 1]
        cand = jnp.concatenate([sv0[a:a + 1, :] + sv1 for a in range(PEER_TOPK)], axis=0)
        expert = jnp.concatenate([si0[a:a + 1, :] * float(N_KEYS) + si1 for a in range(PEER_TOPK)], axis=0)
        best, eid = _top_rows(cand, pair_id, PEER_TOPK, payload=expert)
        ex = jnp.exp(best - best[0:1, :])
        g_ref[0, h * PEER_TOPK:(h + 1) * PEER_TOPK, :] = ex / jnp.sum(ex, axis=0, keepdims=True)
        e_ref[0, h * PEER_TOPK:(h + 1) * PEER_TOPK, :] = eid.astype(jnp.int32)


def _peer_route(h, w_q, sub_keys):
    n, d = h.shape
    tm = PEER_TILE
    nb = n // tm
    n_hp = 2 * PEER_HEADS
    half = PEER_QDIM // 2
    out = jax.ShapeDtypeStruct((nb, N_ROUTES, tm), jnp.float32)
    return pl.pallas_call(
        _peer_route_kernel,
        grid=(nb,),
        in_specs=[pl.BlockSpec((tm, d), lambda i: (i, 0)),
                  pl.BlockSpec((d, PEER_HEADS * PEER_QDIM), lambda i: (0, 0)),
                  pl.BlockSpec((n_hp, N_KEYS, half), lambda i: (0, 0, 0))],
        out_specs=[pl.BlockSpec((1, N_ROUTES, tm), lambda i: (i, 0, 0)),
                   pl.BlockSpec((1, N_ROUTES, tm), lambda i: (i, 0, 0))],
        out_shape=[out, jax.ShapeDtypeStruct((nb, N_ROUTES, tm), jnp.int32)],
        scratch_shapes=[pltpu.VMEM((n_hp, PEER_TOPK, tm), jnp.float32),
                        pltpu.VMEM((n_hp, PEER_TOPK, tm), jnp.float32)],
        compiler_params=pltpu.CompilerParams(
            dimension_semantics=("parallel",), vmem_limit_bytes=VMEM_LIMIT),
        name="peer_route",
    )(h, w_q.astype(jnp.bfloat16), sub_keys.reshape(n_hp, N_KEYS, half).astype(jnp.bfloat16))


PACK_DTYPE = jnp.dtype("bfloat16")
PEER_SLOTS = 4
PEER_AHEAD = PEER_SLOTS - 1


def _pack_expert_tables(u, v):
    def pack(t):
        bits = lax.bitcast_convert_type(t.astype(PACK_DTYPE), jnp.uint16).astype(jnp.uint32)
        half = t.shape[1] // 2
        return bits[:, :half] | (bits[:, half:] << 16)
    return jnp.concatenate([pack(u), pack(v)], axis=1)


def _unpack_words(w):
    lo = lax.bitcast_convert_type(w << 16, jnp.float32)
    hi = lax.bitcast_convert_type(w & jnp.uint32(0xFFFF0000), jnp.float32)
    return lo, hi


def _peer_expert_kernel(e_ref, g_ref, x_ref, uv_hbm, o_ref, e_smem, buf, sem, esem):
    tm = PEER_TILE
    hw = D_MODEL // 2
    n_chunk = hw // LANE
    per_point = N_ROUTES // (2 * n_chunk)
    ids = pltpu.make_async_copy(e_ref.at[0], e_smem, esem)
    ids.start()
    ids.wait()

    def issue(t, slot, ks):
        for k in ks:
            pltpu.make_async_copy(uv_hbm.at[pl.ds(e_smem[k, t], 1), :],
                                  buf.at[slot, pl.ds(k, 1), :], sem.at[slot]).start(priority=k % 2)

    def wait_rows(slot):
        pltpu.make_async_copy(uv_hbm.at[pl.ds(0, N_ROUTES), :], buf.at[slot], sem.at[slot]).wait()

    def compute(t, slot, t_ahead):
        other = (slot + PEER_AHEAD) % PEER_SLOTS
        point = [0]

        def issue_some():
            issue(t_ahead, other, range(point[0] * per_point, (point[0] + 1) * per_point))
            point[0] += 1

        x_row = x_ref[pl.ds(t, 1), :]
        acc = None
        for c in range(n_chunk):
            lo, hi = _unpack_words(buf[slot, :, c * LANE:(c + 1) * LANE])
            term = lo * x_row[:, c * LANE:(c + 1) * LANE] + hi * x_row[:, hw + c * LANE:hw + (c + 1) * LANE]
            acc = term if acc is None else acc + term
            issue_some()
        s = jnp.sum(acc, axis=1, keepdims=True)
        gate = pltpu.roll(g_ref[0], jnp.where(t == 0, 0, tm - t), 1)[:, 0:1]
        coef = gate * _gelu_tanh(s)
        lo_out, hi_out = [], []
        for c in range(n_chunk):
            lo, hi = _unpack_words(buf[slot, :, hw + c * LANE:hw + (c + 1) * LANE])
            lo_out.append(jnp.sum(lo * coef, axis=0, keepdims=True))
            hi_out.append(jnp.sum(hi * coef, axis=0, keepdims=True))
            issue_some()
        o_ref[pl.ds(t, 1), :] = jnp.concatenate(lo_out + hi_out, axis=1)

    for j in range(PEER_AHEAD):
        issue(j, j, range(N_ROUTES))

    def body(i, carry):
        for j in range(PEER_SLOTS):
            t = PEER_SLOTS * i + j
            wait_rows(j)
            compute(t, j, jnp.minimum(t + PEER_AHEAD, tm - 1))
        return carry

    lax.fori_loop(0, tm // PEER_SLOTS, body, 0)
    for j in range(PEER_AHEAD):
        wait_rows((tm + j) % PEER_SLOTS)


def _peer_experts(h, gates, experts, uv):
    n, d = h.shape
    tm = PEER_TILE
    nb = n // tm
    return pl.pallas_call(
        _peer_expert_kernel,
        grid=(nb,),
        in_specs=[pl.BlockSpec((1, N_ROUTES, tm), lambda i: (i, 0, 0)),
                  pl.BlockSpec((1, N_ROUTES, tm), lambda i: (i, 0, 0)),
                  pl.BlockSpec((tm, d), lambda i: (i, 0)),
                  pl.BlockSpec(memory_space=pl.ANY)],
        out_specs=pl.BlockSpec((tm, d), lambda i: (i, 0)),
        out_shape=jax.ShapeDtypeStruct((n, d), jnp.float32),
        scratch_shapes=[pltpu.SMEM((N_ROUTES, tm), jnp.int32),
                        pltpu.VMEM((PEER_SLOTS, N_ROUTES, d), jnp.uint32),
                        pltpu.SemaphoreType.DMA((PEER_SLOTS,)),
                        pltpu.SemaphoreType.DMA],
        compiler_params=pltpu.CompilerParams(
            dimension_semantics=("arbitrary",), vmem_limit_bytes=VMEM_LIMIT),
        name="peer_experts",
    )(experts, gates, h, uv)


def _peer(h, w_q, sub_keys, uv):
    gates, experts = _peer_route(h, w_q, sub_keys)
    return _peer_experts(h, gates, experts, uv)


QSLOT = 8
DEC_GROUP = 4
NSA_STREAMS = 2 * 2 * NSA_KV_GROUPS


def _nsa_decode_kernel(pt_ref, q_ref, ks_new_ref, vs_new_ref, kw_new_ref, vw_new_ref, kwin_ref, vwin_ref,
                       gate_ref, biasc_ref, biast_ref, c2s_ref, expand_ref, pe_ref, w1_ref, w2_ref, pool_hbm,
                       o_ref, kbuf, nbuf, wbuf, kc_s, q_s, sel_s, m_s, l_s, acc_s, out_s, sem,
                       *, n_pages, page, tq, n_slc, n_top):
    b = pl.program_id(0)
    g = pl.program_id(1)
    bf16 = jnp.bfloat16
    rows_q = NSA_HPG * QSLOT
    p0 = n_pages * page
    n_half = p0 // CMP_STRIDE
    i_slc = p0 // LANE
    i_win = WINDOW // LANE

    def page_copy(p):
        return pltpu.make_async_copy(pool_hbm.at[pt_ref[b, p]],
                                     kbuf.at[pl.ds(p * page * NSA_STREAMS, page * NSA_STREAMS), :], sem.at[0])

    @pl.when(g == 0)
    def _():
        for p in range(n_pages):
            page_copy(p).start()
        for p in range(n_pages):
            page_copy(p).wait()

    def cached_rows(c, first, count, step=1):
        return kbuf[pl.ds(first * NSA_STREAMS + 2 * c + g, count, stride=step * NSA_STREAMS), :]

    q_s[...] = jnp.zeros((rows_q, HEAD_DIM), jnp.float32)
    qt = q_ref[0]
    for h in range(NSA_HPG):
        q_s[h * QSLOT:h * QSLOT + tq, :] = qt[:, h * LANE:(h + 1) * LANE]
    zeros_tile = jnp.zeros((LANE, HEAD_DIM), jnp.float32)
    wbuf[0, 0:WINDOW, :] = kwin_ref[0]
    wbuf[1, 0:WINDOW, :] = vwin_ref[0]
    wbuf[0, WINDOW:WINDOW + LANE, :] = zeros_tile
    wbuf[1, WINDOW:WINDOW + LANE, :] = zeros_tile
    wbuf[0, WINDOW:WINDOW + tq, :] = kw_new_ref[0]
    wbuf[1, WINDOW:WINDOW + tq, :] = vw_new_ref[0]

    for kv in range(2):
        def half(s0):
            acc = jnp.zeros((n_half, HEAD_DIM), jnp.float32)
            for s in range(CMP_STRIDE):
                rows = cached_rows(kv, s, n_half, CMP_STRIDE) + pe_ref[kv, s0 + s:s0 + s + 1, :]
                acc = acc + _dot(rows.astype(bf16), w1_ref[kv, s0 + s])
            return acc

        first = half(0)
        second = half(CMP_STRIDE)
        hmid = first + pltpu.roll(second, n_half - 1, 0)
        kc_s[kv] = _dot(_gelu_tanh(hmid).astype(bf16), w2_ref[kv])

    qb = q_s[...].astype(bf16)
    r = lax.broadcasted_iota(jnp.int32, (rows_q, n_half), 0) & (QSLOT - 1)
    c = lax.broadcasted_iota(jnp.int32, (rows_q, n_half), 1)
    s = _dot_nt(qb, kc_s[0].astype(bf16)) * SCALE + biasc_ref[...].reshape(rows_q, n_half)
    mask = c * CMP_STRIDE + (CMP_BLOCK - 1) <= p0 + r
    s = jnp.where(mask, s, NEG)
    p = jnp.where(mask, jnp.exp(s - jnp.max(s, -1, keepdims=True)), 0.0)
    p = p / jnp.maximum(jnp.sum(p, -1, keepdims=True), 1e-30)
    out_s[0] = _dot(p.astype(bf16), kc_s[1].astype(bf16))

    p_sum = p[0:QSLOT]
    for h in range(1, NSA_HPG):
        p_sum = p_sum + p[h * QSLOT:(h + 1) * QSLOT]
    p_hi = p_sum.astype(bf16)
    p_lo = (p_sum - p_hi.astype(jnp.float32)).astype(bf16)
    imp = _dot(p_hi, c2s_ref[...]) + _dot(p_lo, c2s_ref[...])
    wide = 2 * LANE
    jj = lax.broadcasted_iota(jnp.int32, (QSLOT, wide), 1)
    qp = p0 + lax.broadcasted_iota(jnp.int32, (QSLOT, wide), 0)
    cur = qp >> int(math.log2(SLC_BLOCK))
    forced = (jj == 0) | (jj == cur) | (jj == cur - 1)
    val = jnp.where(jj * SLC_BLOCK <= qp, imp + jnp.where(forced, FORCE_BONUS, 0.0), NEG)
    val = jnp.where(jj < n_slc, val, -3e38)
    rank = jnp.zeros((QSLOT, wide), jnp.int32)
    for t in range(n_slc):
        col = val[:, t:t + 1]
        ahead = (col > val) | ((col == val) & (jj > t))
        rank = rank + ahead.astype(jnp.int32)
    sel_s[...] = jnp.where(rank < n_top, 1.0, 0.0)[:, :LANE]

    def attend(kv_at, i_tile, use_sel, use_win, slot):
        m_s[...] = jnp.full((rows_q, 1), NEG, jnp.float32)
        l_s[...] = jnp.zeros((rows_q, 1), jnp.float32)
        acc_s[...] = jnp.zeros((rows_q, HEAD_DIM), jnp.float32)

        def tiles(k0, w, with_sel):
            k, v = kv_at(k0, w)
            k = k.astype(bf16)
            v = v.astype(bf16)
            bias = [biast_ref[0, jnp.minimum(i_tile - k0 - j, 2)] for j in range(w)]
            sc = _dot_nt(qb, k) * SCALE + (bias[0] if w == 1 else jnp.concatenate(bias, axis=1))
            rr = lax.broadcasted_iota(jnp.int32, (rows_q, w * LANE), 0) & (QSLOT - 1)
            cc = lax.broadcasted_iota(jnp.int32, (rows_q, w * LANE), 1)
            dist = rr - cc + (i_tile - k0) * LANE
            msk = dist >= 0
            if use_win:
                msk = msk & (dist <= WINDOW)
            if with_sel:
                sb = sel_s[...].astype(bf16)
                se = jnp.concatenate([_dot(sb, expand_ref[k0 + j]) for j in range(w)], axis=1)
                msk = msk & (jnp.concatenate([se] * NSA_HPG, axis=0) > 0.5)
            sc = jnp.where(msk, sc, NEG)
            m_old = m_s[...]
            m_new = jnp.maximum(m_old, jnp.max(sc, -1, keepdims=True))
            pe = jnp.where(msk, jnp.exp(sc - m_new), 0.0)
            alpha = jnp.exp(m_old - m_new)
            l_s[...] = alpha * l_s[...] + jnp.sum(pe, -1, keepdims=True)
            acc_s[...] = alpha * acc_s[...] + _dot(pe.astype(bf16), v)
            m_s[...] = m_new

        def body(grp, carry):
            tiles(grp * DEC_GROUP, DEC_GROUP, use_sel)
            return carry

        lax.fori_loop(0, i_tile // DEC_GROUP, body, 0)
        tiles(i_tile, 1, False)
        out_s[slot] = acc_s[...] / jnp.maximum(l_s[...], 1e-30)

    nbuf[0] = zeros_tile
    nbuf[1] = zeros_tile
    nbuf[0, 0:tq, :] = ks_new_ref[0]
    nbuf[1, 0:tq, :] = vs_new_ref[0]

    def slc_at(k0, w):
        if w == 1:
            return nbuf[0], nbuf[1]
        return cached_rows(2, k0 * LANE, w * LANE), cached_rows(3, k0 * LANE, w * LANE)

    def win_at(k0, w):
        off = pl.multiple_of(k0 * LANE, LANE)
        return wbuf[0, pl.ds(off, w * LANE), :], wbuf[1, pl.ds(off, w * LANE), :]

    attend(slc_at, i_slc, True, False, 1)
    attend(win_at, i_win, False, True, 2)

    gt = gate_ref[0]
    for h in range(NSA_HPG):
        sl = slice(h * QSLOT, h * QSLOT + tq)
        o_ref[0, :, h * LANE:(h + 1) * LANE] = (gt[:, 3 * h:3 * h + 1] * out_s[0, sl, :]
                                                 + gt[:, 3 * h + 1:3 * h + 2] * out_s[1, sl, :]
                                                 + gt[:, 3 * h + 2:3 * h + 3] * out_s[2, sl, :])


def _nsa_decode(yb3, ys3, pool, win_cache, page_table, bias_table, cmp_pos, cmp_w1, cmp_w2):
    b, tq, _ = yb3.shape
    n_pages = page_table.shape[1]
    page = pool.shape[1] // NSA_STREAMS
    p0 = n_pages * page
    tk = p0 + tq
    n_slc = -(-tk // SLC_BLOCK)
    n_cmp = (tk - CMP_BLOCK) // CMP_STRIDE + 1
    n_half = p0 // CMP_STRIDE
    n_top = min(N_SELECT, n_slc)
    rows_q = NSA_HPG * QSLOT
    g_n = NSA_KV_GROUPS
    wide = 2 * LANE
    assert p0 % LANE == 0 and tq <= QSLOT and tq <= SLC_BLOCK and p0 % SLC_BLOCK == 0
    assert n_cmp + 1 == n_half and n_slc <= wide and (n_slc - 1) * SLC_BLOCK == p0
    assert (p0 // LANE) % DEC_GROUP == 0 and (WINDOW // LANE) % DEC_GROUP == 0
    assert win_cache.shape[1] == WINDOW

    table = bias_table.astype(jnp.float32)
    dist_c = p0 + np.arange(QSLOT)[:, None] - (np.arange(n_half) * CMP_STRIDE + CMP_BLOCK - 1)[None, :]
    bias_c = jnp.moveaxis(table[_t5_bucket_np(dist_c)], -1, 0)
    rc = np.arange(QSLOT)[:, None] - np.arange(LANE)[None, :]
    buckets_t = np.stack([_t5_bucket_np(rc), _t5_bucket_np(rc + LANE), _t5_bucket_np(rc + 2 * LANE)])
    bias_t = jnp.moveaxis(table[buckets_t], -1, 0)
    bias_t = bias_t.reshape(g_n, NSA_HPG, 3, QSLOT, LANE).transpose(0, 2, 1, 3, 4).reshape(g_n, 3, rows_q, LANE)

    c0 = np.arange(n_cmp) * CMP_STRIDE
    s0 = np.arange(n_slc) * SLC_BLOCK
    ov = np.minimum(c0[:, None] + CMP_BLOCK, s0[None, :] + SLC_BLOCK) - np.maximum(c0[:, None], s0[None, :])
    c2s = np.zeros((n_half, wide), np.float32)
    c2s[:n_cmp, :n_slc] = np.maximum(ov, 0) / CMP_STRIDE
    n_t = p0 // LANE
    expand = np.zeros((n_t, LANE, LANE), np.float32)
    for kt in range(n_t):
        expand[kt, (kt * LANE + np.arange(LANE)) // SLC_BLOCK, np.arange(LANE)] = 1.0

    new_spec = lambda cb: pl.BlockSpec((1, tq, LANE), lambda bi, g, pt: (bi, 0, cb + g))
    full = lambda shape: pl.BlockSpec(shape, lambda bi, g, pt: (0,) * len(shape))
    grid_spec = pltpu.PrefetchScalarGridSpec(
        num_scalar_prefetch=1,
        grid=(b, g_n),
        in_specs=[
            pl.BlockSpec((1, tq, NSA_HPG * LANE), lambda bi, g, pt: (bi, 0, g)),
            new_spec(CB_KVN + 2 * g_n), new_spec(CB_KVN + 3 * g_n),
            new_spec(CB_KVN + 4 * g_n), new_spec(CB_KVN + 5 * g_n),
            pl.BlockSpec((1, WINDOW, LANE), lambda bi, g, pt: (bi, 0, g)),
            pl.BlockSpec((1, WINDOW, LANE), lambda bi, g, pt: (bi, 0, g_n + g)),
            pl.BlockSpec((1, tq, LANE), lambda bi, g, pt: (bi, 0, g)),
            pl.BlockSpec((NSA_HPG, QSLOT, n_half), lambda bi, g, pt: (g, 0, 0)),
            pl.BlockSpec((1, 3, rows_q, LANE), lambda bi, g, pt: (g, 0, 0, 0)),
            full((n_half, wide)), full((n_t, LANE, LANE)),
            full((2, CMP_BLOCK, HEAD_DIM)), full((2, CMP_BLOCK, HEAD_DIM, HEAD_DIM)), full((2, HEAD_DIM, HEAD_DIM)),
            pl.BlockSpec(memory_space=pl.ANY),
        ],
        out_specs=pl.BlockSpec((1, tq, NSA_HPG * LANE), lambda bi, g, pt: (bi, 0, g)),
        scratch_shapes=[pltpu.VMEM((p0 * NSA_STREAMS, HEAD_DIM), jnp.float32),
                        pltpu.VMEM((2, LANE, HEAD_DIM), jnp.float32),
                        pltpu.VMEM((2, WINDOW + LANE, HEAD_DIM), jnp.float32),
                        pltpu.VMEM((2, n_half, HEAD_DIM), jnp.float32),
                        pltpu.VMEM((rows_q, HEAD_DIM), jnp.float32),
                        pltpu.VMEM((QSLOT, LANE), jnp.float32),
                        pltpu.VMEM((rows_q, 1), jnp.float32),
                        pltpu.VMEM((rows_q, 1), jnp.float32),
                        pltpu.VMEM((rows_q, HEAD_DIM), jnp.float32),
                        pltpu.VMEM((3, rows_q, HEAD_DIM), jnp.float32),
                        pltpu.SemaphoreType.DMA((1,))],
    )
    return pl.pallas_call(
        functools.partial(_nsa_decode_kernel, n_pages=n_pages, page=page, tq=tq, n_slc=n_slc, n_top=n_top),
        grid_spec=grid_spec,
        out_shape=jax.ShapeDtypeStruct((b, tq, NSA_Q), jnp.float32),
        compiler_params=pltpu.CompilerParams(
            dimension_semantics=("arbitrary", "arbitrary"), vmem_limit_bytes=VMEM_LIMIT),
        name="nsa_decode",
    )(page_table, yb3, yb3, yb3, yb3, yb3, win_cache, win_cache, ys3, bias_c, bias_t,
      jnp.asarray(c2s, jnp.bfloat16), jnp.asarray(expand, jnp.bfloat16),
      cmp_pos, cmp_w1.astype(jnp.bfloat16), cmp_w2.astype(jnp.bfloat16), pool)


FOX_PAGES = 4


def _fox_decode_kernel(pt_ref, qa_ref, qb_ref, ka_ref, kb_ref, va_ref, vb_ref, cq_ref, ck_ref, *rest, n_pages, tq):
    kv_refs = rest[:FOX_PAGES]
    o_ref, q_s, new_s, m_s, l_s, acc_s = rest[FOX_PAGES:]
    p = pl.program_id(1)
    bf16 = jnp.bfloat16
    half_h = FOX_HEADS // 2

    rows = FOX_HEADS * QSLOT

    @pl.when(p == 0)
    def _():
        m_s[...] = jnp.full(m_s.shape, NEG, jnp.float32)
        l_s[...] = jnp.zeros(l_s.shape, jnp.float32)
        acc_s[...] = jnp.zeros(acc_s.shape, jnp.float32)
        q_s[...] = jnp.zeros(q_s.shape, jnp.float32)
        new_s[...] = jnp.zeros(new_s.shape, jnp.float32)
        for h in range(FOX_HEADS):
            src_q, src_k, src_v = (qa_ref, ka_ref, va_ref) if h < half_h else (qb_ref, kb_ref, vb_ref)
            lo = (h % half_h) * LANE
            q_s[h * QSLOT:h * QSLOT + tq, :] = src_q[0, :, lo:lo + LANE]
            new_s[0:tq, h * LANE:(h + 1) * LANE] = src_k[0, :, lo:lo + LANE]
            new_s[0:tq, FOX_W + h * LANE:FOX_W + (h + 1) * LANE] = src_v[0, :, lo:lo + LANE]

    def step(kv_at, ck_tile, is_new):
        qb = q_s[...].astype(bf16)
        sc = jnp.concatenate(
            [_dot_nt(qb[h * QSLOT:(h + 1) * QSLOT], kv_at(h * LANE).astype(bf16)) for h in range(FOX_HEADS)], axis=0)
        ck = jnp.concatenate([jnp.broadcast_to(ck_ref[0, h, ck_tile], (QSLOT, LANE)) for h in range(FOX_HEADS)],
                             axis=0)
        sc = sc * SCALE + cq_ref[0].reshape(rows, 1) - ck
        if is_new:
            rr = lax.broadcasted_iota(jnp.int32, (rows, LANE), 0) & (QSLOT - 1)
            cc = lax.broadcasted_iota(jnp.int32, (rows, LANE), 1)
            msk = cc <= rr
            sc = jnp.where(msk, sc, NEG)
        m_old = m_s[...]
        m_new = jnp.maximum(m_old, jnp.max(sc, -1, keepdims=True))
        pe = jnp.exp(sc - m_new)
        if is_new:
            pe = jnp.where(msk, pe, 0.0)
        alpha = jnp.exp(m_old - m_new)
        l_s[...] = alpha * l_s[...] + jnp.sum(pe, -1, keepdims=True)
        pb = pe.astype(bf16)
        pv = jnp.concatenate(
            [_dot(pb[h * QSLOT:(h + 1) * QSLOT], kv_at(FOX_W + h * LANE).astype(bf16)) for h in range(FOX_HEADS)],
            axis=0)
        acc_s[...] = alpha * acc_s[...] + pv
        m_s[...] = m_new

    @pl.when(p < n_pages // FOX_PAGES)
    def _():
        for j, kv_ref in enumerate(kv_refs):
            step(lambda col: kv_ref[0, pl.ds(col // LANE, LANE, stride=2 * FOX_HEADS), :], j, False)

    @pl.when(p == n_pages // FOX_PAGES)
    def _():
        step(lambda col: new_s[:, col:col + LANE], 0, True)
        res = acc_s[...] / jnp.maximum(l_s[...], 1e-30)
        for h in range(FOX_HEADS):
            o_ref[0, :, h * LANE:(h + 1) * LANE] = res[h * QSLOT:h * QSLOT + tq, :]


def _fox_decode(yb3, logf_new, pool, logf_past, page_table):
    b, tq, _ = yb3.shape
    n_pages = page_table.shape[1]
    page = pool.shape[1] // (2 * FOX_HEADS)
    p0 = n_pages * page
    assert page == LANE and tq <= QSLOT and n_pages % FOX_PAGES == 0
    n_steps = n_pages // FOX_PAGES
    cum = jnp.cumsum(jnp.concatenate([logf_past.astype(jnp.float32), logf_new], axis=1), axis=1)
    cum_q = jnp.pad(jnp.moveaxis(cum[:, p0:], 1, 2), ((0, 0), (0, 0), (0, QSLOT - tq)))[..., None]
    cum_k = jnp.pad(jnp.moveaxis(cum, 1, 2), ((0, 0), (0, 0), (0, FOX_PAGES * LANE - tq)))
    cum_k = cum_k.reshape(b, FOX_HEADS, (n_steps + 1) * FOX_PAGES, 1, LANE)
    page_spec = lambda j: pl.BlockSpec(
        (1, page * 2 * FOX_HEADS, HEAD_DIM),
        lambda bi, p, pt: (pt[bi, jnp.minimum(p * FOX_PAGES + j, n_pages - 1)], 0, 0))
    wq = FOX_W // 2
    blk = lambda col: pl.BlockSpec((1, tq, wq), lambda bi, p, pt: (bi, 0, col))
    base_q, base_k, base_v = CB_QF * LANE // wq, CB_KF * LANE // wq, CB_VF * LANE // wq
    grid_spec = pltpu.PrefetchScalarGridSpec(
        num_scalar_prefetch=1,
        grid=(b, n_steps + 1),
        in_specs=[blk(base_q), blk(base_q + 1), blk(base_k), blk(base_k + 1), blk(base_v), blk(base_v + 1),
                  pl.BlockSpec((1, FOX_HEADS, QSLOT, 1), lambda bi, p, pt: (bi, 0, 0, 0)),
                  pl.BlockSpec((1, FOX_HEADS, FOX_PAGES, 1, LANE), lambda bi, p, pt: (bi, 0, p, 0, 0))]
                 + [page_spec(j) for j in range(FOX_PAGES)],
        out_specs=pl.BlockSpec((1, tq, FOX_W), lambda bi, p, pt: (bi, 0, 0)),
        scratch_shapes=[pltpu.VMEM((FOX_HEADS * QSLOT, HEAD_DIM), jnp.float32),
                        pltpu.VMEM((LANE, 2 * FOX_W), jnp.float32),
                        pltpu.VMEM((FOX_HEADS * QSLOT, 1), jnp.float32),
                        pltpu.VMEM((FOX_HEADS * QSLOT, 1), jnp.float32),
                        pltpu.VMEM((FOX_HEADS * QSLOT, HEAD_DIM), jnp.float32)],
    )
    return pl.pallas_call(
        functools.partial(_fox_decode_kernel, n_pages=n_pages, tq=tq),
        grid_spec=grid_spec,
        out_shape=jax.ShapeDtypeStruct((b, tq, FOX_W), jnp.float32),
        compiler_params=pltpu.CompilerParams(
            dimension_semantics=("arbitrary", "arbitrary"), vmem_limit_bytes=VMEM_LIMIT),
        name="fox_decode",
    )(page_table, yb3, yb3, yb3, yb3, yb3, yb3, cum_q, cum_k, *([pool] * FOX_PAGES))


def sample_forward(x, caches, page_table, w_proj, cmp_pos, cmp_w1, cmp_w2, bias_table, g_nsa, g_fox, w_out,
                   ln1_g, ln1_b, peer_w_q, peer_keys, peer_uv, ln2_g, ln2_b):
    B, T, D = x.shape
    G = NSA_KV_GROUPS
    cache_nsa, cache_win, cache_fox, cache_logf = caches
    n_pool, page = cache_nsa.shape[:2]
    x2 = x.reshape(B * T, D)
    w_big, w_small, b_small = w_proj
    yb3 = _matmul(x2, w_big).reshape(B, T, BIG_WIDTH)
    ys3 = _proj_small(x2, w_small, b_small).reshape(B, T, SMALL_WIDTH)
    logf = ys3[:, :, 2 * LANE:2 * LANE + FOX_HEADS]
    kvn = yb3[:, :, CB_KVN * LANE:CB_QF * LANE].reshape(B, T, 3, 2, G, HEAD_DIM)
    kvf = yb3[:, :, CB_KF * LANE:].reshape(B, T, 2, FOX_HEADS, HEAD_DIM)

    o_n = _nsa_decode(yb3, ys3, cache_nsa.reshape(n_pool, page * NSA_STREAMS, HEAD_DIM),
                      cache_win.reshape(B, -1, 2 * NSA_KV),
                      page_table, bias_table, cmp_pos, cmp_w1, cmp_w2)
    logf_past = cache_logf[page_table].reshape(B, -1, FOX_HEADS)
    o_f = _fox_decode(yb3, logf, cache_fox.reshape(n_pool, page * 2 * FOX_HEADS, HEAD_DIM), logf_past, page_table)
    h = _post_attention(o_n.reshape(B * T, NSA_Q), o_f.reshape(B * T, FOX_W), x2,
                        g_nsa, g_fox, w_out, ln1_g, ln1_b)
    f = _peer(h, peer_w_q, peer_keys, peer_uv)
    y = _add_ln(h, f, ln2_g, ln2_b).reshape(B, T, D)
    win_buf = jnp.concatenate([cache_win[:, T:], kvn[:, :, 2]], axis=1)
    return y, kvn[:, :, :2], win_buf, kvf, logf


def prompt_forward(x, w_proj, cmp_pos, cmp_w1, cmp_w2, bias_table, g_nsa, g_fox, w_out,
                   ln1_g, ln1_b, peer_w_q, peer_keys, peer_uv, ln2_g, ln2_b):
    B, T, D = x.shape
    G = NSA_KV_GROUPS
    x2 = x.reshape(B * T, D)
    w_big, w_small, b_small = w_proj
    yb = _matmul(x2, w_big)
    ys = _proj_small(x2, w_small, b_small)
    yb3 = yb.reshape(B, T, BIG_WIDTH)
    ys3 = ys.reshape(B, T, SMALL_WIDTH)
    logf = ys3[:, :, 2 * LANE:2 * LANE + FOX_HEADS]
    kvn = yb3[:, :, CB_KVN * LANE:CB_QF * LANE].reshape(B, T, 3, 2, G, HEAD_DIM)
    kvf = yb3[:, :, CB_KF * LANE:].reshape(B, T, 2, FOX_HEADS, HEAD_DIM)

    kcvc = _compress_prompt(yb3, cmp_pos, cmp_w1, cmp_w2)
    o_n = _nsa_prompt(yb3, ys3, kcvc, bias_table)
    o_f = _fox_prompt(yb3, logf)
    h = _post_attention(o_n.reshape(B * T, NSA_Q), o_f.reshape(B * T, FOX_W), x2,
                        g_nsa, g_fox, w_out, ln1_g, ln1_b)
    f = _peer(h, peer_w_q, peer_keys, peer_uv)
    y = _add_ln(h, f, ln2_g, ln2_b).reshape(B, T, D)
    buf_len = min(WINDOW, T)
    return y, kvn[:, :, :2], kvn[:, T - buf_len:, 2], kvf, logf


def kernel(x_prompt, x_sample, cache_nsa_kv, cache_nsa_win, cache_fox_kv, cache_fox_logf, page_table,
           w_in, b_forget, nsa_cmp_pos, nsa_cmp_w1, nsa_cmp_w2, rel_bias_table, g_nsa, g_fox, w_out,
           ln1_g, ln1_b, peer_w_q, peer_sub_keys, peer_u, peer_v, ln2_g, ln2_b):
    layer = 0
    w = (_permute_w_in(w_in[layer], b_forget[layer]), nsa_cmp_pos[layer], nsa_cmp_w1[layer], nsa_cmp_w2[layer],
         rel_bias_table, g_nsa[layer], g_fox[layer], w_out[layer], ln1_g[layer], ln1_b[layer],
         peer_w_q[layer], peer_sub_keys[layer], _pack_expert_tables(peer_u[layer], peer_v[layer]),
         ln2_g[layer], ln2_b[layer])
    yp, a_nsa, a_win, a_fox, a_logf = prompt_forward(x_prompt, *w)
    caches = (cache_nsa_kv[layer], cache_nsa_win[layer], cache_fox_kv[layer], cache_fox_logf[layer])
    ys, b_nsa, b_win, b_fox, b_logf = sample_forward(x_sample, caches, page_table, *w)
    return (yp, ys, a_nsa[None], a_win[None], a_fox[None], a_logf[None],
            b_nsa[None], b_win[None], b_fox[None], b_logf[None])
```
